```python
import math
import jax
import jax.numpy as jnp
from jax import lax
import numpy as np

D_MODEL = 1024
BATCH = 32
SEQ = 256
DEPTH = 4
DEC_BATCH = 4
DEC_SEQ = 2048
PAST_LEN = 512

GRID_W = 64
N_MIXERS = 4
HEAD_DIM = 64
ROPE_THETA = 10000.0
Q_BLOCK = 128
WINDOW = 128
EPS = 1e-6
NEG_INF = -1e30
A_HEADS = D_MODEL // (2 * HEAD_DIM)
A_VDIM = 2 * HEAD_DIM
GQA_HEADS = D_MODEL // HEAD_DIM
GQA_KV_HEADS = GQA_HEADS // 4
GQA_GROUP = GQA_HEADS // GQA_KV_HEADS
D_KDIM = 128
D_HEADS = D_MODEL // D_KDIM
D_VDIM = D_MODEL // D_HEADS
D_FDIM = D_HEADS * D_KDIM
CHUNK = 64
D_FF = -(-8 * D_MODEL // (3 * 256)) * 256
N_LAYERS_A = (DEPTH + 3) // 4
N_LAYERS_B = (DEPTH + 2) // 4
N_LAYERS_C = (DEPTH + 1) // 4
N_LAYERS_D = DEPTH // 4

kernel_name = 'hybrid_diffusion_prefix_trunk_step'


def rms_norm(x, gain):
    xf = x.astype(jnp.float32)
    y = xf * lax.rsqrt(jnp.mean(xf * xf, axis=-1, keepdims=True) + EPS)
    return (y * gain.astype(jnp.float32)).astype(x.dtype)


def modulate(h, shift, scale):
    return h * (1 + scale[:, None]) + shift[:, None]


def axial_rope_tables(n_tokens):
    rows = n_tokens // GRID_W
    row = jnp.repeat(jnp.arange(rows, dtype=jnp.float32), GRID_W)
    col = jnp.tile(jnp.arange(GRID_W, dtype=jnp.float32), rows)
    axis_dim = HEAD_DIM // 2
    inv_freq = ROPE_THETA ** (-jnp.arange(0, axis_dim, 2, dtype=jnp.float32) / axis_dim)
    ang = jnp.stack([row[:, None] * inv_freq, col[:, None] * inv_freq], axis=1)
    return jnp.cos(ang), jnp.sin(ang)


def apply_rope(x, cos, sin):
    b, t, h, dh = x.shape
    xr = x.reshape(b, t, h, 2, 2, dh // 4)
    x1, x2 = xr[..., 0, :], xr[..., 1, :]
    cs = cos[None, :, None].astype(x.dtype)
    sn = sin[None, :, None].astype(x.dtype)
    out = jnp.stack([x1 * cs - x2 * sn, x2 * cs + x1 * sn], axis=-2)
    return out.reshape(b, t, h, dh)


def sweep_query_blocks(fn, q):
    b, t = q.shape[:2]
    nb = t // Q_BLOCK
    qb = jnp.moveaxis(q.reshape((b, nb, Q_BLOCK) + q.shape[2:]), 1, 0)
    out = lax.map(lambda a: fn(a[0], a[1]), (qb, jnp.arange(nb)))
    return jnp.moveaxis(out, 0, 1).reshape((b, t) + out.shape[3:])


def gqa_scores(qb, k):
    return jnp.einsum('bqgrd,bsgd->bgrqs', qb, k).astype(jnp.float32) * (HEAD_DIM ** -0.5)


def gqa_values(p, v):
    return jnp.einsum('bgrqs,bsgd->bqgrd', p.astype(v.dtype), v)


def softmax_with_sink(s, sink):
    sk = jnp.broadcast_to(sink[None, :, :, None, None], s.shape[:-1] + (1,))
    return jax.nn.softmax(jnp.concatenate([sk, s], axis=-1), axis=-1)[..., 1:]


def dense_gqa(q, k, v, sink):
    def block(qb, _):
        s = gqa_scores(qb, k)
        p = jax.nn.softmax(s, axis=-1) if sink is None else softmax_with_sink(s, sink)
        return gqa_values(p, v)
    return sweep_query_blocks(block, q)


def project_gqa(h, w_qkv, q_gain, k_gain, rope):
    b, t, _ = h.shape
    q, k, v = jnp.split(h @ w_qkv, [GQA_HEADS * HEAD_DIM, (GQA_HEADS + GQA_KV_HEADS) * HEAD_DIM], axis=-1)
    q = rms_norm(q.reshape(b, t, GQA_HEADS, HEAD_DIM), q_gain)
    k = rms_norm(k.reshape(b, t, GQA_KV_HEADS, HEAD_DIM), k_gain)
    v = v.reshape(b, t, GQA_KV_HEADS, HEAD_DIM)
    if rope is not None:
        q = apply_rope(q, *rope)
        k = apply_rope(k, *rope)
    return q.reshape(b, t, GQA_KV_HEADS, GQA_GROUP, HEAD_DIM), k, v


def mixer_diff_attn(h, w_qkv, w_o, q_gain, k_gain, sub_gain, lam, lam_init, rope, ctx_k, ctx_v):
    b, t, _ = h.shape
    q, k, v = jnp.split(h @ w_qkv, [D_MODEL, 2 * D_MODEL], axis=-1)
    q = rms_norm(q.reshape(b, t, 2 * A_HEADS, HEAD_DIM), q_gain)
    k = rms_norm(k.reshape(b, t, 2 * A_HEADS, HEAD_DIM), k_gain)
    v = v.reshape(b, t, A_HEADS, A_VDIM)
    if rope is not None:
        q = apply_rope(q, *rope)
        k = apply_rope(k, *rope)
    keys, vals = (k, v) if ctx_k is None else (jnp.concatenate([ctx_k, k], axis=1), jnp.concatenate([ctx_v, v], axis=1))
    keys = keys.reshape(b, keys.shape[1], A_HEADS, 2, HEAD_DIM)
    qh = q.reshape(b, t, A_HEADS, 2, HEAD_DIM)

    def block(qb, _):
        s = jnp.einsum('bqhmd,bkhmd->bhmqk', qb, keys).astype(jnp.float32) * (HEAD_DIM ** -0.5)
        p = jax.nn.softmax(s, axis=-1)
        a = p[:, :, 0] - lam * p[:, :, 1]
        return jnp.einsum('bhqk,bkhe->bqhe', a.astype(vals.dtype), vals)

    o = sweep_query_blocks(block, qh)
    o = rms_norm(o, sub_gain) * (1.0 - lam_init)
    return o.reshape(b, t, D_MODEL) @ w_o, k, v


def mixer_window_sink(h, w_qkv, w_o, q_gain, k_gain, sink, rope, ctx_k, ctx_v):
    b, t, _ = h.shape
    q, k, v = project_gqa(h, w_qkv, q_gain, k_gain, rope)
    sink = sink.astype(jnp.float32).reshape(GQA_KV_HEADS, GQA_GROUP)
    if ctx_k is None:
        o = dense_gqa(q, k, v, sink)
    else:
        n_ctx = ctx_k.shape[1]
        pad = ((0, 0), (Q_BLOCK, Q_BLOCK), (0, 0), (0, 0))
        kp, vp = jnp.pad(k, pad), jnp.pad(v, pad)
        offs = jnp.arange(3 * Q_BLOCK) - Q_BLOCK
        qi = jnp.arange(Q_BLOCK)

        def block(qb, j):
            kb = lax.dynamic_slice_in_dim(kp, j * Q_BLOCK, 3 * Q_BLOCK, axis=1)
            vb = lax.dynamic_slice_in_dim(vp, j * Q_BLOCK, 3 * Q_BLOCK, axis=1)
            qpos = j * Q_BLOCK + qi
            kpos = j * Q_BLOCK + offs
            valid = (jnp.abs(qpos[:, None] - kpos[None, :]) <= WINDOW) & ((kpos >= 0) & (kpos < t))[None, :]
            s_band = jnp.where(valid, gqa_scores(qb, kb), NEG_INF)
            s_ctx = gqa_scores(qb, ctx_k)
            p = softmax_with_sink(jnp.concatenate([s_ctx, s_band], axis=-1), sink)
            return gqa_values(p[..., :n_ctx], ctx_v) + gqa_values(p[..., n_ctx:], vb)

        o = sweep_query_blocks(block, q)
    return o.reshape(b, t, D_MODEL) @ w_o, k, v


def mixer_axial_gqa(h, w_qkv, w_o, q_gain, k_gain, rope, ctx_k, ctx_v):
    b, t, _ = h.shape
    q, k, v = project_gqa(h, w_qkv, q_gain, k_gain, rope)
    keys, vals = (k, v) if ctx_k is None else (jnp.concatenate([ctx_k, k], axis=1), jnp.concatenate([ctx_v, v], axis=1))
    o = dense_gqa(q, keys, vals, None)
    return o.reshape(b, t, D_MODEL) @ w_o, k, v


def chunk_gla(q, k, v, log_f, s0):
    b, t, h, dk = q.shape
    n = t // CHUNK
    rs = lambda a: a.reshape(b, n, CHUNK, h, a.shape[-1]).astype(jnp.float32)
    q, k, v, log_f = rs(q), rs(k), rs(v), rs(log_f)
    cum = jnp.cumsum(log_f, axis=2)
    last = cum[:, :, -1:]
    q_dec = q * jnp.exp(cum)
    k_in = k * jnp.exp(-cum)
    k_out = k * jnp.exp(last - cum)
    mask = jnp.tril(jnp.ones((CHUNK, CHUNK), jnp.float32))
    att = jnp.einsum('bnchk,bnshk->bnhcs', q_dec, k_in) * mask
    o_intra = jnp.einsum('bnhcs,bnshv->bnchv', att, v)
    d_state = jnp.einsum('bnshk,bnshv->bnhkv', k_out, v)
    decay = jnp.exp(last[:, :, 0])

    def step(s, inp):
        dec, ds = inp
        return dec[..., None] * s + ds, s

    s_final, s_prev = lax.scan(step, s0.astype(jnp.float32), (jnp.moveaxis(decay, 1, 0), jnp.moveaxis(d_state, 1, 0)))
    s_prev = jnp.moveaxis(s_prev, 0, 1)
    o_inter = jnp.einsum('bnchk,bnhkv->bnchv', q_dec, s_prev)
    return (o_intra + o_inter).reshape(b, t, h, v.shape[-1]), s_final


def mixer_hgrn2(h, w_in, w_o, g_gain, lb, s0):
    b, t, _ = h.shape
    q, f_fwd, f_bwd, i, g = jnp.split(h @ w_in, [D_FDIM, 2 * D_FDIM, 3 * D_FDIM, 3 * D_FDIM + D_MODEL], axis=-1)
    q = jax.nn.silu(q).reshape(b, t, D_HEADS, D_KDIM)
    v = i.reshape(b, t, D_HEADS, D_VDIM)
    outs, finals = [], []
    for d, f_logit in enumerate((f_fwd, f_bwd)):
        lbd = lb[d].astype(jnp.float32)
        f = lbd + (1.0 - lbd) * jax.nn.sigmoid(f_logit.astype(jnp.float32))
        k = (1.0 - f).reshape(b, t, D_HEADS, D_KDIM)
        log_f = jnp.log(f).reshape(b, t, D_HEADS, D_KDIM)
        qd, vd = q, v
        if d == 1:
            qd, k, vd, log_f = (jnp.flip(a, axis=1) for a in (qd, k, vd, log_f))
        o, s_fin = chunk_gla(qd, k, vd, log_f, s0[:, d])
        if d == 1:
            o = jnp.flip(o, axis=1)
        outs.append(o)
        finals.append(s_fin)
    o = (outs[0] + outs[1]).astype(h.dtype)
    o = rms_norm(o, g_gain) * jax.nn.silu(g.reshape(b, t, D_HEADS, D_VDIM))
    return o.reshape(b, t, D_MODEL) @ w_o, jnp.stack(finals, axis=1)


def swiglu(h, w_gate, w_up, w_down):
    return (jax.nn.silu(h @ w_gate) * (h @ w_up)) @ w_down


def setup_inputs(seed: int = 0) -> dict:
    key = jax.random.key(seed)
    ks = iter(jax.random.split(key, 48))

    def normal(shape, scale):
        return jax.random.normal(next(ks), shape, jnp.float32) * scale

    def gain(shape):
        return 1.0 + normal(shape, 0.05)

    D = D_MODEL
    gqa_in = (GQA_HEADS + 2 * GQA_KV_HEADS) * HEAD_DIM
    return {
        'x_prompt': normal((BATCH, SEQ, D), 1.0),
        'x_sample': normal((DEC_BATCH, DEC_SEQ, D), 1.0),
        'c': normal((DEC_BATCH, D), 1.0),
        'cache_a_k': normal((DEC_BATCH, N_LAYERS_A, PAST_LEN, 2 * A_HEADS, HEAD_DIM), 1.0),
        'cache_a_v': normal((DEC_BATCH, N_LAYERS_A, PAST_LEN, A_HEADS, A_VDIM), 1.0),
        'cache_b_k': normal((DEC_BATCH, N_LAYERS_B, PAST_LEN, GQA_KV_HEADS, HEAD_DIM), 1.0),
        'cache_b_v': normal((DEC_BATCH, N_LAYERS_B, PAST_LEN, GQA_KV_HEADS, HEAD_DIM), 1.0),
        'cache_c_k': normal((DEC_BATCH, N_LAYERS_C, PAST_LEN, GQA_KV_HEADS, HEAD_DIM), 1.0),
        'cache_c_v': normal((DEC_BATCH, N_LAYERS_C, PAST_LEN, GQA_KV_HEADS, HEAD_DIM), 1.0),
        'state_d': normal((DEC_BATCH, N_LAYERS_D, 2, D_HEADS, D_KDIM, D_VDIM), 0.3),
        'c_ctx': normal((D,), 1.0),
        'norm_mix': gain((DEPTH, D)),
        'norm_ffn': gain((DEPTH, D)),
        'w_ada': normal((DEPTH, D, 6 * D), 0.5 * D ** -0.5),
        'b_ada': normal((DEPTH, 6 * D), 0.01),
        'w_ffn_gate': normal((DEPTH, D, D_FF), D ** -0.5),
        'w_ffn_up': normal((DEPTH, D, D_FF), D ** -0.5),
        'w_ffn_down': normal((DEPTH, D_FF, D), D_FF ** -0.5),
        'w_qkv_a': normal((N_LAYERS_A, D, 3 * D), D ** -0.5),
        'w_o_a': normal((N_LAYERS_A, D, D), D ** -0.5),
        'qn_a': gain((N_LAYERS_A, HEAD_DIM)),
        'kn_a': gain((N_LAYERS_A, HEAD_DIM)),
        'subln_a': gain((N_LAYERS_A, A_VDIM)),
        'lam_q1_a': normal((N_LAYERS_A, HEAD_DIM), 0.1),
        'lam_k1_a': normal((N_LAYERS_A, HEAD_DIM), 0.1),
        'lam_q2_a': normal((N_LAYERS_A, HEAD_DIM), 0.1),
        'lam_k2_a': normal((N_LAYERS_A, HEAD_DIM), 0.1),
        'w_qkv_b': normal((N_LAYERS_B, D, gqa_in), D ** -0.5),
        'w_o_b': normal((N_LAYERS_B, D, D), D ** -0.5),
        'qn_b': gain((N_LAYERS_B, HEAD_DIM)),
        'kn_b': gain((N_LAYERS_B, HEAD_DIM)),
        'sink_b': normal((N_LAYERS_B, GQA_HEADS), 0.5),
        'w_qkv_c': normal((N_LAYERS_C, D, gqa_in), D ** -0.5),
        'w_o_c': normal((N_LAYERS_C, D, D), D ** -0.5),
        'qn_c': gain((N_LAYERS_C, HEAD_DIM)),
        'kn_c': gain((N_LAYERS_C, HEAD_DIM)),
        'w_in_d': normal((N_LAYERS_D, D, 3 * D_FDIM + 2 * D), D ** -0.5),
        'w_o_d': normal((N_LAYERS_D, D, D), D ** -0.5),
        'gn_d': gain((N_LAYERS_D, D_VDIM)),
        'lb_logits_d': normal((2, DEPTH, D_FDIM), 0.5),
    }


def reference(x_prompt, x_sample, c, cache_a_k, cache_a_v, cache_b_k, cache_b_v, cache_c_k, cache_c_v,
              state_d, c_ctx, norm_mix, norm_ffn, w_ada, b_ada, w_ffn_gate, w_ffn_up, w_ffn_down,
              w_qkv_a, w_o_a, qn_a, kn_a, subln_a, lam_q1_a, lam_k1_a, lam_q2_a, lam_k2_a,
              w_qkv_b, w_o_b, qn_b, kn_b, sink_b, w_qkv_c, w_o_c, qn_c, kn_c,
              w_in_d, w_o_d, gn_d, lb_logits_d):
    rope = axial_rope_tables(x_sample.shape[1])
    cond_ctx = jax.nn.silu(c_ctx)[None]
    cond_lat = jax.nn.silu(c)
    p_lb = jax.nn.softmax(lb_logits_d.astype(jnp.float32), axis=1)
    lb_all = jnp.cumsum(p_lb, axis=1) - p_lb[:, :1]
    zero_state = jnp.zeros((x_prompt.shape[0], 2, D_HEADS, D_KDIM, D_VDIM), jnp.float32)

    xp, xs = x_prompt, x_sample
    new_a_k, new_a_v, new_b_k, new_b_v, new_c_k, new_c_v, new_d = [], [], [], [], [], [], []
    for li in range(DEPTH):
        kind, j = li % N_MIXERS, li // N_MIXERS
        mod_p = jnp.split(cond_ctx @ w_ada[li] + b_ada[li], 6, axis=-1)
        mod_s = jnp.split(cond_lat @ w_ada[li] + b_ada[li], 6, axis=-1)
        hp = modulate(rms_norm(xp, norm_mix[li]), mod_p[0], mod_p[1])
        hs = modulate(rms_norm(xs, norm_mix[li]), mod_s[0], mod_s[1])
        if kind == 0:
            lam_init = 0.8 - 0.6 * math.exp(-0.3 * li)
            lam = (jnp.exp(jnp.sum(lam_q1_a[j] * lam_k1_a[j])) - jnp.exp(jnp.sum(lam_q2_a[j] * lam_k2_a[j]))
                   + lam_init).astype(jnp.float32)
            op, kc, vc = mixer_diff_attn(hp, w_qkv_a[j], w_o_a[j], qn_a[j], kn_a[j], subln_a[j], lam, lam_init,
                                         None, None, None)
            os_, _, _ = mixer_diff_attn(hs, w_qkv_a[j], w_o_a[j], qn_a[j], kn_a[j], subln_a[j], lam, lam_init,
                                        rope, cache_a_k[:, j], cache_a_v[:, j])
            new_a_k.append(kc)
            new_a_v.append(vc)
        elif kind == 1:
            op, kc, vc = mixer_window_sink(hp, w_qkv_b[j], w_o_b[j], qn_b[j], kn_b[j], sink_b[j], None, None, None)
            os_, _, _ = mixer_window_sink(hs, w_qkv_b[j], w_o_b[j], qn_b[j], kn_b[j], sink_b[j],
                                          rope, cache_b_k[:, j], cache_b_v[:, j])
            new_b_k.append(kc)
            new_b_v.append(vc)
        elif kind == 2:
            op, kc, vc = mixer_axial_gqa(hp, w_qkv_c[j], w_o_c[j], qn_c[j], kn_c[j], None, None, None)
            os_, _, _ = mixer_axial_gqa(hs, w_qkv_c[j], w_o_c[j], qn_c[j], kn_c[j],
                                        rope, cache_c_k[:, j], cache_c_v[:, j])
            new_c_k.append(kc)
            new_c_v.append(vc)
        else:
            op, sc = mixer_hgrn2(hp, w_in_d[j], w_o_d[j], gn_d[j], lb_all[:, li], zero_state)
            os_, _ = mixer_hgrn2(hs, w_in_d[j], w_o_d[j], gn_d[j], lb_all[:, li], state_d[:, j])
            new_d.append(sc)
        xp = xp + mod_p[2][:, None] * op
        xs = xs + mod_s[2][:, None] * os_
        hp = modulate(rms_norm(xp, norm_ffn[li]), mod_p[3], mod_p[4])
        hs = modulate(rms_norm(xs, norm_ffn[li]), mod_s[3], mod_s[4])
        xp = xp + mod_p[5][:, None] * swiglu(hp, w_ffn_gate[li], w_ffn_up[li], w_ffn_down[li])
        xs = xs + mod_s[5][:, None] * swiglu(hs, w_ffn_gate[li], w_ffn_up[li], w_ffn_down[li])

    return (xp, xs, jnp.stack(new_a_k, axis=1), jnp.stack(new_a_v, axis=1), jnp.stack(new_b_k, axis=1),
            jnp.stack(new_b_v, axis=1), jnp.stack(new_c_k, axis=1), jnp.stack(new_c_v, axis=1),
            jnp.stack(new_d, axis=1))
```

```python
import functools
import math

import jax
import jax.numpy as jnp
from jax import lax
from jax.experimental import pallas as pl
from jax.experimental.pallas import tpu as pltpu

F32 = jnp.float32
BF16 = jnp.bfloat16

D_MODEL = 1024
HEAD_DIM = 64
GRID_W = 64
ROPE_THETA = 10000.0
WINDOW = 128
EPS = 1e-6
NEG_INF = -1e30
CHUNK = 64
D_KDIM = 128
D_HEADS = D_MODEL // D_KDIM
N_MOD = 6
MOD_ROWS = 8
QK_CHUNK = 256
V7X_VMEM_BYTES = 64 * 1024 * 1024
VMEM_LIMIT = V7X_VMEM_BYTES - 8 * 1024 * 1024


def _dot(a, b):
    return jnp.dot(a, b, preferred_element_type=F32)


def _dot_nt(a, b):
    return lax.dot_general(a, b, (((1,), (1,)), ((), ())), preferred_element_type=F32)


def _dot_tn(a, b):
    return lax.dot_general(a, b, (((0,), (0,)), ((), ())), preferred_element_type=F32)


def _silu(x):
    return x * jax.nn.sigmoid(x)


def _params(n_axes, **kw):
    return pltpu.CompilerParams(dimension_semantics=("arbitrary",) * n_axes, vmem_limit_bytes=VMEM_LIMIT, **kw)


def _const_spec(shape):
    return pl.BlockSpec(shape, lambda *_: (0,) * len(shape), pipeline_mode=pl.Buffered(1))


def _norm_mod(x, gain, shift, scale):
    ms = jnp.mean(x * x, axis=-1, keepdims=True)
    return (x * lax.rsqrt(ms + EPS) * gain) * (1.0 + scale) + shift


def _split2(x):
    hi = x.astype(BF16)
    lo = (x - hi.astype(F32)).astype(BF16)
    return hi, lo


def _split3(x):
    hi = x.astype(BF16)
    r = x - hi.astype(F32)
    mid = r.astype(BF16)
    lo = (r - mid.astype(F32)).astype(BF16)
    return hi, mid, lo


def _group_mean_sq(y, gmat):
    hi, lo = _split2(y * y)
    return (_dot(hi, gmat) + _dot(lo, gmat)) * (1.0 / HEAD_DIM)


def _tile4(y, grp):
    outs = []
    for g in range(4):
        t = jnp.where(grp == g, y, 0.0)
        u = t + pltpu.roll(t, 128, 1)
        outs.append(u + pltpu.roll(u, 64, 1))
    return outs


def _ada_kernel(cond_ref, w_ref, b_ref, o_ref):
    a = _silu(cond_ref[...]).astype(BF16)
    o_ref[...] = _dot(a, w_ref[...].astype(BF16)) + b_ref[...]


def _ada_all(cond, w_ada, b_ada):
    depth, d, n = w_ada.shape
    tn = n // 4
    return pl.pallas_call(
        _ada_kernel,
        out_shape=jax.ShapeDtypeStruct((depth, MOD_ROWS, n), F32),
        grid=(depth, n // tn),
        in_specs=[
            pl.BlockSpec((MOD_ROWS, d), lambda l, j: (0, 0)),
            pl.BlockSpec((None, d, tn), lambda l, j: (l, 0, j)),
            pl.BlockSpec((None, 1, tn), lambda l, j: (l, 0, j)),
        ],
        out_specs=pl.BlockSpec((None, MOD_ROWS, tn), lambda l, j: (l, 0, j)),
        compiler_params=_params(2),
        name="ada_mod",
    )(cond, w_ada, b_ada.reshape(depth, 1, n))


class _Geom:
    def __init__(self, bp, tp, bs, ts, past):
        self.bp, self.tp, self.bs, self.ts, self.past = bp, tp, bs, ts, past
        self.n_p = bp * tp
        self.n_s = bs * ts
        self.n_t = self.n_p + self.n_s
        self.tm = math.gcd(512, math.gcd(self.n_p, ts))
        assert self.n_p % ts == 0 and ts % CHUNK == 0 and tp % CHUNK == 0
        assert bs + 1 <= MOD_ROWS

    def mod_row(self, i):
        return jnp.where(i * self.tm < self.n_p, 0, 1 + (i * self.tm - self.n_p) // self.ts)

    def prompt_tile(self, i):
        return jnp.minimum(i, self.n_p // self.tm - 1)

    def sample_tile(self, i):
        return jnp.maximum(i - self.n_p // self.tm, 0)

    def rope_tile(self, i):
        return self.sample_tile(i) % (self.ts // self.tm)


def _mod_spec(geo):
    return pl.BlockSpec((None, 1, N_MOD * D_MODEL), lambda i: (geo.mod_row(i), 0, 0))


def _qkv_kernel(x_ref, mod_ref, gain_ref, w_ref, gmat_ref, qg_ref, kg_ref, cos_ref, sin_ref,
                q_ref, k_ref, v_ref, k32_ref, v32_ref, h_scr, *, tm, n_p, n_q, n_k, n_v, tiled):
    i = pl.program_id(0)
    is_sample = i * tm >= n_p
    is_prompt = jnp.logical_not(is_sample)
    mod = mod_ref[...]
    h = _norm_mod(x_ref[...], gain_ref[...], mod[:, 0:D_MODEL], mod[:, D_MODEL:2 * D_MODEL])
    h_scr[...] = h.astype(BF16)
    lane = lax.broadcasted_iota(jnp.int32, (tm, QK_CHUNK), 1)
    first_half = (lane % 32) < 16
    grp = lane // HEAD_DIM

    def rope(y):
        partner = jnp.where(first_half, pltpu.roll(y, QK_CHUNK - 16, 1), pltpu.roll(y, 16, 1))
        return y * cos_ref[...] + partner * sin_ref[...]

    def normed(c0, g_ref):
        y = _dot(h_scr[...], w_ref[:, c0:c0 + QK_CHUNK])
        return y * lax.rsqrt(_group_mean_sq(y, gmat_ref[...]) + EPS) * g_ref[...]

    for c in range(n_q // QK_CHUNK):
        cs = slice(c * QK_CHUNK, (c + 1) * QK_CHUNK)
        yn = normed(c * QK_CHUNK, qg_ref)

        @pl.when(is_sample)
        def _():
            q_ref[:, cs] = rope(yn).astype(BF16)

        @pl.when(is_prompt)
        def _():
            q_ref[:, cs] = yn.astype(BF16)

    for c in range(n_k // QK_CHUNK):
        cs = slice(c * QK_CHUNK, (c + 1) * QK_CHUNK)
        yn = normed(n_q + c * QK_CHUNK, kg_ref)

        @pl.when(is_sample)
        def _():
            yr = rope(yn)
            if tiled:
                for g, t in enumerate(_tile4(yr, grp)):
                    k_ref[:, g * QK_CHUNK:(g + 1) * QK_CHUNK] = t.astype(BF16)
            else:
                k_ref[:, cs] = yr.astype(BF16)

        @pl.when(is_prompt)
        def _():
            k32_ref[:, cs] = yn
            if tiled:
                for g, t in enumerate(_tile4(yn, grp)):
                    k_ref[:, g * QK_CHUNK:(g + 1) * QK_CHUNK] = t.astype(BF16)
            else:
                k_ref[:, cs] = yn.astype(BF16)

    for c in range(n_v // QK_CHUNK):
        cs = slice(c * QK_CHUNK, (c + 1) * QK_CHUNK)
        c0 = n_q + n_k + c * QK_CHUNK
        y = _dot(h_scr[...], w_ref[:, c0:c0 + QK_CHUNK])
        if tiled:
            for g, t in enumerate(_tile4(y, grp)):
                v_ref[:, g * QK_CHUNK:(g + 1) * QK_CHUNK] = t.astype(BF16)
        else:
            v_ref[:, cs] = y.astype(BF16)

        @pl.when(is_prompt)
        def _():
            v32_ref[:, cs] = y


def _qkv_proj(geo, x, mod, gain, w, gmat, q_gain, k_gain, cos, sin, *, n_q, n_k, n_v, tiled):
    tm = geo.tm
    n_kt = 4 * n_k if tiled else n_k
    n_vt = 4 * n_v if tiled else n_v
    row = lambda i: (i, 0)
    prow = lambda i: (geo.prompt_tile(i), 0)
    vec = lambda n: pl.BlockSpec((1, n), lambda i: (0, 0))
    kern = functools.partial(_qkv_kernel, tm=tm, n_p=geo.n_p, n_q=n_q, n_k=n_k, n_v=n_v, tiled=tiled)
    return pl.pallas_call(
        kern,
        out_shape=(
            jax.ShapeDtypeStruct((geo.n_t, n_q), BF16),
            jax.ShapeDtypeStruct((geo.n_t, n_kt), BF16),
            jax.ShapeDtypeStruct((geo.n_t, n_vt), BF16),
            jax.ShapeDtypeStruct((geo.n_p, n_k), F32),
            jax.ShapeDtypeStruct((geo.n_p, n_v), F32),
        ),
        grid=(geo.n_t // tm,),
        in_specs=[
            pl.BlockSpec((tm, D_MODEL), row),
            _mod_spec(geo),
            vec(D_MODEL),
            _const_spec((D_MODEL, n_q + n_k + n_v)),
            _const_spec((QK_CHUNK, QK_CHUNK)),
            vec(QK_CHUNK),
            vec(QK_CHUNK),
            pl.BlockSpec((tm, QK_CHUNK), lambda i: (geo.rope_tile(i), 0)),
            pl.BlockSpec((tm, QK_CHUNK), lambda i: (geo.rope_tile(i), 0)),
        ],
        out_specs=(
            pl.BlockSpec((tm, n_q), row),
            pl.BlockSpec((tm, n_kt), row),
            pl.BlockSpec((tm, n_vt), row),
            pl.BlockSpec((tm, n_k), prow),
            pl.BlockSpec((tm, n_v), prow),
        ),
        scratch_shapes=[pltpu.VMEM((tm, D_MODEL), BF16)],
        compiler_params=_params(1),
        name="qkv_proj",
    )(x, mod, gain, w, gmat, q_gain, k_gain, cos, sin)


def _flash_kernel(*refs, R, tq, kb, n_own, t_len, has_ctx, ctx_tiled, use_sink, window, diff, lam_scale):
    refs = list(refs)
    q_ref, k_ref, v_ref = refs[:3]
    del refs[:3]
    if has_ctx:
        ck_ref, cv_ref = refs[:2]
        del refs[:2]
    if use_sink:
        sink_ref = refs.pop(0)
    if diff:
        lam_ref, subg_ref = refs[:2]
        del refs[:2]
    o_ref, qs_scr, acc_scr, m_scr, l_scr = refs[:5]
    del refs[:5]
    if has_ctx and ctx_tiled:
        ctk_scr, ctv_scr = refs

    W = HEAD_DIM * R
    c = pl.program_id(1)
    qi = pl.program_id(2)

    q = q_ref[...].astype(F32)
    lane = lax.broadcasted_iota(jnp.int32, (tq, W), 1) // HEAD_DIM
    for r in range(R):
        qs_scr[r * tq:(r + 1) * tq, :] = jnp.where(lane == r, q, 0.0).astype(BF16)

    if use_sink:
        for r in range(R):
            m_scr[r * tq:(r + 1) * tq, :] = jnp.full((tq, 1), sink_ref[c * R + r], F32)
        l_scr[...] = jnp.ones_like(l_scr)
    else:
        m_scr[...] = jnp.full_like(m_scr, NEG_INF)
        l_scr[...] = jnp.zeros_like(l_scr)
    acc_scr[...] = jnp.zeros_like(acc_scr)

    def step(kblk, vblk, mask=None):
        s = _dot_nt(qs_scr[...], kblk)
        if mask is not None:
            s = jnp.where(mask, s, NEG_INF)
        m_prev = m_scr[...]
        m_new = jnp.maximum(m_prev, jnp.max(s, axis=-1, keepdims=True))
        alpha = jnp.exp(m_prev - m_new)
        p = jnp.exp(s - m_new)
        l_scr[...] = alpha * l_scr[...] + jnp.sum(p, axis=-1, keepdims=True)
        acc_scr[...] = alpha * acc_scr[...] + _dot(p.astype(BF16), vblk)
        m_scr[...] = m_new

    if has_ctx:
        if ctx_tiled:
            @pl.when(qi == 0)
            def _():
                p_len = ck_ref.shape[0]
                grp = lax.broadcasted_iota(jnp.int32, (p_len, W), 1) // HEAD_DIM
                for src, dst in ((ck_ref, ctk_scr), (cv_ref, ctv_scr)):
                    t = jnp.where(grp == c, src[...], 0.0)
                    u = t + pltpu.roll(t, 128, 1)
                    dst[...] = (u + pltpu.roll(u, 64, 1)).astype(BF16)

            step(ctk_scr[...], ctv_scr[...])
        else:
            step(ck_ref[...].astype(BF16), cv_ref[...].astype(BF16))

    if window:
        q0 = qi * tq
        start = pl.multiple_of(jnp.clip(q0 - WINDOW, 0, t_len - kb), WINDOW)
        qpos = q0 + lax.broadcasted_iota(jnp.int32, (R * tq, kb), 0) % tq
        kpos = start + lax.broadcasted_iota(jnp.int32, (R * tq, kb), 1)
        mask = jnp.abs(qpos - kpos) <= WINDOW
        step(k_ref[pl.ds(start, kb), :], v_ref[pl.ds(start, kb), :], mask)
    elif n_own == 1:
        step(k_ref[...], v_ref[...])
    else:
        def body(j, carry):
            off = pl.multiple_of(j * kb, kb)
            step(k_ref[pl.ds(off, kb), :], v_ref[pl.ds(off, kb), :])
            return carry

        lax.fori_loop(0, n_own, body, 0)

    a = acc_scr[...] * (1.0 / l_scr[...])
    if diff:
        o = a[:tq] - lam_ref[0] * a[tq:]
        ms = jnp.mean(o * o, axis=-1, keepdims=True)
        o = o * lax.rsqrt(ms + EPS) * subg_ref[...] * lam_scale
    else:
        o = jnp.zeros((tq, W), F32)
        for r in range(R):
            o = o + jnp.where(lane == r, a[r * tq:(r + 1) * tq], 0.0)
    o_ref[...] = o.astype(BF16)


def _flash(q, k, v, *, row0, n_b, t_len, tq, R, ctx=None, ctx_tiled=False, sink=None, window=False,
           lam=None, subg=None, lam_scale=1.0, name):
    W = HEAD_DIM * R
    n_c = D_MODEL // W
    n_qb = t_len // tq
    diff = lam is not None
    if window:
        kb, n_own = tq + 2 * WINDOW, 1
    else:
        kb = min(t_len, 512)
        n_own = t_len // kb
    q_base, k_base = row0 // tq, row0 // t_len
    q_map = lambda b, c, qi: (q_base + b * n_qb + qi, c)
    kv_map = lambda b, c, qi: (k_base + b, c)
    in_specs = [pl.BlockSpec((tq, W), q_map), pl.BlockSpec((t_len, W), kv_map), pl.BlockSpec((t_len, W), kv_map)]
    args = [q, k, v]
    scratch = [pltpu.VMEM((R * tq, W), BF16), pltpu.VMEM((R * tq, W), F32),
               pltpu.VMEM((R * tq, 1), F32), pltpu.VMEM((R * tq, 1), F32)]
    if ctx is not None:
        ck, cv = ctx
        p_len = ck.shape[1]
        if ctx_tiled:
            cmap = lambda b, c, qi: (b, 0, 0)
            scratch += [pltpu.VMEM((p_len, W), BF16), pltpu.VMEM((p_len, W), BF16)]
        else:
            cmap = lambda b, c, qi: (b, 0, c)
        in_specs += [pl.BlockSpec((None, p_len, W), cmap)] * 2
        args += [ck, cv]
    if sink is not None:
        in_specs.append(pl.BlockSpec(memory_space=pltpu.SMEM))
        args.append(sink)
    if diff:
        in_specs += [pl.BlockSpec(memory_space=pltpu.SMEM), pl.BlockSpec((1, W), lambda b, c, qi: (0, 0))]
        args += [lam, subg]
    kern = functools.partial(
        _flash_kernel, R=R, tq=tq, kb=kb, n_own=n_own, t_len=t_len, has_ctx=ctx is not None,
        ctx_tiled=ctx_tiled, use_sink=sink is not None, window=window, diff=diff, lam_scale=lam_scale)
    return pl.pallas_call(
        kern,
        out_shape=jax.ShapeDtypeStruct((n_b * t_len, D_MODEL), BF16),
        grid=(n_b, n_c, n_qb),
        in_specs=in_specs,
        out_specs=pl.BlockSpec((tq, W), lambda b, c, qi: (b * n_qb + qi, c)),
        scratch_shapes=scratch,
        compiler_params=_params(3),
        name=name,
    )(*args)


def _proj_d_kernel(x_ref, mod_ref, gain_ref, w_ref, lb_ref, qs_ref, lff_ref, lfb_ref, v_ref, gs_ref, h_scr, *, cw):
    mod = mod_ref[...]
    h = _norm_mod(x_ref[...], gain_ref[...], mod[:, 0:D_MODEL], mod[:, D_MODEL:2 * D_MODEL])
    h_scr[...] = h.astype(BF16)
    n_c = D_MODEL // cw

    def proj(part, c):
        c0 = part * D_MODEL + c * cw
        return _dot(h_scr[...], w_ref[:, c0:c0 + cw])

    for c in range(n_c):
        cs = slice(c * cw, (c + 1) * cw)
        qs_ref[:, cs] = _silu(proj(0, c))
        for d, ref in ((0, lff_ref), (1, lfb_ref)):
            lb = lb_ref[d:d + 1, cs]
            ref[:, cs] = jnp.log(lb + (1.0 - lb) * jax.nn.sigmoid(proj(1 + d, c)))
        v_ref[:, cs] = proj(3, c).astype(BF16)
        gs_ref[:, cs] = _silu(proj(4, c))


def _proj_d(geo, x, mod, gain, w, lb):
    tm = geo.tm
    row = lambda i: (i, 0)
    spec = pl.BlockSpec((tm, D_MODEL), row)
    f32_out = jax.ShapeDtypeStruct((geo.n_t, D_MODEL), F32)
    return pl.pallas_call(
        functools.partial(_proj_d_kernel, cw=512),
        out_shape=(f32_out, f32_out, f32_out, jax.ShapeDtypeStruct((geo.n_t, D_MODEL), BF16), f32_out),
        grid=(geo.n_t // tm,),
        in_specs=[spec, _mod_spec(geo), pl.BlockSpec((1, D_MODEL), lambda i: (0, 0)),
                  _const_spec((D_MODEL, 5 * D_MODEL)), pl.BlockSpec((2, D_MODEL), lambda i: (0, 0))],
        out_specs=(spec,) * 5,
        scratch_shapes=[pltpu.VMEM((tm, D_MODEL), BF16)],
        compiler_params=_params(1),
        name="hgrn_proj",
    )(x, mod, gain, w, lb)


def _hgrn_kernel(*refs, n_chunks, has_init, want_final):
    refs = list(refs)
    tri_ref = refs.pop(0)
    ins = [refs[:3], refs[3:6]]
    del refs[:6]
    if has_init:
        s0_ref = refs.pop(0)
    o_refs = refs[:2]
    del refs[:2]
    if want_final:
        fin_ref = refs.pop(0)
    st_scr = refs.pop(0)
    n = pl.program_id(1)

    @pl.when(n == 0)
    def _():
        st_scr[...] = s0_ref[...] if has_init else jnp.zeros_like(st_scr)

    row = lax.broadcasted_iota(jnp.int32, (CHUNK, CHUNK), 0)
    col = lax.broadcasted_iota(jnp.int32, (CHUNK, CHUNK), 1)
    for d in range(2):
        qs_ref, lf_ref, v_ref = ins[d]
        tri = tri_ref[d]
        lf = lf_ref[...]
        cum = sum(_dot(tri, part) for part in _split3(lf))
        last = cum[CHUNK - 1:CHUNK] if d == 0 else cum[0:1]
        e_last = jnp.exp(last)
        q_dec = (qs_ref[...] * jnp.exp(cum)).astype(BF16)
        k_in = (1.0 - jnp.exp(lf)) * jnp.exp(-cum)
        k_out = (k_in * e_last).astype(BF16)
        k_in = k_in.astype(BF16)
        keep = (col <= row) if d == 0 else (col >= row)
        for hd in range(D_HEADS):
            hs = slice(hd * D_KDIM, (hd + 1) * D_KDIM)
            st = st_scr[d, hd]
            att = jnp.where(keep, _dot_nt(q_dec[:, hs], k_in[:, hs]), 0.0).astype(BF16)
            v = v_ref[:, hs]
            o_refs[d][:, hs] = _dot(att, v) + _dot_nt(q_dec[:, hs], st.astype(BF16))
            st_new = st * e_last[:, hs] + _dot_tn(v, k_out[:, hs])
            st_scr[d, hd] = st_new
            if want_final:
                @pl.when(n == n_chunks - 1)
                def _():
                    fin_ref[d, hd] = st_new.T


def _hgrn_scan(tri, qs, lff, lfb, v, *, row0, n_b, t_len, s0t=None, want_final=False, name):
    n_chunks = t_len // CHUNK
    base = row0 // CHUNK
    fwd = lambda b, n: (base + b * n_chunks + n, 0)
    bwd = lambda b, n: (base + b * n_chunks + n_chunks - 1 - n, 0)
    blk = lambda m: pl.BlockSpec((CHUNK, D_MODEL), m)
    st_shape = (2, D_HEADS, D_KDIM, D_KDIM)
    in_specs = [pl.BlockSpec((2, CHUNK, CHUNK), lambda b, n: (0, 0, 0)),
                blk(fwd), blk(fwd), blk(fwd), blk(bwd), blk(bwd), blk(bwd)]
    args = [tri, qs, lff, v, qs, lfb, v]
    if s0t is not None:
        in_specs.append(pl.BlockSpec((None,) + st_shape, lambda b, n: (b, 0, 0, 0, 0)))
        args.append(s0t)
    o_shape = jax.ShapeDtypeStruct((n_b * t_len, D_MODEL), F32)
    out_shape = [o_shape, o_shape]
    out_specs = [pl.BlockSpec((CHUNK, D_MODEL), lambda b, n: (b * n_chunks + n, 0)),
                 pl.BlockSpec((CHUNK, D_MODEL), lambda b, n: (b * n_chunks + n_chunks - 1 - n, 0))]
    if want_final:
        out_shape.append(jax.ShapeDtypeStruct((n_b,) + st_shape, F32))
        out_specs.append(pl.BlockSpec((None,) + st_shape, lambda b, n: (b, 0, 0, 0, 0)))
    kern = functools.partial(_hgrn_kernel, n_chunks=n_chunks, has_init=s0t is not None, want_final=want_final)
    return pl.pallas_call(
        kern,
        out_shape=tuple(out_shape),
        grid=(n_b, n_chunks),
        in_specs=in_specs,
        out_specs=tuple(out_specs),
        scratch_shapes=[pltpu.VMEM(st_shape, F32)],
        compiler_params=_params(2),
        name=name,
    )(*args)


def _ffn_chunks(d_ff):
    chunks, f0 = [], 0
    while f0 < d_ff:
        fs = min(512, d_ff - f0)
        chunks.append((f0, fs))
        f0 += fs
    return tuple(chunks)


def _out_ffn_kernel(*refs, tm, n_p, hgrn, f_chunks):
    refs = list(refs)
    x_ref = refs.pop(0)
    n_o = 4 if hgrn else 2
    o_in = refs[:n_o]
    del refs[:n_o]
    if hgrn:
        gs_ref, gn_ref = refs[:2]
        del refs[:2]
    mod_ref, wo_ref, ng_ref, wg_ref, wu_ref, wd_ref, out_ref, h_scr, acc_scr = refs
    is_sample = pl.program_id(0) * tm >= n_p
    mod = mod_ref[...]
    m = [mod[:, j * D_MODEL:(j + 1) * D_MODEL] for j in range(N_MOD)]

    if hgrn:
        ofp, obp, ofs, obs = o_in
        o = jnp.where(is_sample, ofs[...] + obs[...], ofp[...] + obp[...])
        parts = []
        for hd in range(D_HEADS):
            hs = slice(hd * D_KDIM, (hd + 1) * D_KDIM)
            oh = o[:, hs]
            ms = jnp.mean(oh * oh, axis=-1, keepdims=True)
            parts.append(oh * lax.rsqrt(ms + EPS) * gn_ref[...] * gs_ref[:, hs])
        o = jnp.concatenate(parts, axis=-1).astype(BF16)
    else:
        o = jnp.where(is_sample, o_in[1][...], o_in[0][...])

    x1 = x_ref[...] + m[2] * _dot(o, wo_ref[...])
    out_ref[...] = x1
    h_scr[...] = _norm_mod(x1, ng_ref[...], m[3], m[4]).astype(BF16)
    for idx, (f0, fs) in enumerate(f_chunks):
        hb = h_scr[...]
        g = _dot(hb, wg_ref[:, f0:f0 + fs])
        u = _dot(hb, wu_ref[:, f0:f0 + fs])
        y = _dot((_silu(g) * u).astype(BF16), wd_ref[f0:f0 + fs, :])
        acc_scr[...] = y if idx == 0 else acc_scr[...] + y
    out_ref[...] = out_ref[...] + m[5] * acc_scr[...]


def _out_ffn(geo, x, o_parts, mod, w_o, norm_gain, w_gate, w_up, w_down, gs=None, gn=None):
    tm = geo.tm
    hgrn = gs is not None
    d_ff = w_gate.shape[1]
    row = lambda i: (i, 0)
    prow = lambda i: (geo.prompt_tile(i), 0)
    srow = lambda i: (geo.sample_tile(i), 0)
    blk = lambda m: pl.BlockSpec((tm, D_MODEL), m)
    in_specs = [blk(row)]
    args = [x]
    if hgrn:
        in_specs += [blk(prow), blk(prow), blk(srow), blk(srow), blk(row), pl.BlockSpec((1, D_KDIM), lambda i: (0, 0))]
        args += list(o_parts) + [gs, gn]
    else:
        in_specs += [blk(prow), blk(srow)]
        args += list(o_parts)
    in_specs += [_mod_spec(geo), _const_spec((D_MODEL, D_MODEL)), pl.BlockSpec((1, D_MODEL), lambda i: (0, 0)),
                 _const_spec((D_MODEL, d_ff)), _const_spec((D_MODEL, d_ff)), _const_spec((d_ff, D_MODEL))]
    args += [mod, w_o, norm_gain, w_gate, w_up, w_down]
    kern = functools.partial(_out_ffn_kernel, tm=tm, n_p=geo.n_p, hgrn=hgrn, f_chunks=_ffn_chunks(d_ff))
    return pl.pallas_call(
        kern,
        out_shape=jax.ShapeDtypeStruct((geo.n_t, D_MODEL), F32),
        grid=(geo.n_t // tm,),
        in_specs=in_specs,
        out_specs=blk(row),
        scratch_shapes=[pltpu.VMEM((tm, D_MODEL), BF16), pltpu.VMEM((tm, D_MODEL), F32)],
        compiler_params=_params(1),
        name="out_ffn",
    )(*args)


def _rope_tables(n_tokens):
    rows = n_tokens // GRID_W
    row = jnp.repeat(jnp.arange(rows, dtype=F32), GRID_W)
    col = jnp.tile(jnp.arange(GRID_W, dtype=F32), rows)
    axis_dim = HEAD_DIM // 2
    inv_freq = ROPE_THETA ** (-jnp.arange(0, axis_dim, 2, dtype=F32) / axis_dim)
    ang = jnp.stack([row[:, None] * inv_freq, col[:, None] * inv_freq], axis=1)
    cos, sin = jnp.cos(ang), jnp.sin(ang)
    cos64 = jnp.concatenate([cos, cos], axis=-1).reshape(n_tokens, HEAD_DIM)
    sin64 = jnp.concatenate([-sin, sin], axis=-1).reshape(n_tokens, HEAD_DIM)
    reps = QK_CHUNK // HEAD_DIM
    return jnp.tile(cos64, (1, reps)), jnp.tile(sin64, (1, reps))


def _group_matrix():
    g = jnp.arange(QK_CHUNK) // HEAD_DIM
    return (g[:, None] == g[None, :]).astype(BF16)


def _tri_matrices():
    r = jnp.arange(CHUNK)
    lower = (r[None, :] <= r[:, None]).astype(BF16)
    return jnp.stack([lower, lower.T])


def kernel(x_prompt, x_sample, c, cache_a_k, cache_a_v, cache_b_k, cache_b_v, cache_c_k, cache_c_v, state_d, c_ctx, norm_mix, norm_ffn, w_ada, b_ada, w_ffn_gate, w_ffn_up, w_ffn_down, w_qkv_a, w_o_a, qn_a, kn_a, subln_a, lam_q1_a, lam_k1_a, lam_q2_a, lam_k2_a, w_qkv_b, w_o_b, qn_b, kn_b, sink_b, w_qkv_c, w_o_c, qn_c, kn_c, w_in_d, w_o_d, gn_d, lb_logits_d):
    bp, tp, _ = x_prompt.shape
    bs, ts, _ = x_sample.shape
    past = cache_a_k.shape[2]
    depth = w_ada.shape[0]
    geo = _Geom(bp, tp, bs, ts, past)
    n_p = geo.n_p

    cond = jnp.concatenate([c_ctx[None], c, jnp.zeros((MOD_ROWS - 1 - bs, D_MODEL), F32)], axis=0)
    mod_all = _ada_all(cond, w_ada, b_ada).reshape(depth, MOD_ROWS, 1, N_MOD * D_MODEL)

    cos, sin = _rope_tables(ts)
    gmat = _group_matrix()
    tri = _tri_matrices()
    p_lb = jax.nn.softmax(lb_logits_d.astype(F32), axis=1)
    lb_all = jnp.cumsum(p_lb, axis=1) - p_lb[:, :1]
    scale = HEAD_DIM ** -0.5
    tile_gain = lambda g, s=1.0: jnp.tile(g * s, QK_CHUNK // HEAD_DIM)[None]
    bf = lambda w: w.astype(BF16)

    x = jnp.concatenate([x_prompt.reshape(n_p, D_MODEL), x_sample.reshape(geo.n_s, D_MODEL)], axis=0)
    outs = {k: [] for k in ("ak", "av", "bk", "bv", "ck", "cv", "d")}
    for li in range(depth):
        kind, j = li % 4, li // 4
        mod = mod_all[li]
        gain = norm_mix[li][None]
        ffn_w = (norm_ffn[li][None], bf(w_ffn_gate[li]), bf(w_ffn_up[li]), bf(w_ffn_down[li]))
        if kind == 0:
            lam_init = 0.8 - 0.6 * math.exp(-0.3 * li)
            lam = (jnp.exp(jnp.sum(lam_q1_a[j] * lam_k1_a[j])) - jnp.exp(jnp.sum(lam_q2_a[j] * lam_k2_a[j]))
                   + lam_init).astype(F32).reshape(1)
            q, k, v, k32, v32 = _qkv_proj(geo, x, mod, gain, bf(w_qkv_a[j]), gmat, tile_gain(qn_a[j], scale),
                                          tile_gain(kn_a[j]), cos, sin, n_q=D_MODEL, n_k=D_MODEL, n_v=D_MODEL,
                                          tiled=False)
            common = dict(R=2, lam=lam, subg=subln_a[j][None], lam_scale=1.0 - lam_init)
            o_p = _flash(q, k, v, row0=0, n_b=bp, t_len=tp, tq=tp, name="attn_a_prompt", **common)
            ctx = (cache_a_k[:, j].reshape(bs, past, D_MODEL), cache_a_v[:, j].reshape(bs, past, D_MODEL))
            o_s = _flash(q, k, v, row0=n_p, n_b=bs, t_len=ts, tq=min(512, ts), ctx=ctx, name="attn_a_sample",
                         **common)
            outs["ak"].append(k32.reshape(bp, tp, 2 * D_MODEL // (2 * HEAD_DIM), HEAD_DIM))
            outs["av"].append(v32.reshape(bp, tp, D_MODEL // (2 * HEAD_DIM), 2 * HEAD_DIM))
            x = _out_ffn(geo, x, (o_p, o_s), mod, bf(w_o_a[j]), *ffn_w)
        elif kind in (1, 2):
            w_qkv, w_o, qn, kn = (w_qkv_b, w_o_b, qn_b, kn_b) if kind == 1 else (w_qkv_c, w_o_c, qn_c, kn_c)
            cache_k, cache_v = (cache_b_k, cache_b_v) if kind == 1 else (cache_c_k, cache_c_v)
            n_kv = w_qkv.shape[-1] - D_MODEL
            q, k, v, k32, v32 = _qkv_proj(geo, x, mod, gain, bf(w_qkv[j]), gmat, tile_gain(qn[j], scale),
                                          tile_gain(kn[j]), cos, sin, n_q=D_MODEL, n_k=n_kv // 2, n_v=n_kv // 2,
                                          tiled=True)
            sink = sink_b[j].astype(F32) if kind == 1 else None
            ctx = (cache_k[:, j].reshape(bs, past, n_kv // 2), cache_v[:, j].reshape(bs, past, n_kv // 2))
            tag = "b" if kind == 1 else "c"
            o_p = _flash(q, k, v, row0=0, n_b=bp, t_len=tp, tq=tp, R=4, sink=sink, name=f"attn_{tag}_prompt")
            o_s = _flash(q, k, v, row0=n_p, n_b=bs, t_len=ts, tq=min(256, ts), R=4, ctx=ctx, ctx_tiled=True,
                         sink=sink, window=kind == 1, name=f"attn_{tag}_sample")
            kv_heads = n_kv // 2 // HEAD_DIM
            key = "b" if kind == 1 else "c"
            outs[key + "k"].append(k32.reshape(bp, tp, kv_heads, HEAD_DIM))
            outs[key + "v"].append(v32.reshape(bp, tp, kv_heads, HEAD_DIM))
            x = _out_ffn(geo, x, (o_p, o_s), mod, bf(w_o[j]), *ffn_w)
        else:
            qs, lff, lfb, v, gs = _proj_d(geo, x, mod, gain, bf(w_in_d[j]), lb_all[:, li])
            of_p, ob_p, fin = _hgrn_scan(tri, qs, lff, lfb, v, row0=0, n_b=bp, t_len=tp, want_final=True,
                                         name="hgrn_prompt")
            s0t = jnp.swapaxes(state_d[:, j], -1, -2)
            of_s, ob_s = _hgrn_scan(tri, qs, lff, lfb, v, row0=n_p, n_b=bs, t_len=ts, s0t=s0t, name="hgrn_sample")
            outs["d"].append(fin)
            x = _out_ffn(geo, x, (of_p, ob_p, of_s, ob_s), mod, bf(w_o_d[j]), *ffn_w, gs=gs, gn=gn_d[j][None])

    stack = lambda xs: jnp.stack(xs, axis=1)
    return (x[:n_p].reshape(bp, tp, D_MODEL), x[n_p:].reshape(bs, ts, D_MODEL), stack(outs["ak"]), stack(outs["av"]),
            stack(outs["bk"]), stack(outs["bv"]), stack(outs["ck"]), stack(outs["cv"]), stack(outs["d"]))
```

```python
import functools
import math

import jax
import jax.numpy as jnp
from jax import lax
from jax.experimental import pallas as pl
from jax.experimental.pallas import tpu as pltpu

F32 = jnp.float32
BF16 = jnp.bfloat16

D_MODEL = 1024
HEAD_DIM = 64
GRID_W = 64
ROPE_THETA = 10000.0
WINDOW = 128
EPS = 1e-6
NEG_INF = -1e30
CHUNK = 64
D_KDIM = 128
D_HEADS = D_MODEL // D_KDIM
N_MOD = 6
MOD_ROWS = 8
QK_CHUNK = 256
V7X_VMEM_BYTES = 64 * 1024 * 1024
VMEM_LIMIT = V7X_VMEM_BYTES - 8 * 1024 * 1024


def _dot(a, b):
    return jnp.dot(a, b, preferred_element_type=F32)


def _dot_nt(a, b):
    return lax.dot_general(a, b, (((1,), (1,)), ((), ())), preferred_element_type=F32)


def _dot_tn(a, b):
    return lax.dot_general(a, b, (((0,), (0,)), ((), ())), preferred_element_type=F32)


def _silu(x):
    return x * jax.nn.sigmoid(x)


def _params(n_axes, **kw):
    return pltpu.CompilerParams(dimension_semantics=("arbitrary",) * n_axes, vmem_limit_bytes=VMEM_LIMIT, **kw)


def _const_spec(shape):
    return pl.BlockSpec(shape, lambda *_: (0,) * len(shape), pipeline_mode=pl.Buffered(1))


def _norm_mod(x, gain, shift, scale):
    ms = jnp.mean(x * x, axis=-1, keepdims=True)
    return (x * lax.rsqrt(ms + EPS) * gain) * (1.0 + scale) + shift


def _split2(x):
    hi = x.astype(BF16)
    lo = (x - hi.astype(F32)).astype(BF16)
    return hi, lo


def _split3(x):
    hi = x.astype(BF16)
    r = x - hi.astype(F32)
    mid = r.astype(BF16)
    lo = (r - mid.astype(F32)).astype(BF16)
    return hi, mid, lo


def _group_mean_sq(y, gmat):
    hi, lo = _split2(y * y)
    return (_dot(hi, gmat) + _dot(lo, gmat)) * (1.0 / HEAD_DIM)


def _tile4(y, grp):
    outs = []
    for g in range(4):
        t = jnp.where(grp == g, y, 0.0)
        u = t + pltpu.roll(t, 128, 1)
        outs.append(u + pltpu.roll(u, 64, 1))
    return outs


def _ada_kernel(cond_ref, w_ref, b_ref, o_ref):
    a = _silu(cond_ref[...]).astype(BF16)
    o_ref[...] = _dot(a, w_ref[...].astype(BF16)) + b_ref[...]


def _ada_all(cond, w_ada, b_ada):
    depth, d, n = w_ada.shape
    tn = n // 4
    return pl.pallas_call(
        _ada_kernel,
        out_shape=jax.ShapeDtypeStruct((depth, MOD_ROWS, n), F32),
        grid=(depth, n // tn),
        in_specs=[
            pl.BlockSpec((MOD_ROWS, d), lambda l, j: (0, 0)),
            pl.BlockSpec((None, d, tn), lambda l, j: (l, 0, j)),
            pl.BlockSpec((None, 1, tn), lambda l, j: (l, 0, j)),
        ],
        out_specs=pl.BlockSpec((None, MOD_ROWS, tn), lambda l, j: (l, 0, j)),
        compiler_params=_params(2),
        name="ada_mod",
    )(cond, w_ada, b_ada.reshape(depth, 1, n))


class _Geom:
    def __init__(self, bp, tp, bs, ts, past):
        self.bp, self.tp, self.bs, self.ts, self.past = bp, tp, bs, ts, past
        self.n_p = bp * tp
        self.n_s = bs * ts
        self.n_t = self.n_p + self.n_s
        self.tm = math.gcd(512, math.gcd(self.n_p, ts))
        assert self.n_p % ts == 0 and ts % CHUNK == 0 and tp % CHUNK == 0
        assert bs + 1 <= MOD_ROWS

    def mod_row(self, i):
        return jnp.where(i * self.tm < self.n_p, 0, 1 + (i * self.tm - self.n_p) // self.ts)

    def prompt_tile(self, i):
        return jnp.minimum(i, self.n_p // self.tm - 1)

    def sample_tile(self, i):
        return jnp.maximum(i - self.n_p // self.tm, 0)

    def rope_tile(self, i):
        return self.sample_tile(i) % (self.ts // self.tm)


def _mod_spec(geo):
    return pl.BlockSpec((None, 1, N_MOD * D_MODEL), lambda i: (geo.mod_row(i), 0, 0))


def _qkv_kernel(x_ref, mod_ref, gain_ref, w_ref, gmat_ref, qg_ref, kg_ref, cos_ref, sin_ref,
                q_ref, k_ref, v_ref, k32_ref, v32_ref, h_scr, *, tm, n_p, n_q, n_k, n_v, tiled):
    i = pl.program_id(0)
    is_sample = i * tm >= n_p
    is_prompt = jnp.logical_not(is_sample)
    mod = mod_ref[...]
    h = _norm_mod(x_ref[...], gain_ref[...], mod[:, 0:D_MODEL], mod[:, D_MODEL:2 * D_MODEL])
    h_scr[...] = h.astype(BF16)
    lane = lax.broadcasted_iota(jnp.int32, (tm, QK_CHUNK), 1)
    first_half = (lane % 32) < 16
    grp = lane // HEAD_DIM

    def rope(y):
        partner = jnp.where(first_half, pltpu.roll(y, QK_CHUNK - 16, 1), pltpu.roll(y, 16, 1))
        return y * cos_ref[...] + partner * sin_ref[...]

    def normed(c0, g_ref):
        y = _dot(h_scr[...], w_ref[:, c0:c0 + QK_CHUNK])
        return y * lax.rsqrt(_group_mean_sq(y, gmat_ref[...]) + EPS) * g_ref[...]

    for c in range(n_q // QK_CHUNK):
        cs = slice(c * QK_CHUNK, (c + 1) * QK_CHUNK)
        yn = normed(c * QK_CHUNK, qg_ref)

        @pl.when(is_sample)
        def _():
            q_ref[:, cs] = rope(yn).astype(BF16)

        @pl.when(is_prompt)
        def _():
            q_ref[:, cs] = yn.astype(BF16)

    for c in range(n_k // QK_CHUNK):
        cs = slice(c * QK_CHUNK, (c + 1) * QK_CHUNK)
        yn = normed(n_q + c * QK_CHUNK, kg_ref)

        @pl.when(is_sample)
        def _():
            yr = rope(yn)
            if tiled:
                for g, t in enumerate(_tile4(yr, grp)):
                    k_ref[:, g * QK_CHUNK:(g + 1) * QK_CHUNK] = t.astype(BF16)
            else:
                k_ref[:, cs] = yr.astype(BF16)

        @pl.when(is_prompt)
        def _():
            k32_ref[:, cs] = yn
            if tiled:
                for g, t in enumerate(_tile4(yn, grp)):
                    k_ref[:, g * QK_CHUNK:(g + 1) * QK_CHUNK] = t.astype(BF16)
            else:
                k_ref[:, cs] = yn.astype(BF16)

    for c in range(n_v // QK_CHUNK):
        cs = slice(c * QK_CHUNK, (c + 1) * QK_CHUNK)
        c0 = n_q + n_k + c * QK_CHUNK
        y = _dot(h_scr[...], w_ref[:, c0:c0 + QK_CHUNK])
        if tiled:
            for g, t in enumerate(_tile4(y, grp)):
                v_ref[:, g * QK_CHUNK:(g + 1) * QK_CHUNK] = t.astype(BF16)
        else:
            v_ref[:, cs] = y.astype(BF16)

        @pl.when(is_prompt)
        def _():
            v32_ref[:, cs] = y


def _qkv_proj(geo, x, mod, gain, w, gmat, q_gain, k_gain, cos, sin, *, n_q, n_k, n_v, tiled):
    tm = geo.tm
    n_kt = 4 * n_k if tiled else n_k
    n_vt = 4 * n_v if tiled else n_v
    row = lambda i: (i, 0)
    prow = lambda i: (geo.prompt_tile(i), 0)
    vec = lambda n: pl.BlockSpec((1, n), lambda i: (0, 0))
    kern = functools.partial(_qkv_kernel, tm=tm, n_p=geo.n_p, n_q=n_q, n_k=n_k, n_v=n_v, tiled=tiled)
    return pl.pallas_call(
        kern,
        out_shape=(
            jax.ShapeDtypeStruct((geo.n_t, n_q), BF16),
            jax.ShapeDtypeStruct((geo.n_t, n_kt), BF16),
            jax.ShapeDtypeStruct((geo.n_t, n_vt), BF16),
            jax.ShapeDtypeStruct((geo.n_p, n_k), F32),
            jax.ShapeDtypeStruct((geo.n_p, n_v), F32),
        ),
        grid=(geo.n_t // tm,),
        in_specs=[
            pl.BlockSpec((tm, D_MODEL), row),
            _mod_spec(geo),
            vec(D_MODEL),
            _const_spec((D_MODEL, n_q + n_k + n_v)),
            _const_spec((QK_CHUNK, QK_CHUNK)),
            vec(QK_CHUNK),
            vec(QK_CHUNK),
            pl.BlockSpec((tm, QK_CHUNK), lambda i: (geo.rope_tile(i), 0)),
            pl.BlockSpec((tm, QK_CHUNK), lambda i: (geo.rope_tile(i), 0)),
        ],
        out_specs=(
            pl.BlockSpec((tm, n_q), row),
            pl.BlockSpec((tm, n_kt), row),
            pl.BlockSpec((tm, n_vt), row),
            pl.BlockSpec((tm, n_k), prow),
            pl.BlockSpec((tm, n_v), prow),
        ),
        scratch_shapes=[pltpu.VMEM((tm, D_MODEL), BF16)],
        compiler_params=_params(1),
        name="qkv_proj",
    )(x, mod, gain, w, gmat, q_gain, k_gain, cos, sin)


def _flash_kernel(*refs, R, tq, kb, n_own, t_len, vt_cached, has_ctx, ctx_tiled, use_sink, window, diff,
                  lam_scale):
    refs = list(refs)
    q_ref, k_ref, v_ref = refs[:3]
    del refs[:3]
    if has_ctx:
        ck_ref, cv_ref = refs[:2]
        del refs[:2]
    if use_sink:
        sink_ref = refs.pop(0)
    if diff:
        lam_ref, subg_ref = refs[:2]
        del refs[:2]
    o_ref, qs_scr = refs[:2]
    del refs[:2]
    multi = n_own > 1
    if vt_cached:
        vt_scr = refs.pop(0)
    if multi:
        acc_scr, m_scr, l_scr = refs[:3]
        del refs[:3]
    if has_ctx:
        ctk_scr, ctvt_scr = refs

    W = HEAD_DIM * R
    M = R * tq
    c = pl.program_id(1)
    qi = pl.program_id(2)

    q = q_ref[...].astype(F32)
    lane = lax.broadcasted_iota(jnp.int32, (tq, W), 1) // HEAD_DIM
    for r in range(R):
        qs_scr[r * tq:(r + 1) * tq, :] = jnp.where(lane == r, q, 0.0).astype(BF16)

    def transposed(v):
        return v.astype(F32).T.astype(BF16)

    def block(kblk, vt, mask, state, sink_row):
        st = _dot_nt(kblk, qs_scr[...])
        if mask is not None:
            st = jnp.where(mask, st, NEG_INF)
        m_new = jnp.max(st, axis=0, keepdims=True)
        if state is not None:
            m_prev, l_prev, acc_prev = state
            m_new = jnp.maximum(m_new, m_prev)
        elif sink_row is not None:
            m_new = jnp.maximum(m_new, sink_row)
        p = jnp.exp(st - m_new)
        l_new = jnp.sum(p, axis=0, keepdims=True)
        acc = _dot(vt, p.astype(BF16))
        if state is not None:
            alpha = jnp.exp(m_prev - m_new)
            l_new = alpha * l_prev + l_new
            acc = alpha * acc_prev + acc
        elif sink_row is not None:
            l_new = l_new + jnp.exp(sink_row - m_new)
        return m_new, l_new, acc

    sink_row = None
    if use_sink:
        sink_row = jnp.concatenate([jnp.full((1, tq), sink_ref[c * R + r], F32) for r in range(R)], axis=1)

    state = None
    if has_ctx:
        @pl.when(qi == 0)
        def _():
            ck, cv = ck_ref[...], cv_ref[...]
            if ctx_tiled:
                grp = lax.broadcasted_iota(jnp.int32, ck.shape, 1) // HEAD_DIM
                tiled = []
                for src in (ck, cv):
                    t = jnp.where(grp == c, src, 0.0)
                    u = t + pltpu.roll(t, 128, 1)
                    tiled.append(u + pltpu.roll(u, 64, 1))
                ck, cv = tiled
            ctk_scr[...] = ck.astype(BF16)
            ctvt_scr[...] = cv.T.astype(BF16)

        state = block(ctk_scr[...], ctvt_scr[...], None, state, sink_row)

    if vt_cached:
        @pl.when(qi == 0)
        def _():
            vt_scr[...] = transposed(v_ref[...])

    def vt_block(off):
        if vt_cached:
            return vt_scr[:, pl.ds(off, kb)]
        return transposed(v_ref[pl.ds(off, kb), :])

    if window:
        q0 = qi * tq
        start = pl.multiple_of(jnp.clip(q0 - WINDOW, 0, t_len - kb), WINDOW)
        kpos = start + lax.broadcasted_iota(jnp.int32, (kb, M), 0)
        qpos = q0 + lax.broadcasted_iota(jnp.int32, (kb, M), 1) % tq
        mask = jnp.abs(qpos - kpos) <= WINDOW
        state = block(k_ref[pl.ds(start, kb), :], vt_block(start), mask, state, sink_row)
    elif not multi:
        state = block(k_ref[...], vt_block(0), None, state, sink_row)
    else:
        first = 0
        if state is None:
            state = block(k_ref[0:kb, :], vt_block(0), None, None, sink_row)
            first = 1
        m_scr[...], l_scr[...], acc_scr[...] = state

        def body(j, carry):
            off = pl.multiple_of(j * kb, kb)
            prev = (m_scr[...], l_scr[...], acc_scr[...])
            m_scr[...], l_scr[...], acc_scr[...] = block(
                k_ref[pl.ds(off, kb), :], vt_block(off), None, prev, None)
            return carry

        lax.fori_loop(first, n_own, body, 0)
        state = (m_scr[...], l_scr[...], acc_scr[...])

    _, l_fin, acc = state
    a = acc * (1.0 / l_fin)
    if diff:
        ot = a[:, :tq] - lam_ref[0] * a[:, tq:]
        ms = jnp.mean(ot * ot, axis=0, keepdims=True)
        o = (ot * lax.rsqrt(ms + EPS)).T * (subg_ref[...] * lam_scale)
    else:
        o = jnp.concatenate([a[r * HEAD_DIM:(r + 1) * HEAD_DIM, r * tq:(r + 1) * tq] for r in range(R)], axis=0).T
    o_ref[...] = o.astype(BF16)


def _flash(q, k, v, *, row0, n_b, t_len, tq, R, ctx=None, ctx_tiled=False, sink=None, window=False,
           lam=None, subg=None, lam_scale=1.0, name):
    W = HEAD_DIM * R
    n_c = D_MODEL // W
    n_qb = t_len // tq
    diff = lam is not None
    if window:
        kb, n_own = tq + 2 * WINDOW, 1
    else:
        kb = min(t_len, 512)
        n_own = t_len // kb
    q_base, k_base = row0 // tq, row0 // t_len
    q_map = lambda b, c, qi: (q_base + b * n_qb + qi, c)
    kv_map = lambda b, c, qi: (k_base + b, c)
    in_specs = [pl.BlockSpec((tq, W), q_map), pl.BlockSpec((t_len, W), kv_map), pl.BlockSpec((t_len, W), kv_map)]
    args = [q, k, v]
    M = R * tq
    vt_cached = n_qb > 1
    scratch = [pltpu.VMEM((M, W), BF16)]
    if vt_cached:
        scratch.append(pltpu.VMEM((W, t_len), BF16))
    if n_own > 1:
        scratch += [pltpu.VMEM((W, M), F32), pltpu.VMEM((1, M), F32), pltpu.VMEM((1, M), F32)]
    if ctx is not None:
        ck, cv = ctx
        p_len = ck.shape[1]
        cmap = (lambda b, c, qi: (b, 0, 0)) if ctx_tiled else (lambda b, c, qi: (b, 0, c))
        scratch += [pltpu.VMEM((p_len, W), BF16), pltpu.VMEM((W, p_len), BF16)]
        in_specs += [pl.BlockSpec((None, p_len, W), cmap)] * 2
        args += [ck, cv]
    if sink is not None:
        in_specs.append(pl.BlockSpec(memory_space=pltpu.SMEM))
        args.append(sink)
    if diff:
        in_specs += [pl.BlockSpec(memory_space=pltpu.SMEM), pl.BlockSpec((1, W), lambda b, c, qi: (0, 0))]
        args += [lam, subg]
    kern = functools.partial(
        _flash_kernel, R=R, tq=tq, kb=kb, n_own=n_own, t_len=t_len, vt_cached=vt_cached, has_ctx=ctx is not None,
        ctx_tiled=ctx_tiled, use_sink=sink is not None, window=window, diff=diff, lam_scale=lam_scale)
    return pl.pallas_call(
        kern,
        out_shape=jax.ShapeDtypeStruct((n_b * t_len, D_MODEL), BF16),
        grid=(n_b, n_c, n_qb),
        in_specs=in_specs,
        out_specs=pl.BlockSpec((tq, W), lambda b, c, qi: (b * n_qb + qi, c)),
        scratch_shapes=scratch,
        compiler_params=_params(3),
        name=name,
    )(*args)


def _proj_d_kernel(x_ref, mod_ref, gain_ref, w_ref, lb_ref, qs_ref, lff_ref, lfb_ref, v_ref, gs_ref, h_scr, *, cw):
    mod = mod_ref[...]
    h = _norm_mod(x_ref[...], gain_ref[...], mod[:, 0:D_MODEL], mod[:, D_MODEL:2 * D_MODEL])
    h_scr[...] = h.astype(BF16)
    n_c = D_MODEL // cw

    def proj(part, c):
        c0 = part * D_MODEL + c * cw
        return _dot(h_scr[...], w_ref[:, c0:c0 + cw])

    for c in range(n_c):
        cs = slice(c * cw, (c + 1) * cw)
        qs_ref[:, cs] = _silu(proj(0, c))
        for d, ref in ((0, lff_ref), (1, lfb_ref)):
            lb = lb_ref[d:d + 1, cs]
            ref[:, cs] = jnp.log(lb + (1.0 - lb) * jax.nn.sigmoid(proj(1 + d, c)))
        v_ref[:, cs] = proj(3, c).astype(BF16)
        gs_ref[:, cs] = _silu(proj(4, c))


def _proj_d(geo, x, mod, gain, w, lb):
    tm = geo.tm
    row = lambda i: (i, 0)
    spec = pl.BlockSpec((tm, D_MODEL), row)
    f32_out = jax.ShapeDtypeStruct((geo.n_t, D_MODEL), F32)
    return pl.pallas_call(
        functools.partial(_proj_d_kernel, cw=512),
        out_shape=(f32_out, f32_out, f32_out, jax.ShapeDtypeStruct((geo.n_t, D_MODEL), BF16), f32_out),
        grid=(geo.n_t // tm,),
        in_specs=[spec, _mod_spec(geo), pl.BlockSpec((1, D_MODEL), lambda i: (0, 0)),
                  _const_spec((D_MODEL, 5 * D_MODEL)), pl.BlockSpec((2, D_MODEL), lambda i: (0, 0))],
        out_specs=(spec,) * 5,
        scratch_shapes=[pltpu.VMEM((tm, D_MODEL), BF16)],
        compiler_params=_params(1),
        name="hgrn_proj",
    )(x, mod, gain, w, lb)


def _hgrn_kernel(*refs, n_chunks, has_init, want_final):
    refs = list(refs)
    tri_ref = refs.pop(0)
    ins = [refs[:3], refs[3:6]]
    del refs[:6]
    if has_init:
        s0_ref = refs.pop(0)
    o_refs = refs[:2]
    del refs[:2]
    if want_final:
        fin_ref = refs.pop(0)
    st_scr = refs.pop(0)
    n = pl.program_id(1)

    @pl.when(n == 0)
    def _():
        st_scr[...] = s0_ref[...] if has_init else jnp.zeros_like(st_scr)

    row = lax.broadcasted_iota(jnp.int32, (CHUNK, CHUNK), 0)
    col = lax.broadcasted_iota(jnp.int32, (CHUNK, CHUNK), 1)
    for d in range(2):
        qs_ref, lf_ref, v_ref = ins[d]
        tri = tri_ref[d]
        lf = lf_ref[...]
        cum = sum(_dot(tri, part) for part in _split3(lf))
        last = cum[CHUNK - 1:CHUNK] if d == 0 else cum[0:1]
        e_last = jnp.exp(last)
        q_dec = (qs_ref[...] * jnp.exp(cum)).astype(BF16)
        k_in = (1.0 - jnp.exp(lf)) * jnp.exp(-cum)
        k_out = (k_in * e_last).astype(BF16)
        k_in = k_in.astype(BF16)
        keep = (col <= row) if d == 0 else (col >= row)
        for hd in range(D_HEADS):
            hs = slice(hd * D_KDIM, (hd + 1) * D_KDIM)
            st = st_scr[d, hd]
            att = jnp.where(keep, _dot_nt(q_dec[:, hs], k_in[:, hs]), 0.0).astype(BF16)
            v = v_ref[:, hs]
            o_refs[d][:, hs] = _dot(att, v) + _dot_nt(q_dec[:, hs], st.astype(BF16))
            st_scr[d, hd] = st * e_last[:, hs] + _dot_tn(v, k_out[:, hs])

    if want_final:
        @pl.when(n == n_chunks - 1)
        def _():
            for d in range(2):
                for hd in range(D_HEADS):
                    fin_ref[d, hd] = st_scr[d, hd].T


def _hgrn_scan(tri, qs, lff, lfb, v, *, row0, n_b, t_len, s0t=None, want_final=False, name):
    n_chunks = t_len // CHUNK
    base = row0 // CHUNK
    fwd = lambda b, n: (base + b * n_chunks + n, 0)
    bwd = lambda b, n: (base + b * n_chunks + n_chunks - 1 - n, 0)
    blk = lambda m: pl.BlockSpec((CHUNK, D_MODEL), m)
    st_shape = (2, D_HEADS, D_KDIM, D_KDIM)
    in_specs = [pl.BlockSpec((2, CHUNK, CHUNK), lambda b, n: (0, 0, 0)),
                blk(fwd), blk(fwd), blk(fwd), blk(bwd), blk(bwd), blk(bwd)]
    args = [tri, qs, lff, v, qs, lfb, v]
    if s0t is not None:
        in_specs.append(pl.BlockSpec((None,) + st_shape, lambda b, n: (b, 0, 0, 0, 0)))
        args.append(s0t)
    o_shape = jax.ShapeDtypeStruct((n_b * t_len, D_MODEL), F32)
    out_shape = [o_shape, o_shape]
    out_specs = [pl.BlockSpec((CHUNK, D_MODEL), lambda b, n: (b * n_chunks + n, 0)),
                 pl.BlockSpec((CHUNK, D_MODEL), lambda b, n: (b * n_chunks + n_chunks - 1 - n, 0))]
    if want_final:
        out_shape.append(jax.ShapeDtypeStruct((n_b,) + st_shape, F32))
        out_specs.append(pl.BlockSpec((None,) + st_shape, lambda b, n: (b, 0, 0, 0, 0)))
    kern = functools.partial(_hgrn_kernel, n_chunks=n_chunks, has_init=s0t is not None, want_final=want_final)
    return pl.pallas_call(
        kern,
        out_shape=tuple(out_shape),
        grid=(n_b, n_chunks),
        in_specs=in_specs,
        out_specs=tuple(out_specs),
        scratch_shapes=[pltpu.VMEM(st_shape, F32)],
        compiler_params=_params(2),
        name=name,
    )(*args)


def _ffn_chunks(d_ff):
    chunks, f0 = [], 0
    while f0 < d_ff:
        fs = min(512, d_ff - f0)
        chunks.append((f0, fs))
        f0 += fs
    return tuple(chunks)


def _out_ffn_kernel(*refs, tm, n_p, hgrn, f_chunks):
    refs = list(refs)
    x_ref = refs.pop(0)
    n_o = 4 if hgrn else 2
    o_in = refs[:n_o]
    del refs[:n_o]
    if hgrn:
        gs_ref, gn_ref = refs[:2]
        del refs[:2]
    mod_ref, wo_ref, ng_ref, wg_ref, wu_ref, wd_ref, out_ref, h_scr, acc_scr = refs
    is_sample = pl.program_id(0) * tm >= n_p
    mod = mod_ref[...]
    m = [mod[:, j * D_MODEL:(j + 1) * D_MODEL] for j in range(N_MOD)]

    if hgrn:
        ofp, obp, ofs, obs = o_in
        o = jnp.where(is_sample, ofs[...] + obs[...], ofp[...] + obp[...])
        parts = []
        for hd in range(D_HEADS):
            hs = slice(hd * D_KDIM, (hd + 1) * D_KDIM)
            oh = o[:, hs]
            ms = jnp.mean(oh * oh, axis=-1, keepdims=True)
            parts.append(oh * lax.rsqrt(ms + EPS) * gn_ref[...] * gs_ref[:, hs])
        o = jnp.concatenate(parts, axis=-1).astype(BF16)
    else:
        o = jnp.where(is_sample, o_in[1][...], o_in[0][...])

    x1 = x_ref[...] + m[2] * _dot(o, wo_ref[...])
    out_ref[...] = x1
    h_scr[...] = _norm_mod(x1, ng_ref[...], m[3], m[4]).astype(BF16)
    for idx, (f0, fs) in enumerate(f_chunks):
        hb = h_scr[...]
        g = _dot(hb, wg_ref[:, f0:f0 + fs])
        u = _dot(hb, wu_ref[:, f0:f0 + fs])
        y = _dot((_silu(g) * u).astype(BF16), wd_ref[f0:f0 + fs, :])
        acc_scr[...] = y if idx == 0 else acc_scr[...] + y
    out_ref[...] = out_ref[...] + m[5] * acc_scr[...]


def _out_ffn(geo, x, o_parts, mod, w_o, norm_gain, w_gate, w_up, w_down, gs=None, gn=None):
    tm = geo.tm
    hgrn = gs is not None
    d_ff = w_gate.shape[1]
    row = lambda i: (i, 0)
    prow = lambda i: (geo.prompt_tile(i), 0)
    srow = lambda i: (geo.sample_tile(i), 0)
    blk = lambda m: pl.BlockSpec((tm, D_MODEL), m)
    in_specs = [blk(row)]
    args = [x]
    if hgrn:
        in_specs += [blk(prow), blk(prow), blk(srow), blk(srow), blk(row), pl.BlockSpec((1, D_KDIM), lambda i: (0, 0))]
        args += list(o_parts) + [gs, gn]
    else:
        in_specs += [blk(prow), blk(srow)]
        args += list(o_parts)
    in_specs += [_mod_spec(geo), _const_spec((D_MODEL, D_MODEL)), pl.BlockSpec((1, D_MODEL), lambda i: (0, 0)),
                 _const_spec((D_MODEL, d_ff)), _const_spec((D_MODEL, d_ff)), _const_spec((d_ff, D_MODEL))]
    args += [mod, w_o, norm_gain, w_gate, w_up, w_down]
    kern = functools.partial(_out_ffn_kernel, tm=tm, n_p=geo.n_p, hgrn=hgrn, f_chunks=_ffn_chunks(d_ff))
    return pl.pallas_call(
        kern,
        out_shape=jax.ShapeDtypeStruct((geo.n_t, D_MODEL), F32),
        grid=(geo.n_t // tm,),
        in_specs=in_specs,
        out_specs=blk(row),
        scratch_shapes=[pltpu.VMEM((tm, D_MODEL), BF16), pltpu.VMEM((tm, D_MODEL), F32)],
        compiler_params=_params(1),
        name="out_ffn",
    )(*args)


def _rope_tables(n_tokens):
    rows = n_tokens // GRID_W
    row = jnp.repeat(jnp.arange(rows, dtype=F32), GRID_W)
    col = jnp.tile(jnp.arange(GRID_W, dtype=F32), rows)
    axis_dim = HEAD_DIM // 2
    inv_freq = ROPE_THETA ** (-jnp.arange(0, axis_dim, 2, dtype=F32) / axis_dim)
    ang = jnp.stack([row[:, None] * inv_freq, col[:, None] * inv_freq], axis=1)
    cos, sin = jnp.cos(ang), jnp.sin(ang)
    cos64 = jnp.concatenate([cos, cos], axis=-1).reshape(n_tokens, HEAD_DIM)
    sin64 = jnp.concatenate([-sin, sin], axis=-1).reshape(n_tokens, HEAD_DIM)
    reps = QK_CHUNK // HEAD_DIM
    return jnp.tile(cos64, (1, reps)), jnp.tile(sin64, (1, reps))


def _group_matrix():
    g = jnp.arange(QK_CHUNK) // HEAD_DIM
    return (g[:, None] == g[None, :]).astype(BF16)


def _tri_matrices():
    r = jnp.arange(CHUNK)
    lower = (r[None, :] <= r[:, None]).astype(BF16)
    return jnp.stack([lower, lower.T])


def kernel(x_prompt, x_sample, c, cache_a_k, cache_a_v, cache_b_k, cache_b_v, cache_c_k, cache_c_v, state_d, c_ctx, norm_mix, norm_ffn, w_ada, b_ada, w_ffn_gate, w_ffn_up, w_ffn_down, w_qkv_a, w_o_a, qn_a, kn_a, subln_a, lam_q1_a, lam_k1_a, lam_q2_a, lam_k2_a, w_qkv_b, w_o_b, qn_b, kn_b, sink_b, w_qkv_c, w_o_c, qn_c, kn_c, w_in_d, w_o_d, gn_d, lb_logits_d):
    bp, tp, _ = x_prompt.shape
    bs, ts, _ = x_sample.shape
    past = cache_a_k.shape[2]
    depth = w_ada.shape[0]
    geo = _Geom(bp, tp, bs, ts, past)
    n_p = geo.n_p

    cond = jnp.concatenate([c_ctx[None], c, jnp.zeros((MOD_ROWS - 1 - bs, D_MODEL), F32)], axis=0)
    mod_all = _ada_all(cond, w_ada, b_ada).reshape(depth, MOD_ROWS, 1, N_MOD * D_MODEL)

    cos, sin = _rope_tables(ts)
    gmat = _group_matrix()
    tri = _tri_matrices()
    p_lb = jax.nn.softmax(lb_logits_d.astype(F32), axis=1)
    lb_all = jnp.cumsum(p_lb, axis=1) - p_lb[:, :1]
    scale = HEAD_DIM ** -0.5
    tile_gain = lambda g, s=1.0: jnp.tile(g * s, QK_CHUNK // HEAD_DIM)[None]
    bf = lambda w: w.astype(BF16)

    x = jnp.concatenate([x_prompt.reshape(n_p, D_MODEL), x_sample.reshape(geo.n_s, D_MODEL)], axis=0)
    outs = {k: [] for k in ("ak", "av", "bk", "bv", "ck", "cv", "d")}
    for li in range(depth):
        kind, j = li % 4, li // 4
        mod = mod_all[li]
        gain = norm_mix[li][None]
        ffn_w = (norm_ffn[li][None], bf(w_ffn_gate[li]), bf(w_ffn_up[li]), bf(w_ffn_down[li]))
        if kind == 0:
            lam_init = 0.8 - 0.6 * math.exp(-0.3 * li)
            lam = (jnp.exp(jnp.sum(lam_q1_a[j] * lam_k1_a[j])) - jnp.exp(jnp.sum(lam_q2_a[j] * lam_k2_a[j]))
                   + lam_init).astype(F32).reshape(1)
            q, k, v, k32, v32 = _qkv_proj(geo, x, mod, gain, bf(w_qkv_a[j]), gmat, tile_gain(qn_a[j], scale),
                                          tile_gain(kn_a[j]), cos, sin, n_q=D_MODEL, n_k=D_MODEL, n_v=D_MODEL,
                                          tiled=False)
            common = dict(R=2, lam=lam, subg=subln_a[j][None], lam_scale=1.0 - lam_init)
            o_p = _flash(q, k, v, row0=0, n_b=bp, t_len=tp, tq=tp, name="attn_a_prompt", **common)
            ctx = (cache_a_k[:, j].reshape(bs, past, D_MODEL), cache_a_v[:, j].reshape(bs, past, D_MODEL))
            o_s = _flash(q, k, v, row0=n_p, n_b=bs, t_len=ts, tq=min(512, ts), ctx=ctx, name="attn_a_sample",
                         **common)
            outs["ak"].append(k32.reshape(bp, tp, 2 * D_MODEL // (2 * HEAD_DIM), HEAD_DIM))
            outs["av"].append(v32.reshape(bp, tp, D_MODEL // (2 * HEAD_DIM), 2 * HEAD_DIM))
            x = _out_ffn(geo, x, (o_p, o_s), mod, bf(w_o_a[j]), *ffn_w)
        elif kind in (1, 2):
            w_qkv, w_o, qn, kn = (w_qkv_b, w_o_b, qn_b, kn_b) if kind == 1 else (w_qkv_c, w_o_c, qn_c, kn_c)
            cache_k, cache_v = (cache_b_k, cache_b_v) if kind == 1 else (cache_c_k, cache_c_v)
            n_kv = w_qkv.shape[-1] - D_MODEL
            q, k, v, k32, v32 = _qkv_proj(geo, x, mod, gain, bf(w_qkv[j]), gmat, tile_gain(qn[j], scale),
                                          tile_gain(kn[j]), cos, sin, n_q=D_MODEL, n_k=n_kv // 2, n_v=n_kv // 2,
                                          tiled=True)
            sink = sink_b[j].astype(F32) if kind == 1 else None
            ctx = (cache_k[:, j].reshape(bs, past, n_kv // 2), cache_v[:, j].reshape(bs, past, n_kv // 2))
            tag = "b" if kind == 1 else "c"
            o_p = _flash(q, k, v, row0=0, n_b=bp, t_len=tp, tq=tp, R=4, sink=sink, name=f"attn_{tag}_prompt")
            o_s = _flash(q, k, v, row0=n_p, n_b=bs, t_len=ts, tq=min(256, ts), R=4, ctx=ctx, ctx_tiled=True,
                         sink=sink, window=kind == 1, name=f"attn_{tag}_sample")
            kv_heads = n_kv // 2 // HEAD_DIM
            key = "b" if kind == 1 else "c"
            outs[key + "k"].append(k32.reshape(bp, tp, kv_heads, HEAD_DIM))
            outs[key + "v"].append(v32.reshape(bp, tp, kv_heads, HEAD_DIM))
            x = _out_ffn(geo, x, (o_p, o_s), mod, bf(w_o[j]), *ffn_w)
        else:
            qs, lff, lfb, v, gs = _proj_d(geo, x, mod, gain, bf(w_in_d[j]), lb_all[:, li])
            of_p, ob_p, fin = _hgrn_scan(tri, qs, lff, lfb, v, row0=0, n_b=bp, t_len=tp, want_final=True,
                                         name="hgrn_prompt")
            s0t = jnp.swapaxes(state_d[:, j], -1, -2)
            of_s, ob_s = _hgrn_scan(tri, qs, lff, lfb, v, row0=n_p, n_b=bs, t_len=ts, s0t=s0t, name="hgrn_sample")
            outs["d"].append(fin)
            x = _out_ffn(geo, x, (of_p, ob_p, of_s, ob_s), mod, bf(w_o_d[j]), *ffn_w, gs=gs, gn=gn_d[j][None])

    stack = lambda xs: jnp.stack(xs, axis=1)
    return (x[:n_p].reshape(bp, tp, D_MODEL), x[n_p:].reshape(bs, ts, D_MODEL), stack(outs["ak"]), stack(outs["av"]),
            stack(outs["bk"]), stack(outs["bv"]), stack(outs["ck"]), stack(outs["cv"]), stack(outs["d"]))
```

```python
import functools
import math

import jax
import jax.numpy as jnp
from jax import lax
from jax.experimental import pallas as pl
from jax.experimental.pallas import tpu as pltpu

F32 = jnp.float32
BF16 = jnp.bfloat16

D_MODEL = 1024
HEAD_DIM = 64
GRID_W = 64
ROPE_THETA = 10000.0
WINDOW = 128
EPS = 1e-6
NEG_INF = -1e30
LOG2E = 1.4426950408889634
CHUNK = 64
HGRN_CHUNKS_PER_STEP = 4
D_KDIM = 128
D_HEADS = D_MODEL // D_KDIM
N_MOD = 6
MOD_ROWS = 8
COL_CHUNK = 512
NORM_CHUNK = 256
V7X_VMEM_BYTES = 64 * 1024 * 1024
VMEM_LIMIT = V7X_VMEM_BYTES - 8 * 1024 * 1024


def _dot(a, b):
    return jnp.dot(a, b, preferred_element_type=F32)


def _dot_nt(a, b):
    return lax.dot_general(a, b, (((1,), (1,)), ((), ())), preferred_element_type=F32)


def _dot_tn(a, b):
    return lax.dot_general(a, b, (((0,), (0,)), ((), ())), preferred_element_type=F32)


def _silu(x):
    return x * jax.nn.sigmoid(x)


def _params(n_axes, **kw):
    return pltpu.CompilerParams(dimension_semantics=("arbitrary",) * n_axes, vmem_limit_bytes=VMEM_LIMIT, **kw)


def _const_spec(shape):
    return pl.BlockSpec(shape, lambda *_: (0,) * len(shape), pipeline_mode=pl.Buffered(1))


def _norm_mod(x, gain, shift, scale):
    ms = jnp.mean(x * x, axis=-1, keepdims=True)
    return (x * lax.rsqrt(ms + EPS) * gain) * (1.0 + scale) + shift


def _split3(x):
    hi = x.astype(BF16)
    r = x - hi.astype(F32)
    mid = r.astype(BF16)
    lo = (r - mid.astype(F32)).astype(BF16)
    return hi, mid, lo


def _ada_kernel(cond_ref, w_ref, b_ref, o_ref):
    a = _silu(cond_ref[...]).astype(BF16)
    o_ref[...] = _dot(a, w_ref[...].astype(BF16)) + b_ref[...]


def _ada_all(cond, w_ada, b_ada):
    depth, d, n = w_ada.shape
    tn = n // 4
    return pl.pallas_call(
        _ada_kernel,
        out_shape=jax.ShapeDtypeStruct((depth, MOD_ROWS, n), F32),
        grid=(depth, n // tn),
        in_specs=[
            pl.BlockSpec((MOD_ROWS, d), lambda l, j: (0, 0)),
            pl.BlockSpec((None, d, tn), lambda l, j: (l, 0, j)),
            pl.BlockSpec((None, 1, tn), lambda l, j: (l, 0, j)),
        ],
        out_specs=pl.BlockSpec((None, MOD_ROWS, tn), lambda l, j: (l, 0, j)),
        compiler_params=_params(2),
        name="ada_mod",
    )(cond, w_ada, b_ada.reshape(depth, 1, n))


class _Geom:
    def __init__(self, bp, tp, bs, ts, past):
        self.bp, self.tp, self.bs, self.ts, self.past = bp, tp, bs, ts, past
        self.n_p = bp * tp
        self.n_s = bs * ts
        self.n_t = self.n_p + self.n_s
        self.tm = math.gcd(512, math.gcd(self.n_p, ts))
        assert self.n_p % ts == 0 and ts % CHUNK == 0 and tp % CHUNK == 0
        assert bs + 1 <= MOD_ROWS

    def mod_row(self, i):
        return jnp.where(i * self.tm < self.n_p, 0, 1 + (i * self.tm - self.n_p) // self.ts)

    def prompt_tile(self, i):
        return jnp.minimum(i, self.n_p // self.tm - 1)

    def sample_tile(self, i):
        return jnp.maximum(i - self.n_p // self.tm, 0)

    def rope_tile(self, i):
        return self.sample_tile(i) % (self.ts // self.tm)

    def row_spec(self, width=D_MODEL):
        return pl.BlockSpec((self.tm, width), lambda i: (i, 0))

    def prompt_spec(self, width=D_MODEL):
        return pl.BlockSpec((self.tm, width), lambda i: (self.prompt_tile(i), 0))

    def sample_spec(self, width=D_MODEL):
        return pl.BlockSpec((self.tm, width), lambda i: (self.sample_tile(i), 0))

    def mod_spec(self):
        return pl.BlockSpec((None, 1, N_MOD * D_MODEL), lambda i: (self.mod_row(i), 0, 0))

    def stream_specs(self, x):
        if isinstance(x, tuple):
            return [self.prompt_spec(), self.sample_spec()], list(x)
        return [self.row_spec()], [x]


def _vec_spec(n):
    return pl.BlockSpec((1, n), lambda i: (0, 0))


def _load_stream(x_refs, is_sample):
    if len(x_refs) == 2:
        return jnp.where(is_sample, x_refs[1][...], x_refs[0][...])
    return x_refs[0][...]


def _qkv_kernel(*refs, tm, n_p, n_x, n_q, n_k, n_v):
    refs = list(refs)
    x_refs = refs[:n_x]
    del refs[:n_x]
    (mod_ref, gain_ref, w_ref, gmat_ref, qg_ref, kg_ref, cos_ref, sin_ref,
     q_ref, k_ref, v_ref, k32_ref, v32_ref, h_scr) = refs
    i = pl.program_id(0)
    is_sample = i * tm >= n_p
    is_prompt = jnp.logical_not(is_sample)
    mod = mod_ref[...]
    h = _norm_mod(_load_stream(x_refs, is_sample), gain_ref[...], mod[:, 0:D_MODEL], mod[:, D_MODEL:2 * D_MODEL])
    h_scr[...] = h.astype(BF16)

    def head_norm(y, gain):
        y2 = (y * y).astype(BF16)
        ms = jnp.concatenate([_dot(y2[:, c:c + NORM_CHUNK], gmat_ref[...])
                              for c in range(0, y.shape[1], NORM_CHUNK)], axis=1) * (1.0 / HEAD_DIM)
        return y * lax.rsqrt(ms + EPS) * gain

    def rope(y):
        w = y.shape[1]
        lane = lax.broadcasted_iota(jnp.int32, y.shape, 1)
        partner = jnp.where(lane % 32 < 16, pltpu.roll(y, w - 16, 1), pltpu.roll(y, 16, 1))
        return y * cos_ref[:, :w] + partner * sin_ref[:, :w]

    def store_qk(y, dst, dst32, cs):
        @pl.when(is_sample)
        def _():
            dst[:, cs] = rope(y).astype(BF16)

        @pl.when(is_prompt)
        def _():
            dst[:, cs] = y.astype(BF16)
            if dst32 is not None:
                dst32[:, cs] = y

    def store_v(y, cs):
        v_ref[:, cs] = y.astype(BF16)

        @pl.when(is_prompt)
        def _():
            v32_ref[:, cs] = y

    def proj(c0, width):
        return _dot(h_scr[...], w_ref[:, c0:c0 + width])

    for c0 in range(0, n_q, COL_CHUNK):
        store_qk(head_norm(proj(c0, COL_CHUNK), qg_ref[...]), q_ref, None, slice(c0, c0 + COL_CHUNK))
    if n_k + n_v == COL_CHUNK:
        y = proj(n_q, COL_CHUNK)
        store_qk(head_norm(y[:, :n_k], kg_ref[:, :n_k]), k_ref, k32_ref, slice(0, n_k))
        store_v(y[:, n_k:], slice(0, n_v))
    else:
        for c0 in range(0, n_k, COL_CHUNK):
            cs = slice(c0, c0 + COL_CHUNK)
            store_qk(head_norm(proj(n_q + c0, COL_CHUNK), kg_ref[...]), k_ref, k32_ref, cs)
        for c0 in range(0, n_v, COL_CHUNK):
            store_v(proj(n_q + n_k + c0, COL_CHUNK), slice(c0, c0 + COL_CHUNK))


def _qkv_proj(geo, x, mod, gain, w, gmat, q_gain, k_gain, cos, sin, *, n_q, n_k, n_v):
    tm = geo.tm
    x_specs, x_args = geo.stream_specs(x)
    rope_spec = pl.BlockSpec((tm, COL_CHUNK), lambda i: (geo.rope_tile(i), 0))
    kern = functools.partial(_qkv_kernel, tm=tm, n_p=geo.n_p, n_x=len(x_args), n_q=n_q, n_k=n_k, n_v=n_v)
    return pl.pallas_call(
        kern,
        out_shape=(
            jax.ShapeDtypeStruct((geo.n_t, n_q), BF16),
            jax.ShapeDtypeStruct((geo.n_t, n_k), BF16),
            jax.ShapeDtypeStruct((geo.n_t, n_v), BF16),
            jax.ShapeDtypeStruct((geo.n_p, n_k), F32),
            jax.ShapeDtypeStruct((geo.n_p, n_v), F32),
        ),
        grid=(geo.n_t // tm,),
        in_specs=x_specs + [
            geo.mod_spec(),
            _vec_spec(D_MODEL),
            _const_spec((D_MODEL, n_q + n_k + n_v)),
            _const_spec((NORM_CHUNK, NORM_CHUNK)),
            _vec_spec(COL_CHUNK),
            _vec_spec(COL_CHUNK),
            rope_spec,
            rope_spec,
        ],
        out_specs=(geo.row_spec(n_q), geo.row_spec(n_k), geo.row_spec(n_v),
                   geo.prompt_spec(n_k), geo.prompt_spec(n_v)),
        scratch_shapes=[pltpu.VMEM((tm, D_MODEL), BF16)],
        compiler_params=_params(1),
        name="qkv_proj",
    )(*x_args, mod, gain, w, gmat, q_gain, k_gain, cos, sin)


def _flash_kernel(*refs, R, tq, kb, n_own, t_len, n_inner, has_ctx, use_sink, window, diff, lam_scale, n_alias):
    refs = list(refs)
    q_ref, k_ref, v_ref = refs[:3]
    del refs[:3]
    if has_ctx:
        ck_ref, cv_ref = refs[:2]
        del refs[:2]
    if use_sink:
        sink_ref = refs.pop(0)
    if diff:
        lam_ref, subg_ref = refs[:2]
        del refs[:2]
    del refs[:n_alias]
    o_ref, qs_scr = refs[:2]
    del refs[:2]
    cached = n_inner == 1
    if cached:
        vt_scr = refs.pop(0)
    if has_ctx:
        ctk_scr, ctvt_scr = refs

    W = HEAD_DIM * R
    M = R * tq
    DV = 2 * HEAD_DIM if diff else HEAD_DIM
    qi = pl.program_id(2)

    def transposed(v):
        return v.astype(F32).T.astype(BF16)

    if cached:
        first = qi == 0 if diff else jnp.logical_and(qi == 0, pl.program_id(1) == 0)

        @pl.when(first)
        def _():
            vt_scr[...] = transposed(v_ref[...])
            if has_ctx:
                ctk_scr[...] = ck_ref[...].astype(BF16)
                ctvt_scr[...] = cv_ref[...].T.astype(BF16)
    else:
        vt_all = transposed(v_ref[...])

    def block(kblk, vt, mask, state, sink_row):
        st = _dot_nt(kblk, qs_scr[...])
        if mask is not None:
            st = jnp.where(mask, st, NEG_INF)
        m_new = jnp.max(st, axis=0, keepdims=True)
        if state is not None:
            m_prev, l_prev, acc_prev = state
            m_new = jnp.maximum(m_new, m_prev)
        elif sink_row is not None:
            m_new = jnp.maximum(m_new, sink_row)
        p = jnp.exp2(st - m_new)
        l_new = jnp.sum(p, axis=0, keepdims=True)
        acc = _dot(vt, p.astype(BF16))
        if state is not None:
            alpha = jnp.exp2(m_prev - m_new)
            l_new = alpha * l_prev + l_new
            acc = alpha * acc_prev + acc
        elif sink_row is not None:
            l_new = l_new + jnp.exp2(sink_row - m_new)
        return m_new, l_new, acc

    def head(c):
        static = not cached
        q_cols = slice(c * W, (c + 1) * W) if static else slice(None)
        k_cols = slice(c * W, (c + 1) * W) if (static and diff) else slice(None)
        q = q_ref[:, q_cols].astype(F32)
        lane = lax.broadcasted_iota(jnp.int32, (tq, W), 1) // HEAD_DIM
        for r in range(R):
            t = jnp.where(lane == r, q, 0.0)
            if not diff:
                if static:
                    if (c - r) % R:
                        t = pltpu.roll(t, HEAD_DIM * ((c - r) % R), 1)
                else:
                    t = pltpu.roll(t, HEAD_DIM * ((c - r + R) % R), 1)
            qs_scr[r * tq:(r + 1) * tq, :] = t.astype(BF16)

        def vt_block(off):
            if static:
                return vt_all[c * DV:(c + 1) * DV, off:off + kb]
            rows = slice(None) if diff else pl.ds(pl.multiple_of(c * DV, DV), DV)
            return vt_scr[rows, pl.ds(off, kb)]

        sink_row = None
        if use_sink:
            sink_row = jnp.concatenate(
                [jnp.full((1, tq), sink_ref[c * R + r] * LOG2E, F32) for r in range(R)], axis=1)

        state = None
        if has_ctx:
            rows = slice(None) if diff else pl.ds(pl.multiple_of(c * DV, DV), DV)
            state = block(ctk_scr[...], ctvt_scr[rows, :], None, state, sink_row)
        if window:
            q0 = qi * tq
            start = pl.multiple_of(jnp.clip(q0 - WINDOW, 0, t_len - kb), WINDOW)
            kpos = start + lax.broadcasted_iota(jnp.int32, (kb, M), 0)
            qpos = q0 + lax.broadcasted_iota(jnp.int32, (kb, M), 1) % tq
            mask = jnp.abs(qpos - kpos) <= WINDOW
            state = block(k_ref[pl.ds(start, kb), k_cols], vt_block(start), mask, state, sink_row)
        else:
            for j in range(n_own):
                state = block(k_ref[j * kb:(j + 1) * kb, k_cols], vt_block(j * kb), None, state,
                              sink_row if state is None else None)

        _, l_fin, acc = state
        a = acc * (1.0 / l_fin)
        if diff:
            ot = a[:, :tq] - lam_ref[0] * a[:, tq:]
            ms = jnp.mean(ot * ot, axis=0, keepdims=True)
            o = (ot * lax.rsqrt(ms + EPS)).T * (subg_ref[...] * lam_scale)
        else:
            o = jnp.concatenate([a[:, r * tq:(r + 1) * tq] for r in range(R)], axis=0).T
        o_ref[:, q_cols] = o.astype(BF16)

    if cached:
        head(pl.program_id(1))
    else:
        for c in range(n_inner):
            head(c)


def _flash(q, k, v, *, row0, n_b, t_len, tq, R, inner, ctx=None, sink=None, window=False,
           lam=None, subg=None, lam_scale=1.0, into=None, name):
    diff = lam is not None
    W = HEAD_DIM * R
    w_kv = k.shape[1]
    n_c = D_MODEL // W
    n_qb = t_len // tq
    if window:
        kb, n_own = tq + 2 * WINDOW, 1
    else:
        kb = min(t_len, 512)
        n_own = t_len // kb
    q_base, k_base = row0 // tq, row0 // t_len
    M = R * tq
    scratch = [pltpu.VMEM((M, W), BF16)]
    if inner:
        assert n_qb == 1 and n_own == 1 and ctx is None
        grid = (n_b, 1, 1)
        q_spec = pl.BlockSpec((tq, D_MODEL), lambda b, c, qi: (q_base + b, 0))
        kv_spec = pl.BlockSpec((t_len, w_kv), lambda b, c, qi: (k_base + b, 0))
        o_spec = q_spec
        n_inner = n_c
    else:
        grid = (n_b, n_c, n_qb)
        q_spec = pl.BlockSpec((tq, W), lambda b, c, qi: (q_base + b * n_qb + qi, c))
        o_spec = q_spec
        if diff:
            kv_spec = pl.BlockSpec((t_len, W), lambda b, c, qi: (k_base + b, c))
            scratch.append(pltpu.VMEM((W, t_len), BF16))
        else:
            kv_spec = pl.BlockSpec((t_len, w_kv), lambda b, c, qi: (k_base + b, 0))
            scratch.append(pltpu.VMEM((w_kv, t_len), BF16))
        n_inner = 1
    in_specs = [q_spec, kv_spec, kv_spec]
    args = [q, k, v]
    if ctx is not None:
        ck, cv = ctx
        p_len = ck.shape[1]
        w_ctx = W if diff else w_kv
        cmap = (lambda b, c, qi: (b, 0, c)) if diff else (lambda b, c, qi: (b, 0, 0))
        scratch += [pltpu.VMEM((p_len, w_ctx), BF16), pltpu.VMEM((w_ctx, p_len), BF16)]
        in_specs += [pl.BlockSpec((None, p_len, w_ctx), cmap)] * 2
        args += [ck, cv]
    if sink is not None:
        in_specs.append(pl.BlockSpec(memory_space=pltpu.SMEM))
        args.append(sink)
    if diff:
        in_specs += [pl.BlockSpec(memory_space=pltpu.SMEM), pl.BlockSpec((1, W), lambda b, c, qi: (0, 0))]
        args += [lam, subg]
    aliases = {}
    if into is not None:
        aliases = {len(args): 0}
        in_specs.append(pl.BlockSpec(memory_space=pl.ANY))
        args.append(into)
    kern = functools.partial(
        _flash_kernel, R=R, tq=tq, kb=kb, n_own=n_own, t_len=t_len, n_inner=n_inner, has_ctx=ctx is not None,
        use_sink=sink is not None, window=window, diff=diff, lam_scale=lam_scale, n_alias=len(aliases))
    return pl.pallas_call(
        kern,
        out_shape=jax.ShapeDtypeStruct(q.shape, BF16),
        grid=grid,
        in_specs=in_specs,
        out_specs=o_spec,
        scratch_shapes=scratch,
        input_output_aliases=aliases,
        compiler_params=_params(3),
        name=name,
    )(*args)


def _proj_d_kernel(x_ref, mod_ref, gain_ref, w_ref, lb_ref, qs_ref, lff_ref, lfb_ref, v_ref, gs_ref, h_scr):
    mod = mod_ref[...]
    h = _norm_mod(x_ref[...], gain_ref[...], mod[:, 0:D_MODEL], mod[:, D_MODEL:2 * D_MODEL])
    h_scr[...] = h.astype(BF16)

    def proj(part, c0):
        return _dot(h_scr[...], w_ref[:, part * D_MODEL + c0:part * D_MODEL + c0 + COL_CHUNK])

    for c0 in range(0, D_MODEL, COL_CHUNK):
        cs = slice(c0, c0 + COL_CHUNK)
        qs_ref[:, cs] = _silu(proj(0, c0))
        for d, ref in ((0, lff_ref), (1, lfb_ref)):
            lb = lb_ref[d:d + 1, cs]
            ref[:, cs] = jnp.log(lb + (1.0 - lb) * jax.nn.sigmoid(proj(1 + d, c0)))
        v_ref[:, cs] = proj(3, c0).astype(BF16)
        gs_ref[:, cs] = _silu(proj(4, c0))


def _proj_d(geo, x, mod, gain, w, lb):
    spec = geo.row_spec()
    f32_out = jax.ShapeDtypeStruct((geo.n_t, D_MODEL), F32)
    return pl.pallas_call(
        _proj_d_kernel,
        out_shape=(f32_out, f32_out, f32_out, jax.ShapeDtypeStruct((geo.n_t, D_MODEL), BF16), f32_out),
        grid=(geo.n_t // geo.tm,),
        in_specs=[spec, geo.mod_spec(), _vec_spec(D_MODEL),
                  _const_spec((D_MODEL, 5 * D_MODEL)), pl.BlockSpec((2, D_MODEL), lambda i: (0, 0))],
        out_specs=(spec,) * 5,
        scratch_shapes=[pltpu.VMEM((geo.tm, D_MODEL), BF16)],
        compiler_params=_params(1),
        name="hgrn_proj",
    )(x, mod, gain, w, lb)


def _hgrn_kernel(*refs, n_steps, cps, has_init, want_final, n_alias):
    refs = list(refs)
    tri_ref = refs.pop(0)
    ins = [refs[:3], refs[3:6]]
    del refs[:6]
    if has_init:
        s0_ref = refs.pop(0)
    del refs[:n_alias]
    o_refs = refs[:2]
    del refs[:2]
    if want_final:
        fin_ref = refs.pop(0)
    st_scr = refs.pop(0)
    n = pl.program_id(1)

    @pl.when(n == 0)
    def _():
        st_scr[...] = s0_ref[...] if has_init else jnp.zeros_like(st_scr)

    rows = cps * CHUNK
    row = lax.broadcasted_iota(jnp.int32, (rows, rows), 0)
    col = lax.broadcasted_iota(jnp.int32, (rows, rows), 1)
    same_chunk = row // CHUNK == col // CHUNK
    for d in range(2):
        qs_ref, lf_ref, v_ref = ins[d]
        tri = tri_ref[d]
        lf = lf_ref[...]
        cum = sum(_dot(tri, part) for part in _split3(lf))
        q_dec = (qs_ref[...] * jnp.exp(cum)).astype(BF16)
        k_in = (1.0 - jnp.exp(lf)) * jnp.exp(-cum)
        order = list(range(cps)) if d == 0 else list(reversed(range(cps)))
        chunk_rows = [slice(i * CHUNK, (i + 1) * CHUNK) for i in range(cps)]
        e_last = [jnp.exp(cum[i * CHUNK + CHUNK - 1:(i + 1) * CHUNK] if d == 0 else cum[i * CHUNK:i * CHUNK + 1])
                  for i in range(cps)]
        k_out = jnp.concatenate([k_in[chunk_rows[i]] * e_last[i] for i in range(cps)], axis=0).astype(BF16)
        k_in = k_in.astype(BF16)
        keep = jnp.logical_and(same_chunk, (col <= row) if d == 0 else (col >= row))
        for hd in range(D_HEADS):
            hs = slice(hd * D_KDIM, (hd + 1) * D_KDIM)
            att = jnp.where(keep, _dot_nt(q_dec[:, hs], k_in[:, hs]), 0.0).astype(BF16)
            v = v_ref[:, hs]
            o_intra = _dot(att, v)
            st = st_scr[d, hd]
            o_parts = [None] * cps
            for i in order:
                rs = chunk_rows[i]
                o_parts[i] = o_intra[rs] + _dot_nt(q_dec[rs, hs], st.astype(BF16))
                st = st * e_last[i][:, hs] + _dot_tn(v[rs], k_out[rs, hs])
            st_scr[d, hd] = st
            o_refs[d][:, hs] = jnp.concatenate(o_parts, axis=0)

    if want_final:
        @pl.when(n == n_steps - 1)
        def _():
            for d in range(2):
                for hd in range(D_HEADS):
                    fin_ref[d, hd] = st_scr[d, hd].T


def _hgrn_scan(tri, qs, lff, lfb, v, *, row0, n_b, t_len, s0t=None, want_final=False, into=None, name):
    cps = tri.shape[1] // CHUNK
    rows = cps * CHUNK
    n_steps = t_len // rows
    base = row0 // rows
    fwd = lambda b, n: (base + b * n_steps + n, 0)
    bwd = lambda b, n: (base + b * n_steps + n_steps - 1 - n, 0)
    blk = lambda m: pl.BlockSpec((rows, D_MODEL), m)
    st_shape = (2, D_HEADS, D_KDIM, D_KDIM)
    in_specs = [pl.BlockSpec((2, rows, rows), lambda b, n: (0, 0, 0)),
                blk(fwd), blk(fwd), blk(fwd), blk(bwd), blk(bwd), blk(bwd)]
    args = [tri, qs, lff, v, qs, lfb, v]
    if s0t is not None:
        in_specs.append(pl.BlockSpec((None,) + st_shape, lambda b, n: (b, 0, 0, 0, 0)))
        args.append(s0t)
    aliases = {}
    if into is not None:
        aliases = {len(args) + i: i for i in range(len(into))}
        in_specs += [pl.BlockSpec(memory_space=pl.ANY)] * len(into)
        args += list(into)
    o_shape = jax.ShapeDtypeStruct(qs.shape, F32)
    out_shape = [o_shape, o_shape]
    out_specs = [blk(fwd), blk(bwd)]
    if want_final:
        out_shape.append(jax.ShapeDtypeStruct((n_b,) + st_shape, F32))
        out_specs.append(pl.BlockSpec((None,) + st_shape, lambda b, n: (b, 0, 0, 0, 0)))
    kern = functools.partial(_hgrn_kernel, n_steps=n_steps, cps=cps, has_init=s0t is not None,
                             want_final=want_final, n_alias=len(aliases))
    return pl.pallas_call(
        kern,
        out_shape=tuple(out_shape),
        grid=(n_b, n_steps),
        in_specs=in_specs,
        out_specs=tuple(out_specs),
        scratch_shapes=[pltpu.VMEM(st_shape, F32)],
        input_output_aliases=aliases,
        compiler_params=_params(2),
        name=name,
    )(*args)


def _ffn_chunks(d_ff):
    chunks, f0 = [], 0
    while f0 < d_ff:
        fs = min(COL_CHUNK, d_ff - f0)
        chunks.append((f0, fs))
        f0 += fs
    return tuple(chunks)


def _out_ffn_kernel(*refs, tm, n_p, n_x, hgrn, split_out, f_chunks):
    refs = list(refs)
    x_refs = refs[:n_x]
    del refs[:n_x]
    n_o = 2 if hgrn else 1
    o_in = refs[:n_o]
    del refs[:n_o]
    if hgrn:
        gs_ref, gn_ref = refs[:2]
        del refs[:2]
    mod_ref, wo_ref, ng_ref, wg_ref, wu_ref, wd_ref = refs[:6]
    del refs[:6]
    n_out = 2 if split_out else 1
    out_refs = refs[:n_out]
    x1_scr, h_scr, acc_scr = refs[n_out:]
    is_sample = pl.program_id(0) * tm >= n_p
    mod = mod_ref[...]
    m = [mod[:, j * D_MODEL:(j + 1) * D_MODEL] for j in range(N_MOD)]

    if hgrn:
        o = o_in[0][...] + o_in[1][...]
        parts = []
        for hd in range(D_HEADS):
            hs = slice(hd * D_KDIM, (hd + 1) * D_KDIM)
            oh = o[:, hs]
            ms = jnp.mean(oh * oh, axis=-1, keepdims=True)
            parts.append(oh * lax.rsqrt(ms + EPS) * gn_ref[...] * gs_ref[:, hs])
        o = jnp.concatenate(parts, axis=-1).astype(BF16)
    else:
        o = o_in[0][...]

    x1 = _load_stream(x_refs, is_sample) + m[2] * _dot(o, wo_ref[...])
    x1_scr[...] = x1
    h_scr[...] = _norm_mod(x1, ng_ref[...], m[3], m[4]).astype(BF16)
    for idx, (f0, fs) in enumerate(f_chunks):
        hb = h_scr[...]
        g = _dot(hb, wg_ref[:, f0:f0 + fs])
        u = _dot(hb, wu_ref[:, f0:f0 + fs])
        y = _dot((_silu(g) * u).astype(BF16), wd_ref[f0:f0 + fs, :])
        acc_scr[...] = y if idx == 0 else acc_scr[...] + y
    res = x1_scr[...] + m[5] * acc_scr[...]
    if split_out:
        @pl.when(jnp.logical_not(is_sample))
        def _():
            out_refs[0][...] = res

        @pl.when(is_sample)
        def _():
            out_refs[1][...] = res
    else:
        out_refs[0][...] = res


def _out_ffn(geo, x, o_parts, mod, w_o, norm_gain, w_gate, w_up, w_down, gs=None, gn=None, split_out=False):
    tm = geo.tm
    hgrn = gs is not None
    d_ff = w_gate.shape[1]
    in_specs, args = geo.stream_specs(x)
    n_x = len(args)
    in_specs += [geo.row_spec()] * len(o_parts)
    args += list(o_parts)
    if hgrn:
        in_specs += [geo.row_spec(), _vec_spec(D_KDIM)]
        args += [gs, gn]
    in_specs += [geo.mod_spec(), _const_spec((D_MODEL, D_MODEL)), _vec_spec(D_MODEL),
                 _const_spec((D_MODEL, d_ff)), _const_spec((D_MODEL, d_ff)), _const_spec((d_ff, D_MODEL))]
    args += [mod, w_o, norm_gain, w_gate, w_up, w_down]
    if split_out:
        out_shape = (jax.ShapeDtypeStruct((geo.n_p, D_MODEL), F32), jax.ShapeDtypeStruct((geo.n_s, D_MODEL), F32))
        out_specs = (geo.prompt_spec(), geo.sample_spec())
    else:
        out_shape = jax.ShapeDtypeStruct((geo.n_t, D_MODEL), F32)
        out_specs = geo.row_spec()
    kern = functools.partial(_out_ffn_kernel, tm=tm, n_p=geo.n_p, n_x=n_x, hgrn=hgrn, split_out=split_out,
                             f_chunks=_ffn_chunks(d_ff))
    return pl.pallas_call(
        kern,
        out_shape=out_shape,
        grid=(geo.n_t // tm,),
        in_specs=in_specs,
        out_specs=out_specs,
        scratch_shapes=[pltpu.VMEM((tm, D_MODEL), F32), pltpu.VMEM((tm, D_MODEL), BF16),
                        pltpu.VMEM((tm, D_MODEL), F32)],
        compiler_params=_params(1),
        name="out_ffn",
    )(*args)


def _rope_tables(n_tokens):
    rows = n_tokens // GRID_W
    row = jnp.repeat(jnp.arange(rows, dtype=F32), GRID_W)
    col = jnp.tile(jnp.arange(GRID_W, dtype=F32), rows)
    axis_dim = HEAD_DIM // 2
    inv_freq = ROPE_THETA ** (-jnp.arange(0, axis_dim, 2, dtype=F32) / axis_dim)
    ang = jnp.stack([row[:, None] * inv_freq, col[:, None] * inv_freq], axis=1)
    cos, sin = jnp.cos(ang), jnp.sin(ang)
    cos64 = jnp.concatenate([cos, cos], axis=-1).reshape(n_tokens, HEAD_DIM)
    sin64 = jnp.concatenate([-sin, sin], axis=-1).reshape(n_tokens, HEAD_DIM)
    reps = COL_CHUNK // HEAD_DIM
    return jnp.tile(cos64, (1, reps)), jnp.tile(sin64, (1, reps))


def _group_matrix():
    g = jnp.arange(NORM_CHUNK) // HEAD_DIM
    return (g[:, None] == g[None, :]).astype(BF16)


def _tri_matrices(cps):
    r = jnp.arange(cps * CHUNK)
    same = r[None, :] // CHUNK == r[:, None] // CHUNK
    lower = jnp.logical_and(same, r[None, :] <= r[:, None]).astype(BF16)
    return jnp.stack([lower, lower.T])


def kernel(x_prompt, x_sample, c, cache_a_k, cache_a_v, cache_b_k, cache_b_v, cache_c_k, cache_c_v, state_d, c_ctx, norm_mix, norm_ffn, w_ada, b_ada, w_ffn_gate, w_ffn_up, w_ffn_down, w_qkv_a, w_o_a, qn_a, kn_a, subln_a, lam_q1_a, lam_k1_a, lam_q2_a, lam_k2_a, w_qkv_b, w_o_b, qn_b, kn_b, sink_b, w_qkv_c, w_o_c, qn_c, kn_c, w_in_d, w_o_d, gn_d, lb_logits_d):
    bp, tp, _ = x_prompt.shape
    bs, ts, _ = x_sample.shape
    past = cache_a_k.shape[2]
    depth = w_ada.shape[0]
    geo = _Geom(bp, tp, bs, ts, past)
    n_p = geo.n_p

    cond = jnp.concatenate([c_ctx[None], c, jnp.zeros((MOD_ROWS - 1 - bs, D_MODEL), F32)], axis=0)
    mod_all = _ada_all(cond, w_ada, b_ada).reshape(depth, MOD_ROWS, 1, N_MOD * D_MODEL)

    cos, sin = _rope_tables(ts)
    gmat = _group_matrix()
    tri = _tri_matrices(math.gcd(HGRN_CHUNKS_PER_STEP, math.gcd(tp // CHUNK, ts // CHUNK)))
    p_lb = jax.nn.softmax(lb_logits_d.astype(F32), axis=1)
    lb_all = jnp.cumsum(p_lb, axis=1) - p_lb[:, :1]
    q_scale = HEAD_DIM ** -0.5 * LOG2E
    tile_gain = lambda g, s=1.0: jnp.tile(g * s, COL_CHUNK // HEAD_DIM)[None]
    bf = lambda w: w.astype(BF16)

    x = (x_prompt.reshape(n_p, D_MODEL), x_sample.reshape(geo.n_s, D_MODEL))
    outs = {k: [] for k in ("ak", "av", "bk", "bv", "ck", "cv", "d")}
    for li in range(depth):
        kind, j = li % 4, li // 4
        mod = mod_all[li]
        gain = norm_mix[li][None]
        ffn_w = (norm_ffn[li][None], bf(w_ffn_gate[li]), bf(w_ffn_up[li]), bf(w_ffn_down[li]))
        split_out = li == depth - 1
        if kind == 0:
            lam_init = 0.8 - 0.6 * math.exp(-0.3 * li)
            lam = (jnp.exp(jnp.sum(lam_q1_a[j] * lam_k1_a[j])) - jnp.exp(jnp.sum(lam_q2_a[j] * lam_k2_a[j]))
                   + lam_init).astype(F32).reshape(1)
            q, k, v, k32, v32 = _qkv_proj(geo, x, mod, gain, bf(w_qkv_a[j]), gmat, tile_gain(qn_a[j], q_scale),
                                          tile_gain(kn_a[j]), cos, sin, n_q=D_MODEL, n_k=D_MODEL, n_v=D_MODEL)
            common = dict(R=2, lam=lam, subg=subln_a[j][None], lam_scale=1.0 - lam_init)
            o_p = _flash(q, k, v, row0=0, n_b=bp, t_len=tp, tq=tp, inner=True, name="attn_a_prompt", **common)
            ctx = (cache_a_k[:, j].reshape(bs, past, D_MODEL), cache_a_v[:, j].reshape(bs, past, D_MODEL))
            o = _flash(q, k, v, row0=n_p, n_b=bs, t_len=ts, tq=min(512, ts), inner=False, ctx=ctx, into=o_p,
                       name="attn_a_sample", **common)
            outs["ak"].append(k32.reshape(bp, tp, 2 * D_MODEL // (2 * HEAD_DIM), HEAD_DIM))
            outs["av"].append(v32.reshape(bp, tp, D_MODEL // (2 * HEAD_DIM), 2 * HEAD_DIM))
            x = _out_ffn(geo, x, (o,), mod, bf(w_o_a[j]), *ffn_w, split_out=split_out)
        elif kind in (1, 2):
            w_qkv, w_o, qn, kn = (w_qkv_b, w_o_b, qn_b, kn_b) if kind == 1 else (w_qkv_c, w_o_c, qn_c, kn_c)
            cache_k, cache_v = (cache_b_k, cache_b_v) if kind == 1 else (cache_c_k, cache_c_v)
            n_kv = (w_qkv.shape[-1] - D_MODEL) // 2
            q, k, v, k32, v32 = _qkv_proj(geo, x, mod, gain, bf(w_qkv[j]), gmat, tile_gain(qn[j], q_scale),
                                          tile_gain(kn[j]), cos, sin, n_q=D_MODEL, n_k=n_kv, n_v=n_kv)
            sink = sink_b[j].astype(F32) if kind == 1 else None
            ctx = (cache_k[:, j].reshape(bs, past, n_kv), cache_v[:, j].reshape(bs, past, n_kv))
            tag = "b" if kind == 1 else "c"
            o_p = _flash(q, k, v, row0=0, n_b=bp, t_len=tp, tq=tp, R=4, inner=True, sink=sink,
                         name=f"attn_{tag}_prompt")
            o = _flash(q, k, v, row0=n_p, n_b=bs, t_len=ts, tq=min(256, ts), R=4, inner=False, ctx=ctx,
                       sink=sink, window=kind == 1, into=o_p, name=f"attn_{tag}_sample")
            outs[tag + "k"].append(k32.reshape(bp, tp, n_kv // HEAD_DIM, HEAD_DIM))
            outs[tag + "v"].append(v32.reshape(bp, tp, n_kv // HEAD_DIM, HEAD_DIM))
            x = _out_ffn(geo, x, (o,), mod, bf(w_o[j]), *ffn_w, split_out=split_out)
        else:
            if isinstance(x, tuple):
                x = jnp.concatenate(x, axis=0)
            qs, lff, lfb, v, gs = _proj_d(geo, x, mod, gain, bf(w_in_d[j]), lb_all[:, li])
            of_p, ob_p, fin = _hgrn_scan(tri, qs, lff, lfb, v, row0=0, n_b=bp, t_len=tp, want_final=True,
                                         name="hgrn_prompt")
            s0t = jnp.swapaxes(state_d[:, j], -1, -2)
            o_f, o_b = _hgrn_scan(tri, qs, lff, lfb, v, row0=n_p, n_b=bs, t_len=ts, s0t=s0t, into=(of_p, ob_p),
                                  name="hgrn_sample")
            outs["d"].append(fin)
            x = _out_ffn(geo, x, (o_f, o_b), mod, bf(w_o_d[j]), *ffn_w, gs=gs, gn=gn_d[j][None],
                         split_out=split_out)

    if not isinstance(x, tuple):
        x = (x[:n_p], x[n_p:])
    stack = lambda xs: jnp.stack(xs, axis=1)
    return (x[0].reshape(bp, tp, D_MODEL), x[1].reshape(bs, ts, D_MODEL), stack(outs["ak"]), stack(outs["av"]),
            stack(outs["bk"]), stack(outs["bv"]), stack(outs["ck"]), stack(outs["cv"]), stack(outs["d"]))
```

```python
import functools
import math

import jax
import jax.numpy as jnp
from jax import lax
from jax.experimental import pallas as pl
from jax.experimental.pallas import tpu as pltpu

F32 = jnp.float32
BF16 = jnp.bfloat16

D_MODEL = 1024
HEAD_DIM = 64
GRID_W = 64
ROPE_THETA = 10000.0
WINDOW = 128
EPS = 1e-6
NEG_INF = -1e30
LOG2E = 1.4426950408889634
CHUNK = 64
HGRN_CHUNKS_PER_STEP = 4
D_KDIM = 128
D_HEADS = D_MODEL // D_KDIM
N_MOD = 6
MOD_ROWS = 8
COL_CHUNK = 512
NORM_CHUNK = 256
V7X_VMEM_BYTES = 64 * 1024 * 1024
VMEM_LIMIT = V7X_VMEM_BYTES - 8 * 1024 * 1024


def _dot(a, b):
    return jnp.dot(a, b, preferred_element_type=F32)


def _dot_nt(a, b):
    return lax.dot_general(a, b, (((1,), (1,)), ((), ())), preferred_element_type=F32)


def _dot_tn(a, b):
    return lax.dot_general(a, b, (((0,), (0,)), ((), ())), preferred_element_type=F32)


def _silu(x):
    return x * jax.nn.sigmoid(x)


def _params(n_axes, **kw):
    return pltpu.CompilerParams(dimension_semantics=("arbitrary",) * n_axes, vmem_limit_bytes=VMEM_LIMIT, **kw)


def _const_spec(shape):
    return pl.BlockSpec(shape, lambda *_: (0,) * len(shape), pipeline_mode=pl.Buffered(1))


def _norm_mod(x, gain, shift, scale):
    ms = jnp.mean(x * x, axis=-1, keepdims=True)
    return (x * lax.rsqrt(ms + EPS) * gain) * (1.0 + scale) + shift


def _split3(x):
    hi = x.astype(BF16)
    r = x - hi.astype(F32)
    mid = r.astype(BF16)
    lo = (r - mid.astype(F32)).astype(BF16)
    return hi, mid, lo


def _ada_kernel(cond_ref, w_ref, b_ref, o_ref):
    a = _silu(cond_ref[...]).astype(BF16)
    o_ref[...] = _dot(a, w_ref[...].astype(BF16)) + b_ref[...]


def _ada_all(cond, w_ada, b_ada):
    depth, d, n = w_ada.shape
    tn = n // 4
    return pl.pallas_call(
        _ada_kernel,
        out_shape=jax.ShapeDtypeStruct((depth, MOD_ROWS, n), F32),
        grid=(depth, n // tn),
        in_specs=[
            pl.BlockSpec((MOD_ROWS, d), lambda l, j: (0, 0)),
            pl.BlockSpec((None, d, tn), lambda l, j: (l, 0, j)),
            pl.BlockSpec((None, 1, tn), lambda l, j: (l, 0, j)),
        ],
        out_specs=pl.BlockSpec((None, MOD_ROWS, tn), lambda l, j: (l, 0, j)),
        compiler_params=_params(2),
        name="ada_mod",
    )(cond, w_ada, b_ada.reshape(depth, 1, n))


class _Geom:
    def __init__(self, bp, tp, bs, ts, past):
        self.bp, self.tp, self.bs, self.ts, self.past = bp, tp, bs, ts, past
        self.n_p = bp * tp
        self.n_s = bs * ts
        self.n_t = self.n_p + self.n_s
        self.tm = math.gcd(512, math.gcd(self.n_p, ts))
        assert self.n_p % ts == 0 and ts % CHUNK == 0 and tp % CHUNK == 0
        assert bs + 1 <= MOD_ROWS

    def mod_row(self, i):
        return jnp.where(i * self.tm < self.n_p, 0, 1 + (i * self.tm - self.n_p) // self.ts)

    def prompt_tile(self, i):
        return jnp.minimum(i, self.n_p // self.tm - 1)

    def sample_tile(self, i):
        return jnp.maximum(i - self.n_p // self.tm, 0)

    def rope_tile(self, i):
        per_seq = self.ts // self.tm
        return jnp.where(i * self.tm < self.n_p, per_seq, self.sample_tile(i) % per_seq)

    def row_spec(self, width=D_MODEL):
        return pl.BlockSpec((self.tm, width), lambda i: (i, 0))

    def prompt_spec(self, width=D_MODEL):
        return pl.BlockSpec((self.tm, width), lambda i: (self.prompt_tile(i), 0))

    def sample_spec(self, width=D_MODEL):
        return pl.BlockSpec((self.tm, width), lambda i: (self.sample_tile(i), 0))

    def mod_spec(self):
        return pl.BlockSpec((None, 1, N_MOD * D_MODEL), lambda i: (self.mod_row(i), 0, 0))

    def stream_specs(self, x):
        if isinstance(x, tuple):
            return [self.prompt_spec(), self.sample_spec()], list(x)
        return [self.row_spec()], [x]


def _vec_spec(n):
    return pl.BlockSpec((1, n), lambda i: (0, 0))


def _load_stream(x_refs, is_sample):
    if len(x_refs) == 2:
        return jnp.where(is_sample, x_refs[1][...], x_refs[0][...])
    return x_refs[0][...]


def _qkv_kernel(*refs, tm, n_p, n_x, n_q, n_k, n_v):
    refs = list(refs)
    x_refs = refs[:n_x]
    del refs[:n_x]
    (mod_ref, gain_ref, w_ref, gmat_ref, qg_ref, kg_ref, cos_ref, sin_ref,
     q_ref, k_ref, v_ref, k32_ref, v32_ref, h_scr, k32_scr, v32_scr) = refs
    i = pl.program_id(0)
    is_sample = i * tm >= n_p
    is_prompt = jnp.logical_not(is_sample)
    mod = mod_ref[...]
    h = _norm_mod(_load_stream(x_refs, is_sample), gain_ref[...], mod[:, 0:D_MODEL], mod[:, D_MODEL:2 * D_MODEL])
    h_scr[...] = h.astype(BF16)

    def head_norm(y, gain):
        y2 = (y * y).astype(BF16)
        ms = jnp.concatenate([_dot(y2[:, c:c + NORM_CHUNK], gmat_ref[...])
                              for c in range(0, y.shape[1], NORM_CHUNK)], axis=1) * (1.0 / HEAD_DIM)
        return y * lax.rsqrt(ms + EPS) * gain

    def rope(y):
        w = y.shape[1]
        lane = lax.broadcasted_iota(jnp.int32, y.shape, 1)
        partner = jnp.where(lane % 32 < 16, pltpu.roll(y, w - 16, 1), pltpu.roll(y, 16, 1))
        return y * cos_ref[:, :w] + partner * sin_ref[:, :w]

    def store_qk(y, dst, stage, cs):
        dst[:, cs] = rope(y).astype(BF16)
        if stage is not None:
            stage[:, cs] = y

    def store_v(y, cs):
        v_ref[:, cs] = y.astype(BF16)
        v32_scr[:, cs] = y

    def proj(c0, width):
        return _dot(h_scr[...], w_ref[:, c0:c0 + width])

    for c0 in range(0, n_q, COL_CHUNK):
        store_qk(head_norm(proj(c0, COL_CHUNK), qg_ref[...]), q_ref, None, slice(c0, c0 + COL_CHUNK))
    if n_k + n_v == COL_CHUNK:
        y = proj(n_q, COL_CHUNK)
        store_qk(head_norm(y[:, :n_k], kg_ref[:, :n_k]), k_ref, k32_scr, slice(0, n_k))
        store_v(y[:, n_k:], slice(0, n_v))
    else:
        for c0 in range(0, n_k, COL_CHUNK):
            cs = slice(c0, c0 + COL_CHUNK)
            store_qk(head_norm(proj(n_q + c0, COL_CHUNK), kg_ref[...]), k_ref, k32_scr, cs)
        for c0 in range(0, n_v, COL_CHUNK):
            store_v(proj(n_q + n_k + c0, COL_CHUNK), slice(c0, c0 + COL_CHUNK))

    @pl.when(is_prompt)
    def _():
        k32_ref[...] = k32_scr[...]
        v32_ref[...] = v32_scr[...]


def _qkv_proj(geo, x, mod, gain, w, gmat, q_gain, k_gain, cos, sin, *, n_q, n_k, n_v):
    tm = geo.tm
    x_specs, x_args = geo.stream_specs(x)
    rope_spec = pl.BlockSpec((tm, COL_CHUNK), lambda i: (geo.rope_tile(i), 0))
    kern = functools.partial(_qkv_kernel, tm=tm, n_p=geo.n_p, n_x=len(x_args), n_q=n_q, n_k=n_k, n_v=n_v)
    return pl.pallas_call(
        kern,
        out_shape=(
            jax.ShapeDtypeStruct((geo.n_t, n_q), BF16),
            jax.ShapeDtypeStruct((geo.n_t, n_k), BF16),
            jax.ShapeDtypeStruct((geo.n_t, n_v), BF16),
            jax.ShapeDtypeStruct((geo.n_p, n_k), F32),
            jax.ShapeDtypeStruct((geo.n_p, n_v), F32),
        ),
        grid=(geo.n_t // tm,),
        in_specs=x_specs + [
            geo.mod_spec(),
            _vec_spec(D_MODEL),
            _const_spec((D_MODEL, n_q + n_k + n_v)),
            _const_spec((NORM_CHUNK, NORM_CHUNK)),
            _vec_spec(COL_CHUNK),
            _vec_spec(COL_CHUNK),
            rope_spec,
            rope_spec,
        ],
        out_specs=(geo.row_spec(n_q), geo.row_spec(n_k), geo.row_spec(n_v),
                   geo.prompt_spec(n_k), geo.prompt_spec(n_v)),
        scratch_shapes=[pltpu.VMEM((tm, D_MODEL), BF16), pltpu.VMEM((tm, n_k), F32), pltpu.VMEM((tm, n_v), F32)],
        compiler_params=_params(1),
        name="qkv_proj",
    )(*x_args, mod, gain, w, gmat, q_gain, k_gain, cos, sin)


def _flash_kernel(*refs, R, tq, kb, n_own, t_len, n_inner, has_ctx, use_sink, window, diff, lam_scale, n_alias):
    refs = list(refs)
    q_ref, k_ref, v_ref = refs[:3]
    del refs[:3]
    if has_ctx:
        ck_ref, cv_ref = refs[:2]
        del refs[:2]
    if use_sink:
        sink_ref = refs.pop(0)
    if diff:
        lam_ref, subg_ref = refs[:2]
        del refs[:2]
    del refs[:n_alias]
    o_ref, qs_scr = refs[:2]
    del refs[:2]
    cached = n_inner == 1
    if cached:
        vt_scr = refs.pop(0)
    if has_ctx:
        ctk_scr, ctvt_scr = refs

    W = HEAD_DIM * R
    M = R * tq
    DV = 2 * HEAD_DIM if diff else HEAD_DIM
    qi = pl.program_id(2)

    def transposed(v):
        return v.astype(F32).T.astype(BF16)

    if cached:
        first = qi == 0 if diff else jnp.logical_and(qi == 0, pl.program_id(1) == 0)

        @pl.when(first)
        def _():
            vt_scr[...] = transposed(v_ref[...])
            if has_ctx:
                ctk_scr[...] = ck_ref[...].astype(BF16)
                ctvt_scr[...] = cv_ref[...].T.astype(BF16)
    else:
        vt_all = transposed(v_ref[...])

    def scores(kblk, mask, qs):
        st = _dot_nt(kblk, qs_scr[...] if qs is None else qs)
        if mask is not None:
            st = jnp.where(mask, st, NEG_INF)
        return st

    def update(st, vt, state, sink_row):
        m_new = jnp.max(st, axis=0, keepdims=True)
        if state is not None:
            m_prev, l_prev, acc_prev = state
            m_new = jnp.maximum(m_new, m_prev)
        elif sink_row is not None:
            m_new = jnp.maximum(m_new, sink_row)
        p = jnp.exp2(st - m_new)
        l_new = jnp.sum(p, axis=0, keepdims=True)
        acc = _dot(vt, p.astype(BF16))
        if state is not None:
            alpha = jnp.exp2(m_prev - m_new)
            l_new = alpha * l_prev + l_new
            acc = alpha * acc_prev + acc
        elif sink_row is not None:
            l_new = l_new + jnp.exp2(sink_row - m_new)
        return m_new, l_new, acc

    def head(c):
        static = not cached
        q_cols = slice(c * W, (c + 1) * W) if static else slice(None)
        k_cols = slice(c * W, (c + 1) * W) if (static and diff) else slice(None)
        q = q_ref[:, q_cols].astype(F32)
        lane = lax.broadcasted_iota(jnp.int32, (tq, W), 1) // HEAD_DIM
        stacked = []
        for r in range(R):
            t = jnp.where(lane == r, q, 0.0)
            if not diff:
                if static:
                    if (c - r) % R:
                        t = pltpu.roll(t, HEAD_DIM * ((c - r) % R), 1)
                else:
                    t = pltpu.roll(t, HEAD_DIM * ((c - r + R) % R), 1)
            stacked.append(t.astype(BF16))
        qs = None
        if static:
            qs = jnp.concatenate(stacked, axis=0)
        else:
            for r in range(R):
                qs_scr[r * tq:(r + 1) * tq, :] = stacked[r]

        def vt_block(off):
            if static:
                return vt_all[c * DV:(c + 1) * DV, off:off + kb]
            rows = slice(None) if diff else pl.ds(pl.multiple_of(c * DV, DV), DV)
            return vt_scr[rows, pl.ds(off, kb)]

        sink_row = None
        if use_sink:
            sink_row = jnp.concatenate(
                [jnp.full((1, tq), sink_ref[c * R + r] * LOG2E, F32) for r in range(R)], axis=1)

        blocks = []
        if has_ctx:
            rows = slice(None) if diff else pl.ds(pl.multiple_of(c * DV, DV), DV)
            blocks.append((lambda: ctk_scr[...], lambda: ctvt_scr[rows, :], None))
        if window:
            q0 = qi * tq
            start = pl.multiple_of(jnp.clip(q0 - WINDOW, 0, t_len - kb), WINDOW)
            kpos = start + lax.broadcasted_iota(jnp.int32, (kb, M), 0)
            qpos = q0 + lax.broadcasted_iota(jnp.int32, (kb, M), 1) % tq
            blocks.append((lambda: k_ref[pl.ds(start, kb), k_cols], lambda: vt_block(start),
                           jnp.abs(qpos - kpos) <= WINDOW))
        else:
            for j in range(n_own):
                blocks.append((lambda j=j: k_ref[j * kb:(j + 1) * kb, k_cols], lambda j=j: vt_block(j * kb), None))

        state = None
        st_next = scores(blocks[0][0](), blocks[0][2], qs)
        for j, (_, vt_get, _) in enumerate(blocks):
            st = st_next
            if j + 1 < len(blocks):
                st_next = scores(blocks[j + 1][0](), blocks[j + 1][2], qs)
            state = update(st, vt_get(), state, sink_row if j == 0 else None)

        _, l_fin, acc = state
        a = acc * (1.0 / l_fin)
        if diff:
            ot = a[:, :tq] - lam_ref[0] * a[:, tq:]
            ms = jnp.mean(ot * ot, axis=0, keepdims=True)
            o = (ot * lax.rsqrt(ms + EPS)).T * (subg_ref[...] * lam_scale)
        else:
            o = jnp.concatenate([a[:, r * tq:(r + 1) * tq] for r in range(R)], axis=0).T
        o_ref[:, q_cols] = o.astype(BF16)

    if cached:
        head(pl.program_id(1))
    else:
        for c in range(n_inner):
            head(c)


def _flash(q, k, v, *, row0, n_b, t_len, tq, R, inner, ctx=None, sink=None, window=False,
           lam=None, subg=None, lam_scale=1.0, into=None, name):
    diff = lam is not None
    W = HEAD_DIM * R
    w_kv = k.shape[1]
    n_c = D_MODEL // W
    n_qb = t_len // tq
    if window:
        kb, n_own = tq + 2 * WINDOW, 1
    else:
        kb = min(t_len, 512)
        n_own = t_len // kb
    q_base, k_base = row0 // tq, row0 // t_len
    M = R * tq
    scratch = [pltpu.VMEM((M, W), BF16)]
    if inner:
        assert n_qb == 1 and n_own == 1 and ctx is None
        grid = (n_b, 1, 1)
        q_spec = pl.BlockSpec((tq, D_MODEL), lambda b, c, qi: (q_base + b, 0))
        kv_spec = pl.BlockSpec((t_len, w_kv), lambda b, c, qi: (k_base + b, 0))
        o_spec = q_spec
        n_inner = n_c
    else:
        grid = (n_b, n_c, n_qb)
        q_spec = pl.BlockSpec((tq, W), lambda b, c, qi: (q_base + b * n_qb + qi, c))
        o_spec = q_spec
        if diff:
            kv_spec = pl.BlockSpec((t_len, W), lambda b, c, qi: (k_base + b, c))
            scratch.append(pltpu.VMEM((W, t_len), BF16))
        else:
            kv_spec = pl.BlockSpec((t_len, w_kv), lambda b, c, qi: (k_base + b, 0))
            scratch.append(pltpu.VMEM((w_kv, t_len), BF16))
        n_inner = 1
    in_specs = [q_spec, kv_spec, kv_spec]
    args = [q, k, v]
    if ctx is not None:
        ck, cv = ctx
        p_len = ck.shape[1]
        w_ctx = W if diff else w_kv
        cmap = (lambda b, c, qi: (b, 0, c)) if diff else (lambda b, c, qi: (b, 0, 0))
        scratch += [pltpu.VMEM((p_len, w_ctx), BF16), pltpu.VMEM((w_ctx, p_len), BF16)]
        in_specs += [pl.BlockSpec((None, p_len, w_ctx), cmap)] * 2
        args += [ck, cv]
    if sink is not None:
        in_specs.append(pl.BlockSpec(memory_space=pltpu.SMEM))
        args.append(sink)
    if diff:
        in_specs += [pl.BlockSpec(memory_space=pltpu.SMEM), pl.BlockSpec((1, W), lambda b, c, qi: (0, 0))]
        args += [lam, subg]
    aliases = {}
    if into is not None:
        aliases = {len(args): 0}
        in_specs.append(pl.BlockSpec(memory_space=pl.ANY))
        args.append(into)
    kern = functools.partial(
        _flash_kernel, R=R, tq=tq, kb=kb, n_own=n_own, t_len=t_len, n_inner=n_inner, has_ctx=ctx is not None,
        use_sink=sink is not None, window=window, diff=diff, lam_scale=lam_scale, n_alias=len(aliases))
    return pl.pallas_call(
        kern,
        out_shape=jax.ShapeDtypeStruct(q.shape, BF16),
        grid=grid,
        in_specs=in_specs,
        out_specs=o_spec,
        scratch_shapes=scratch,
        input_output_aliases=aliases,
        compiler_params=_params(3),
        name=name,
    )(*args)


def _proj_d_kernel(x_ref, mod_ref, gain_ref, w_ref, lb_ref, qs_ref, lff_ref, lfb_ref, v_ref, gs_ref, h_scr):
    mod = mod_ref[...]
    h = _norm_mod(x_ref[...], gain_ref[...], mod[:, 0:D_MODEL], mod[:, D_MODEL:2 * D_MODEL])
    h_scr[...] = h.astype(BF16)

    def proj(part, c0):
        return _dot(h_scr[...], w_ref[:, part * D_MODEL + c0:part * D_MODEL + c0 + COL_CHUNK])

    for c0 in range(0, D_MODEL, COL_CHUNK):
        cs = slice(c0, c0 + COL_CHUNK)
        qs_ref[:, cs] = _silu(proj(0, c0))
        for d, ref in ((0, lff_ref), (1, lfb_ref)):
            lb = lb_ref[d:d + 1, cs]
            ref[:, cs] = jnp.log(lb + (1.0 - lb) * jax.nn.sigmoid(proj(1 + d, c0)))
        v_ref[:, cs] = proj(3, c0).astype(BF16)
        gs_ref[:, cs] = _silu(proj(4, c0))


def _proj_d(geo, x, mod, gain, w, lb):
    spec = geo.row_spec()
    f32_out = jax.ShapeDtypeStruct((geo.n_t, D_MODEL), F32)
    return pl.pallas_call(
        _proj_d_kernel,
        out_shape=(f32_out, f32_out, f32_out, jax.ShapeDtypeStruct((geo.n_t, D_MODEL), BF16), f32_out),
        grid=(geo.n_t // geo.tm,),
        in_specs=[spec, geo.mod_spec(), _vec_spec(D_MODEL),
                  _const_spec((D_MODEL, 5 * D_MODEL)), pl.BlockSpec((2, D_MODEL), lambda i: (0, 0))],
        out_specs=(spec,) * 5,
        scratch_shapes=[pltpu.VMEM((geo.tm, D_MODEL), BF16)],
        compiler_params=_params(1),
        name="hgrn_proj",
    )(x, mod, gain, w, lb)


def _hgrn_kernel(*refs, n_steps, cps, has_init, want_final, n_alias):
    refs = list(refs)
    tri_ref = refs.pop(0)
    ins = [refs[:3], refs[3:6]]
    del refs[:6]
    if has_init:
        s0_ref = refs.pop(0)
    del refs[:n_alias]
    o_refs = refs[:2]
    del refs[:2]
    if want_final:
        fin_ref = refs.pop(0)
    st_scr = refs.pop(0)
    n = pl.program_id(1)

    @pl.when(n == 0)
    def _():
        st_scr[...] = s0_ref[...] if has_init else jnp.zeros_like(st_scr)

    rows = cps * CHUNK
    row = lax.broadcasted_iota(jnp.int32, (rows, rows), 0)
    col = lax.broadcasted_iota(jnp.int32, (rows, rows), 1)
    same_chunk = row // CHUNK == col // CHUNK
    for d in range(2):
        qs_ref, lf_ref, v_ref = ins[d]
        tri = tri_ref[d]
        lf = lf_ref[...]
        cum = sum(_dot(tri, part) for part in _split3(lf))
        q_dec = (qs_ref[...] * jnp.exp(cum)).astype(BF16)
        k_in = (1.0 - jnp.exp(lf)) * jnp.exp(-cum)
        order = list(range(cps)) if d == 0 else list(reversed(range(cps)))
        chunk_rows = [slice(i * CHUNK, (i + 1) * CHUNK) for i in range(cps)]
        e_last = [jnp.exp(cum[i * CHUNK + CHUNK - 1:(i + 1) * CHUNK] if d == 0 else cum[i * CHUNK:i * CHUNK + 1])
                  for i in range(cps)]
        k_out = jnp.concatenate([k_in[chunk_rows[i]] * e_last[i] for i in range(cps)], axis=0).astype(BF16)
        k_in = k_in.astype(BF16)
        keep = jnp.logical_and(same_chunk, (col <= row) if d == 0 else (col >= row))
        for hd in range(D_HEADS):
            hs = slice(hd * D_KDIM, (hd + 1) * D_KDIM)
            att = jnp.where(keep, _dot_nt(q_dec[:, hs], k_in[:, hs]), 0.0).astype(BF16)
            v = v_ref[:, hs]
            o_intra = _dot(att, v)
            st = st_scr[d, hd]
            o_parts = [None] * cps
            for i in order:
                rs = chunk_rows[i]
                o_parts[i] = o_intra[rs] + _dot_nt(q_dec[rs, hs], st.astype(BF16))
                st = st * e_last[i][:, hs] + _dot_tn(v[rs], k_out[rs, hs])
            st_scr[d, hd] = st
            o_refs[d][:, hs] = jnp.concatenate(o_parts, axis=0)

    if want_final:
        @pl.when(n == n_steps - 1)
        def _():
            for d in range(2):
                for hd in range(D_HEADS):
                    fin_ref[d, hd] = st_scr[d, hd].T


def _hgrn_scan(tri, qs, lff, lfb, v, *, row0, n_b, t_len, s0t=None, want_final=False, into=None, name):
    cps = tri.shape[1] // CHUNK
    rows = cps * CHUNK
    n_steps = t_len // rows
    base = row0 // rows
    fwd = lambda b, n: (base + b * n_steps + n, 0)
    bwd = lambda b, n: (base + b * n_steps + n_steps - 1 - n, 0)
    blk = lambda m: pl.BlockSpec((rows, D_MODEL), m)
    st_shape = (2, D_HEADS, D_KDIM, D_KDIM)
    in_specs = [pl.BlockSpec((2, rows, rows), lambda b, n: (0, 0, 0)),
                blk(fwd), blk(fwd), blk(fwd), blk(bwd), blk(bwd), blk(bwd)]
    args = [tri, qs, lff, v, qs, lfb, v]
    if s0t is not None:
        in_specs.append(pl.BlockSpec((None,) + st_shape, lambda b, n: (b, 0, 0, 0, 0)))
        args.append(s0t)
    aliases = {}
    if into is not None:
        aliases = {len(args) + i: i for i in range(len(into))}
        in_specs += [pl.BlockSpec(memory_space=pl.ANY)] * len(into)
        args += list(into)
    o_shape = jax.ShapeDtypeStruct(qs.shape, F32)
    out_shape = [o_shape, o_shape]
    out_specs = [blk(fwd), blk(bwd)]
    if want_final:
        out_shape.append(jax.ShapeDtypeStruct((n_b,) + st_shape, F32))
        out_specs.append(pl.BlockSpec((None,) + st_shape, lambda b, n: (b, 0, 0, 0, 0)))
    kern = functools.partial(_hgrn_kernel, n_steps=n_steps, cps=cps, has_init=s0t is not None,
                             want_final=want_final, n_alias=len(aliases))
    return pl.pallas_call(
        kern,
        out_shape=tuple(out_shape),
        grid=(n_b, n_steps),
        in_specs=in_specs,
        out_specs=tuple(out_specs),
        scratch_shapes=[pltpu.VMEM(st_shape, F32)],
        input_output_aliases=aliases,
        compiler_params=_params(2),
        name=name,
    )(*args)


def _ffn_chunks(d_ff):
    chunks, f0 = [], 0
    while f0 < d_ff:
        fs = min(COL_CHUNK, d_ff - f0)
        chunks.append((f0, fs))
        f0 += fs
    return tuple(chunks)


def _out_ffn_kernel(*refs, tm, n_p, n_x, hgrn, split_out, f_chunks):
    refs = list(refs)
    x_refs = refs[:n_x]
    del refs[:n_x]
    n_o = 2 if hgrn else 1
    o_in = refs[:n_o]
    del refs[:n_o]
    if hgrn:
        gs_ref, gn_ref = refs[:2]
        del refs[:2]
    mod_ref, wo_ref, ng_ref, wg_ref, wu_ref, wd_ref = refs[:6]
    del refs[:6]
    n_out = 2 if split_out else 1
    out_refs = refs[:n_out]
    x1_scr, h_scr, acc_scr = refs[n_out:]
    is_sample = pl.program_id(0) * tm >= n_p
    mod = mod_ref[...]
    m = [mod[:, j * D_MODEL:(j + 1) * D_MODEL] for j in range(N_MOD)]

    if hgrn:
        o = o_in[0][...] + o_in[1][...]
        parts = []
        for hd in range(D_HEADS):
            hs = slice(hd * D_KDIM, (hd + 1) * D_KDIM)
            oh = o[:, hs]
            ms = jnp.mean(oh * oh, axis=-1, keepdims=True)
            parts.append(oh * lax.rsqrt(ms + EPS) * gn_ref[...] * gs_ref[:, hs])
        o = jnp.concatenate(parts, axis=-1).astype(BF16)
    else:
        o = o_in[0][...]

    x1 = _load_stream(x_refs, is_sample) + m[2] * _dot(o, wo_ref[...])
    x1_scr[...] = x1
    h_scr[...] = _norm_mod(x1, ng_ref[...], m[3], m[4]).astype(BF16)
    for idx, (f0, fs) in enumerate(f_chunks):
        hb = h_scr[...]
        g = _dot(hb, wg_ref[:, f0:f0 + fs])
        u = _dot(hb, wu_ref[:, f0:f0 + fs])
        y = _dot((_silu(g) * u).astype(BF16), wd_ref[f0:f0 + fs, :])
        acc_scr[...] = y if idx == 0 else acc_scr[...] + y
    res = x1_scr[...] + m[5] * acc_scr[...]
    if split_out:
        @pl.when(jnp.logical_not(is_sample))
        def _():
            out_refs[0][...] = res

        @pl.when(is_sample)
        def _():
            out_refs[1][...] = res
    else:
        out_refs[0][...] = res


def _out_ffn(geo, x, o_parts, mod, w_o, norm_gain, w_gate, w_up, w_down, gs=None, gn=None, split_out=False):
    tm = geo.tm
    hgrn = gs is not None
    d_ff = w_gate.shape[1]
    in_specs, args = geo.stream_specs(x)
    n_x = len(args)
    in_specs += [geo.row_spec()] * len(o_parts)
    args += list(o_parts)
    if hgrn:
        in_specs += [geo.row_spec(), _vec_spec(D_KDIM)]
        args += [gs, gn]
    in_specs += [geo.mod_spec(), _const_spec((D_MODEL, D_MODEL)), _vec_spec(D_MODEL),
                 _const_spec((D_MODEL, d_ff)), _const_spec((D_MODEL, d_ff)), _const_spec((d_ff, D_MODEL))]
    args += [mod, w_o, norm_gain, w_gate, w_up, w_down]
    if split_out:
        out_shape = (jax.ShapeDtypeStruct((geo.n_p, D_MODEL), F32), jax.ShapeDtypeStruct((geo.n_s, D_MODEL), F32))
        out_specs = (geo.prompt_spec(), geo.sample_spec())
    else:
        out_shape = jax.ShapeDtypeStruct((geo.n_t, D_MODEL), F32)
        out_specs = geo.row_spec()
    kern = functools.partial(_out_ffn_kernel, tm=tm, n_p=geo.n_p, n_x=n_x, hgrn=hgrn, split_out=split_out,
                             f_chunks=_ffn_chunks(d_ff))
    return pl.pallas_call(
        kern,
        out_shape=out_shape,
        grid=(geo.n_t // tm,),
        in_specs=in_specs,
        out_specs=out_specs,
        scratch_shapes=[pltpu.VMEM((tm, D_MODEL), F32), pltpu.VMEM((tm, D_MODEL), BF16),
                        pltpu.VMEM((tm, D_MODEL), F32)],
        compiler_params=_params(1),
        name="out_ffn",
    )(*args)


def _rope_tables(n_tokens, n_identity):
    rows = n_tokens // GRID_W
    row = jnp.repeat(jnp.arange(rows, dtype=F32), GRID_W)
    col = jnp.tile(jnp.arange(GRID_W, dtype=F32), rows)
    axis_dim = HEAD_DIM // 2
    inv_freq = ROPE_THETA ** (-jnp.arange(0, axis_dim, 2, dtype=F32) / axis_dim)
    ang = jnp.stack([row[:, None] * inv_freq, col[:, None] * inv_freq], axis=1)
    cos, sin = jnp.cos(ang), jnp.sin(ang)
    cos64 = jnp.concatenate([cos, cos], axis=-1).reshape(n_tokens, HEAD_DIM)
    sin64 = jnp.concatenate([-sin, sin], axis=-1).reshape(n_tokens, HEAD_DIM)
    reps = COL_CHUNK // HEAD_DIM
    ident = jnp.ones((n_identity, COL_CHUNK), F32)
    return (jnp.concatenate([jnp.tile(cos64, (1, reps)), ident], axis=0),
            jnp.concatenate([jnp.tile(sin64, (1, reps)), 0.0 * ident], axis=0))


def _group_matrix():
    g = jnp.arange(NORM_CHUNK) // HEAD_DIM
    return (g[:, None] == g[None, :]).astype(BF16)


def _tri_matrices(cps):
    r = jnp.arange(cps * CHUNK)
    same = r[None, :] // CHUNK == r[:, None] // CHUNK
    lower = jnp.logical_and(same, r[None, :] <= r[:, None]).astype(BF16)
    return jnp.stack([lower, lower.T])


def kernel(x_prompt, x_sample, c, cache_a_k, cache_a_v, cache_b_k, cache_b_v, cache_c_k, cache_c_v, state_d, c_ctx, norm_mix, norm_ffn, w_ada, b_ada, w_ffn_gate, w_ffn_up, w_ffn_down, w_qkv_a, w_o_a, qn_a, kn_a, subln_a, lam_q1_a, lam_k1_a, lam_q2_a, lam_k2_a, w_qkv_b, w_o_b, qn_b, kn_b, sink_b, w_qkv_c, w_o_c, qn_c, kn_c, w_in_d, w_o_d, gn_d, lb_logits_d):
    bp, tp, _ = x_prompt.shape
    bs, ts, _ = x_sample.shape
    past = cache_a_k.shape[2]
    depth = w_ada.shape[0]
    geo = _Geom(bp, tp, bs, ts, past)
    n_p = geo.n_p

    cond = jnp.concatenate([c_ctx[None], c, jnp.zeros((MOD_ROWS - 1 - bs, D_MODEL), F32)], axis=0)
    mod_all = _ada_all(cond, w_ada, b_ada).reshape(depth, MOD_ROWS, 1, N_MOD * D_MODEL)

    cos, sin = _rope_tables(ts, geo.tm)
    gmat = _group_matrix()
    tri = _tri_matrices(math.gcd(HGRN_CHUNKS_PER_STEP, math.gcd(tp // CHUNK, ts // CHUNK)))
    p_lb = jax.nn.softmax(lb_logits_d.astype(F32), axis=1)
    lb_all = jnp.cumsum(p_lb, axis=1) - p_lb[:, :1]
    q_scale = HEAD_DIM ** -0.5 * LOG2E
    tile_gain = lambda g, s=1.0: jnp.tile(g * s, COL_CHUNK // HEAD_DIM)[None]
    bf = lambda w: w.astype(BF16)

    x = (x_prompt.reshape(n_p, D_MODEL), x_sample.reshape(geo.n_s, D_MODEL))
    outs = {k: [] for k in ("ak", "av", "bk", "bv", "ck", "cv", "d")}
    for li in range(depth):
        kind, j = li % 4, li // 4
        mod = mod_all[li]
        gain = norm_mix[li][None]
        ffn_w = (norm_ffn[li][None], bf(w_ffn_gate[li]), bf(w_ffn_up[li]), bf(w_ffn_down[li]))
        split_out = li == depth - 1
        if kind == 0:
            lam_init = 0.8 - 0.6 * math.exp(-0.3 * li)
            lam = (jnp.exp(jnp.sum(lam_q1_a[j] * lam_k1_a[j])) - jnp.exp(jnp.sum(lam_q2_a[j] * lam_k2_a[j]))
                   + lam_init).astype(F32).reshape(1)
            q, k, v, k32, v32 = _qkv_proj(geo, x, mod, gain, bf(w_qkv_a[j]), gmat, tile_gain(qn_a[j], q_scale),
                                          tile_gain(kn_a[j]), cos, sin, n_q=D_MODEL, n_k=D_MODEL, n_v=D_MODEL)
            common = dict(R=2, lam=lam, subg=subln_a[j][None], lam_scale=1.0 - lam_init)
            o_p = _flash(q, k, v, row0=0, n_b=bp, t_len=tp, tq=tp, inner=True, name="attn_a_prompt", **common)
            ctx = (cache_a_k[:, j].reshape(bs, past, D_MODEL), cache_a_v[:, j].reshape(bs, past, D_MODEL))
            o = _flash(q, k, v, row0=n_p, n_b=bs, t_len=ts, tq=min(512, ts), inner=False, ctx=ctx, into=o_p,
                       name="attn_a_sample", **common)
            outs["ak"].append(k32.reshape(bp, tp, 2 * D_MODEL // (2 * HEAD_DIM), HEAD_DIM))
            outs["av"].append(v32.reshape(bp, tp, D_MODEL // (2 * HEAD_DIM), 2 * HEAD_DIM))
            x = _out_ffn(geo, x, (o,), mod, bf(w_o_a[j]), *ffn_w, split_out=split_out)
        elif kind in (1, 2):
            w_qkv, w_o, qn, kn = (w_qkv_b, w_o_b, qn_b, kn_b) if kind == 1 else (w_qkv_c, w_o_c, qn_c, kn_c)
            cache_k, cache_v = (cache_b_k, cache_b_v) if kind == 1 else (cache_c_k, cache_c_v)
            n_kv = (w_qkv.shape[-1] - D_MODEL) // 2
            q, k, v, k32, v32 = _qkv_proj(geo, x, mod, gain, bf(w_qkv[j]), gmat, tile_gain(qn[j], q_scale),
                                          tile_gain(kn[j]), cos, sin, n_q=D_MODEL, n_k=n_kv, n_v=n_kv)
            sink = sink_b[j].astype(F32) if kind == 1 else None
            ctx = (cache_k[:, j].reshape(bs, past, n_kv), cache_v[:, j].reshape(bs, past, n_kv))
            tag = "b" if kind == 1 else "c"
            o_p = _flash(q, k, v, row0=0, n_b=bp, t_len=tp, tq=tp, R=4, inner=True, sink=sink,
                         name=f"attn_{tag}_prompt")
            o = _flash(q, k, v, row0=n_p, n_b=bs, t_len=ts, tq=min(256, ts), R=4, inner=False, ctx=ctx,
                       sink=sink, window=kind == 1, into=o_p, name=f"attn_{tag}_sample")
            outs[tag + "k"].append(k32.reshape(bp, tp, n_kv // HEAD_DIM, HEAD_DIM))
            outs[tag + "v"].append(v32.reshape(bp, tp, n_kv // HEAD_DIM, HEAD_DIM))
            x = _out_ffn(geo, x, (o,), mod, bf(w_o[j]), *ffn_w, split_out=split_out)
        else:
            if isinstance(x, tuple):
                x = jnp.concatenate(x, axis=0)
            qs, lff, lfb, v, gs = _proj_d(geo, x, mod, gain, bf(w_in_d[j]), lb_all[:, li])
            of_p, ob_p, fin = _hgrn_scan(tri, qs, lff, lfb, v, row0=0, n_b=bp, t_len=tp, want_final=True,
                                         name="hgrn_prompt")
            s0t = jnp.swapaxes(state_d[:, j], -1, -2)
            o_f, o_b = _hgrn_scan(tri, qs, lff, lfb, v, row0=n_p, n_b=bs, t_len=ts, s0t=s0t, into=(of_p, ob_p),
                                  name="hgrn_sample")
            outs["d"].append(fin)
            x = _out_ffn(geo, x, (o_f, o_b), mod, bf(w_o_d[j]), *ffn_w, gs=gs, gn=gn_d[j][None],
                         split_out=split_out)

    if not isinstance(x, tuple):
        x = (x[:n_p], x[n_p:])
    stack = lambda xs: jnp.stack(xs, axis=1)
    return (x[0].reshape(bp, tp, D_MODEL), x[1].reshape(bs, ts, D_MODEL), stack(outs["ak"]), stack(outs["av"]),
            stack(outs["bk"]), stack(outs["bv"]), stack(outs["ck"]), stack(outs["cv"]), stack(outs["d"]))
```

```python
import functools
import math

import jax
import jax.numpy as jnp
from jax import lax
from jax.experimental import pallas as pl
from jax.experimental.pallas import tpu as pltpu

F32 = jnp.float32
BF16 = jnp.bfloat16

D_MODEL = 1024
HEAD_DIM = 64
GRID_W = 64
ROPE_THETA = 10000.0
WINDOW = 128
EPS = 1e-6
NEG_INF = -1e30
LOG2E = 1.4426950408889634
CHUNK = 64
HGRN_CHUNKS_PER_STEP = 4
D_KDIM = 128
D_HEADS = D_MODEL // D_KDIM
N_MOD = 6
MOD_ROWS = 8
COL_CHUNK = 512
NORM_CHUNK = 256
V7X_VMEM_BYTES = 64 * 1024 * 1024
VMEM_LIMIT = V7X_VMEM_BYTES - 8 * 1024 * 1024


def _dot(a, b):
    return jnp.dot(a, b, preferred_element_type=F32)


def _dot_nt(a, b):
    return lax.dot_general(a, b, (((1,), (1,)), ((), ())), preferred_element_type=F32)


def _dot_tn(a, b):
    return lax.dot_general(a, b, (((0,), (0,)), ((), ())), preferred_element_type=F32)


def _silu(x):
    return x * jax.nn.sigmoid(x)


def _params(n_axes, **kw):
    return pltpu.CompilerParams(dimension_semantics=("arbitrary",) * n_axes, vmem_limit_bytes=VMEM_LIMIT, **kw)


def _const_spec(shape):
    return pl.BlockSpec(shape, lambda *_: (0,) * len(shape), pipeline_mode=pl.Buffered(1))


def _norm_mod(x, gain, shift, scale):
    ms = jnp.mean(x * x, axis=-1, keepdims=True)
    return (x * lax.rsqrt(ms + EPS) * gain) * (1.0 + scale) + shift


def _split3(x):
    hi = x.astype(BF16)
    r = x - hi.astype(F32)
    mid = r.astype(BF16)
    lo = (r - mid.astype(F32)).astype(BF16)
    return hi, mid, lo


def _ada_kernel(cond_ref, w_ref, b_ref, o_ref):
    a = _silu(cond_ref[...]).astype(BF16)
    o_ref[...] = _dot(a, w_ref[...].astype(BF16)) + b_ref[...]


def _ada_all(cond, w_ada, b_ada):
    depth, d, n = w_ada.shape
    tn = n // 4
    return pl.pallas_call(
        _ada_kernel,
        out_shape=jax.ShapeDtypeStruct((depth, MOD_ROWS, n), F32),
        grid=(depth, n // tn),
        in_specs=[
            pl.BlockSpec((MOD_ROWS, d), lambda l, j: (0, 0)),
            pl.BlockSpec((None, d, tn), lambda l, j: (l, 0, j)),
            pl.BlockSpec((None, 1, tn), lambda l, j: (l, 0, j)),
        ],
        out_specs=pl.BlockSpec((None, MOD_ROWS, tn), lambda l, j: (l, 0, j)),
        compiler_params=_params(2),
        name="ada_mod",
    )(cond, w_ada, b_ada.reshape(depth, 1, n))


class _Geom:
    def __init__(self, bp, tp, bs, ts, past):
        self.bp, self.tp, self.bs, self.ts, self.past = bp, tp, bs, ts, past
        self.n_p = bp * tp
        self.n_s = bs * ts
        self.n_t = self.n_p + self.n_s
        self.tm = math.gcd(512, math.gcd(self.n_p, ts))
        assert self.n_p % ts == 0 and ts % CHUNK == 0 and tp % CHUNK == 0
        assert bs + 1 <= MOD_ROWS

    def mod_row(self, i):
        return jnp.where(i * self.tm < self.n_p, 0, 1 + (i * self.tm - self.n_p) // self.ts)

    def prompt_tile(self, i):
        return jnp.minimum(i, self.n_p // self.tm - 1)

    def sample_tile(self, i):
        return jnp.maximum(i - self.n_p // self.tm, 0)

    def rope_tile(self, i):
        per_seq = self.ts // self.tm
        return jnp.where(i * self.tm < self.n_p, per_seq, self.sample_tile(i) % per_seq)

    def row_spec(self, width=D_MODEL):
        return pl.BlockSpec((self.tm, width), lambda i: (i, 0))

    def prompt_spec(self, width=D_MODEL):
        return pl.BlockSpec((self.tm, width), lambda i: (self.prompt_tile(i), 0))

    def sample_spec(self, width=D_MODEL):
        return pl.BlockSpec((self.tm, width), lambda i: (self.sample_tile(i), 0))

    def mod_spec(self):
        return pl.BlockSpec((None, 1, N_MOD * D_MODEL), lambda i: (self.mod_row(i), 0, 0))

    def stream_specs(self, x):
        if isinstance(x, tuple):
            return [self.prompt_spec(), self.sample_spec()], list(x)
        return [self.row_spec()], [x]


def _vec_spec(n):
    return pl.BlockSpec((1, n), lambda i: (0, 0))


def _load_stream(x_refs, is_sample):
    if len(x_refs) == 2:
        return jnp.where(is_sample, x_refs[1][...], x_refs[0][...])
    return x_refs[0][...]


def _qkv_kernel(*refs, tm, n_p, n_x, n_q, n_k, n_v):
    refs = list(refs)
    x_refs = refs[:n_x]
    del refs[:n_x]
    (mod_ref, gain_ref, w_ref, gmat_ref, qg_ref, kg_ref, cos_ref, sin_ref,
     q_ref, k_ref, v_ref, k32_ref, v32_ref, h_scr, k32_scr, v32_scr) = refs
    i = pl.program_id(0)
    is_sample = i * tm >= n_p
    is_prompt = jnp.logical_not(is_sample)
    mod = mod_ref[...]
    h = _norm_mod(_load_stream(x_refs, is_sample), gain_ref[...], mod[:, 0:D_MODEL], mod[:, D_MODEL:2 * D_MODEL])
    h_scr[...] = h.astype(BF16)

    def head_norm(y, gain):
        y2 = (y * y).astype(BF16)
        ms = jnp.concatenate([_dot(y2[:, c:c + NORM_CHUNK], gmat_ref[...])
                              for c in range(0, y.shape[1], NORM_CHUNK)], axis=1) * (1.0 / HEAD_DIM)
        return y * lax.rsqrt(ms + EPS) * gain

    def rope(y):
        w = y.shape[1]
        lane = lax.broadcasted_iota(jnp.int32, y.shape, 1)
        partner = jnp.where(lane % 32 < 16, pltpu.roll(y, w - 16, 1), pltpu.roll(y, 16, 1))
        return y * cos_ref[:, :w] + partner * sin_ref[:, :w]

    def store_qk(y, dst, stage, cs):
        dst[:, cs] = rope(y).astype(BF16)
        if stage is not None:
            stage[:, cs] = y

    def store_v(y, cs):
        v_ref[:, cs] = y.astype(BF16)
        v32_scr[:, cs] = y

    def proj(c0, width):
        return _dot(h_scr[...], w_ref[:, c0:c0 + width])

    for c0 in range(0, n_q, COL_CHUNK):
        store_qk(head_norm(proj(c0, COL_CHUNK), qg_ref[...]), q_ref, None, slice(c0, c0 + COL_CHUNK))
    if n_k + n_v == COL_CHUNK:
        y = proj(n_q, COL_CHUNK)
        store_qk(head_norm(y[:, :n_k], kg_ref[:, :n_k]), k_ref, k32_scr, slice(0, n_k))
        store_v(y[:, n_k:], slice(0, n_v))
    else:
        for c0 in range(0, n_k, COL_CHUNK):
            cs = slice(c0, c0 + COL_CHUNK)
            store_qk(head_norm(proj(n_q + c0, COL_CHUNK), kg_ref[...]), k_ref, k32_scr, cs)
        for c0 in range(0, n_v, COL_CHUNK):
            store_v(proj(n_q + n_k + c0, COL_CHUNK), slice(c0, c0 + COL_CHUNK))

    @pl.when(is_prompt)
    def _():
        k32_ref[...] = k32_scr[...]
        v32_ref[...] = v32_scr[...]


def _qkv_proj(geo, x, mod, gain, w, gmat, q_gain, k_gain, cos, sin, *, n_q, n_k, n_v, v_hd):
    tm = geo.tm
    x_specs, x_args = geo.stream_specs(x)
    rope_spec = pl.BlockSpec((tm, COL_CHUNK), lambda i: (geo.rope_tile(i), 0))
    kern = functools.partial(_qkv_kernel, tm=tm, n_p=geo.n_p, n_x=len(x_args), n_q=n_q, n_k=n_k, n_v=n_v)
    q, k, v, k32, v32 = pl.pallas_call(
        kern,
        out_shape=(
            jax.ShapeDtypeStruct((geo.n_t, n_q), BF16),
            jax.ShapeDtypeStruct((geo.n_t, n_k), BF16),
            jax.ShapeDtypeStruct((geo.n_t, n_v), BF16),
            jax.ShapeDtypeStruct((geo.n_p, n_k), F32),
            jax.ShapeDtypeStruct((geo.n_p, n_v), F32),
        ),
        grid=(geo.n_t // tm,),
        in_specs=x_specs + [
            geo.mod_spec(),
            _vec_spec(D_MODEL),
            _const_spec((D_MODEL, n_q + n_k + n_v)),
            _const_spec((NORM_CHUNK, NORM_CHUNK)),
            _vec_spec(COL_CHUNK),
            _vec_spec(COL_CHUNK),
            rope_spec,
            rope_spec,
        ],
        out_specs=(geo.row_spec(n_q), geo.row_spec(n_k), geo.row_spec(n_v),
                   geo.prompt_spec(n_k), geo.prompt_spec(n_v)),
        scratch_shapes=[pltpu.VMEM((tm, D_MODEL), BF16), pltpu.VMEM((tm, n_k), F32), pltpu.VMEM((tm, n_v), F32)],
        compiler_params=_params(1),
        name="qkv_proj",
    )(*x_args, mod, gain, w, gmat, q_gain, k_gain, cos, sin)
    return q, k, v, k32.reshape(geo.n_p, n_k // HEAD_DIM, HEAD_DIM), v32.reshape(geo.n_p, n_v // v_hd, v_hd)


def _flash_kernel(*refs, R, tq, kb, n_own, t_len, n_inner, has_ctx, use_sink, window, diff, lam_scale, n_alias):
    refs = list(refs)
    q_ref, k_ref, v_ref = refs[:3]
    del refs[:3]
    if has_ctx:
        ck_ref, cv_ref = refs[:2]
        del refs[:2]
    if use_sink:
        sink_ref = refs.pop(0)
    if diff:
        lam_ref, subg_ref = refs[:2]
        del refs[:2]
    del refs[:n_alias]
    o_ref, qs_scr = refs[:2]
    del refs[:2]
    cached = n_inner == 1
    if cached:
        vt_scr = refs.pop(0)
    if has_ctx:
        ctk_scr, ctvt_scr = refs

    W = HEAD_DIM * R
    M = R * tq
    DV = 2 * HEAD_DIM if diff else HEAD_DIM
    qi = pl.program_id(2)

    def transposed(v):
        return v.astype(F32).T.astype(BF16)

    if cached:
        first = qi == 0 if diff else jnp.logical_and(qi == 0, pl.program_id(1) == 0)

        @pl.when(first)
        def _():
            vt_scr[...] = transposed(v_ref[...])
            if has_ctx:
                ctk_scr[...] = ck_ref[...].astype(BF16)
                ctvt_scr[...] = cv_ref[...].T.astype(BF16)
    else:
        vt_all = transposed(v_ref[...])

    def scores(kblk, mask, qs):
        st = _dot_nt(kblk, qs_scr[...] if qs is None else qs)
        if mask is not None:
            st = jnp.where(mask, st, NEG_INF)
        return st

    def update(st, vt, state, sink_row):
        m_new = jnp.max(st, axis=0, keepdims=True)
        if state is not None:
            m_prev, l_prev, acc_prev = state
            m_new = jnp.maximum(m_new, m_prev)
        elif sink_row is not None:
            m_new = jnp.maximum(m_new, sink_row)
        p = jnp.exp2(st - m_new)
        l_new = jnp.sum(p, axis=0, keepdims=True)
        acc = _dot(vt, p.astype(BF16))
        if state is not None:
            alpha = jnp.exp2(m_prev - m_new)
            l_new = alpha * l_prev + l_new
            acc = alpha * acc_prev + acc
        elif sink_row is not None:
            l_new = l_new + jnp.exp2(sink_row - m_new)
        return m_new, l_new, acc

    def head(c):
        static = not cached
        q_cols = slice(c * W, (c + 1) * W) if static else slice(None)
        k_cols = slice(c * W, (c + 1) * W) if (static and diff) else slice(None)
        q = q_ref[:, q_cols].astype(F32)
        lane = lax.broadcasted_iota(jnp.int32, (tq, W), 1) // HEAD_DIM
        stacked = []
        for r in range(R):
            t = jnp.where(lane == r, q, 0.0)
            if not diff:
                if static:
                    if (c - r) % R:
                        t = pltpu.roll(t, HEAD_DIM * ((c - r) % R), 1)
                else:
                    t = pltpu.roll(t, HEAD_DIM * ((c - r + R) % R), 1)
            stacked.append(t.astype(BF16))
        qs = None
        if static:
            qs = jnp.concatenate(stacked, axis=0)
        else:
            for r in range(R):
                qs_scr[r * tq:(r + 1) * tq, :] = stacked[r]

        def vt_block(off):
            if static:
                return vt_all[c * DV:(c + 1) * DV, off:off + kb]
            rows = slice(None) if diff else pl.ds(pl.multiple_of(c * DV, DV), DV)
            return vt_scr[rows, pl.ds(off, kb)]

        sink_row = None
        if use_sink:
            sink_row = jnp.concatenate(
                [jnp.full((1, tq), sink_ref[c * R + r] * LOG2E, F32) for r in range(R)], axis=1)

        blocks = []
        if has_ctx:
            rows = slice(None) if diff else pl.ds(pl.multiple_of(c * DV, DV), DV)
            blocks.append((lambda: ctk_scr[...], lambda: ctvt_scr[rows, :], None))
        if window:
            q0 = qi * tq
            start = pl.multiple_of(jnp.clip(q0 - WINDOW, 0, t_len - kb), WINDOW)
            kpos = start + lax.broadcasted_iota(jnp.int32, (kb, M), 0)
            qpos = q0 + lax.broadcasted_iota(jnp.int32, (kb, M), 1) % tq
            blocks.append((lambda: k_ref[pl.ds(start, kb), k_cols], lambda: vt_block(start),
                           jnp.abs(qpos - kpos) <= WINDOW))
        else:
            for j in range(n_own):
                blocks.append((lambda j=j: k_ref[j * kb:(j + 1) * kb, k_cols], lambda j=j: vt_block(j * kb), None))

        state = None
        st_next = scores(blocks[0][0](), blocks[0][2], qs)
        for j, (_, vt_get, _) in enumerate(blocks):
            st = st_next
            if j + 1 < len(blocks):
                st_next = scores(blocks[j + 1][0](), blocks[j + 1][2], qs)
            state = update(st, vt_get(), state, sink_row if j == 0 else None)

        _, l_fin, acc = state
        a = acc * (1.0 / l_fin)
        if diff:
            ot = a[:, :tq] - lam_ref[0] * a[:, tq:]
            ms = jnp.mean(ot * ot, axis=0, keepdims=True)
            o = (ot * lax.rsqrt(ms + EPS)).T * (subg_ref[...] * lam_scale)
        else:
            o = jnp.concatenate([a[:, r * tq:(r + 1) * tq] for r in range(R)], axis=0).T
        o_ref[:, q_cols] = o.astype(BF16)

    if cached:
        head(pl.program_id(1))
    else:
        for c in range(n_inner):
            head(c)


def _flash(q, k, v, *, row0, n_b, t_len, tq, R, inner, ctx=None, sink=None, window=False,
           lam=None, subg=None, lam_scale=1.0, into=None, name):
    diff = lam is not None
    W = HEAD_DIM * R
    w_kv = k.shape[1]
    n_c = D_MODEL // W
    n_qb = t_len // tq
    if window:
        kb, n_own = tq + 2 * WINDOW, 1
    else:
        kb = min(t_len, 512)
        n_own = t_len // kb
    q_base, k_base = row0 // tq, row0 // t_len
    M = R * tq
    scratch = [pltpu.VMEM((M, W), BF16)]
    if inner:
        assert n_qb == 1 and n_own == 1 and ctx is None
        grid = (n_b, 1, 1)
        q_spec = pl.BlockSpec((tq, D_MODEL), lambda b, c, qi: (q_base + b, 0))
        kv_spec = pl.BlockSpec((t_len, w_kv), lambda b, c, qi: (k_base + b, 0))
        o_spec = q_spec
        n_inner = n_c
    else:
        grid = (n_b, n_c, n_qb)
        q_spec = pl.BlockSpec((tq, W), lambda b, c, qi: (q_base + b * n_qb + qi, c))
        o_spec = q_spec
        if diff:
            kv_spec = pl.BlockSpec((t_len, W), lambda b, c, qi: (k_base + b, c))
            scratch.append(pltpu.VMEM((W, t_len), BF16))
        else:
            kv_spec = pl.BlockSpec((t_len, w_kv), lambda b, c, qi: (k_base + b, 0))
            scratch.append(pltpu.VMEM((w_kv, t_len), BF16))
        n_inner = 1
    in_specs = [q_spec, kv_spec, kv_spec]
    args = [q, k, v]
    if ctx is not None:
        ck, cv = ctx
        p_len = ck.shape[1]
        w_ctx = W if diff else w_kv
        cmap = (lambda b, c, qi: (b, 0, c)) if diff else (lambda b, c, qi: (b, 0, 0))
        scratch += [pltpu.VMEM((p_len, w_ctx), BF16), pltpu.VMEM((w_ctx, p_len), BF16)]
        in_specs += [pl.BlockSpec((None, p_len, w_ctx), cmap)] * 2
        args += [ck, cv]
    if sink is not None:
        in_specs.append(pl.BlockSpec(memory_space=pltpu.SMEM))
        args.append(sink)
    if diff:
        in_specs += [pl.BlockSpec(memory_space=pltpu.SMEM), pl.BlockSpec((1, W), lambda b, c, qi: (0, 0))]
        args += [lam, subg]
    aliases = {}
    if into is not None:
        aliases = {len(args): 0}
        in_specs.append(pl.BlockSpec(memory_space=pl.ANY))
        args.append(into)
    kern = functools.partial(
        _flash_kernel, R=R, tq=tq, kb=kb, n_own=n_own, t_len=t_len, n_inner=n_inner, has_ctx=ctx is not None,
        use_sink=sink is not None, window=window, diff=diff, lam_scale=lam_scale, n_alias=len(aliases))
    return pl.pallas_call(
        kern,
        out_shape=jax.ShapeDtypeStruct(q.shape, BF16),
        grid=grid,
        in_specs=in_specs,
        out_specs=o_spec,
        scratch_shapes=scratch,
        input_output_aliases=aliases,
        compiler_params=_params(3),
        name=name,
    )(*args)


def _proj_d_kernel(x_ref, mod_ref, gain_ref, w_ref, lb_ref, qs_ref, lff_ref, lfb_ref, v_ref, gs_ref, h_scr):
    mod = mod_ref[...]
    h = _norm_mod(x_ref[...], gain_ref[...], mod[:, 0:D_MODEL], mod[:, D_MODEL:2 * D_MODEL])
    h_scr[...] = h.astype(BF16)

    def proj(part, c0):
        return _dot(h_scr[...], w_ref[:, part * D_MODEL + c0:part * D_MODEL + c0 + COL_CHUNK])

    for c0 in range(0, D_MODEL, COL_CHUNK):
        cs = slice(c0, c0 + COL_CHUNK)
        qs_ref[:, cs] = _silu(proj(0, c0))
        for d, ref in ((0, lff_ref), (1, lfb_ref)):
            lb = lb_ref[d:d + 1, cs]
            ref[:, cs] = jnp.log(lb + (1.0 - lb) * jax.nn.sigmoid(proj(1 + d, c0)))
        v_ref[:, cs] = proj(3, c0).astype(BF16)
        gs_ref[:, cs] = _silu(proj(4, c0))


def _proj_d(geo, x, mod, gain, w, lb):
    spec = geo.row_spec()
    f32_out = jax.ShapeDtypeStruct((geo.n_t, D_MODEL), F32)
    return pl.pallas_call(
        _proj_d_kernel,
        out_shape=(f32_out, f32_out, f32_out, jax.ShapeDtypeStruct((geo.n_t, D_MODEL), BF16), f32_out),
        grid=(geo.n_t // geo.tm,),
        in_specs=[spec, geo.mod_spec(), _vec_spec(D_MODEL),
                  _const_spec((D_MODEL, 5 * D_MODEL)), pl.BlockSpec((2, D_MODEL), lambda i: (0, 0))],
        out_specs=(spec,) * 5,
        scratch_shapes=[pltpu.VMEM((geo.tm, D_MODEL), BF16)],
        compiler_params=_params(1),
        name="hgrn_proj",
    )(x, mod, gain, w, lb)


def _hgrn_kernel(*refs, n_steps, cps, has_init, want_final, n_alias):
    refs = list(refs)
    tri_ref = refs.pop(0)
    ins = [refs[:3], refs[3:6]]
    del refs[:6]
    if has_init:
        s0_ref = refs.pop(0)
    del refs[:n_alias]
    o_refs = refs[:2]
    del refs[:2]
    if want_final:
        fin_ref = refs.pop(0)
    st_scr = refs.pop(0)
    n = pl.program_id(1)

    @pl.when(n == 0)
    def _():
        st_scr[...] = s0_ref[...] if has_init else jnp.zeros_like(st_scr)

    rows = cps * CHUNK
    row = lax.broadcasted_iota(jnp.int32, (rows, rows), 0)
    col = lax.broadcasted_iota(jnp.int32, (rows, rows), 1)
    same_chunk = row // CHUNK == col // CHUNK
    prep = []
    for d in range(2):
        qs_ref, lf_ref, v_ref = ins[d]
        tri = tri_ref[d]
        lf = lf_ref[...]
        cum = sum(_dot(tri, part) for part in _split3(lf))
        q_dec = (qs_ref[...] * jnp.exp(cum)).astype(BF16)
        k_in = (1.0 - jnp.exp(lf)) * jnp.exp(-cum)
        order = list(range(cps)) if d == 0 else list(reversed(range(cps)))
        e_last = [jnp.exp(cum[i * CHUNK + CHUNK - 1:(i + 1) * CHUNK] if d == 0 else cum[i * CHUNK:i * CHUNK + 1])
                  for i in range(cps)]
        k_out = jnp.concatenate([k_in[i * CHUNK:(i + 1) * CHUNK] * e_last[i] for i in range(cps)],
                                axis=0).astype(BF16)
        k_in = k_in.astype(BF16)
        keep = jnp.logical_and(same_chunk, (col <= row) if d == 0 else (col >= row))
        prep.append((q_dec, k_in, k_out, e_last, keep, order, v_ref))

    chunk_rows = [slice(i * CHUNK, (i + 1) * CHUNK) for i in range(cps)]
    heads = [slice(hd * D_KDIM, (hd + 1) * D_KDIM) for hd in range(D_HEADS)]
    for d, (q_dec, k_in, _, _, keep, _, v_ref) in enumerate(prep):
        for hs in heads:
            att = jnp.where(keep, _dot_nt(q_dec[:, hs], k_in[:, hs]), 0.0).astype(BF16)
            o_refs[d][:, hs] = _dot(att, v_ref[:, hs])
    for t in range(cps):
        for d, (q_dec, _, k_out, e_last, _, order, v_ref) in enumerate(prep):
            rs = chunk_rows[order[t]]
            for hd, hs in enumerate(heads):
                st = st_scr[d, hd]
                o_refs[d][rs, hs] += _dot_nt(q_dec[rs, hs], st.astype(BF16))
                st_scr[d, hd] = st * e_last[order[t]][:, hs] + _dot_tn(v_ref[rs, hs], k_out[rs, hs])

    if want_final:
        @pl.when(n == n_steps - 1)
        def _():
            for d in range(2):
                for hd in range(D_HEADS):
                    fin_ref[d, hd] = st_scr[d, hd].T


def _hgrn_scan(tri, qs, lff, lfb, v, *, row0, n_b, t_len, s0t=None, want_final=False, into=None, name):
    cps = tri.shape[1] // CHUNK
    rows = cps * CHUNK
    n_steps = t_len // rows
    base = row0 // rows
    fwd = lambda b, n: (base + b * n_steps + n, 0)
    bwd = lambda b, n: (base + b * n_steps + n_steps - 1 - n, 0)
    blk = lambda m: pl.BlockSpec((rows, D_MODEL), m)
    st_shape = (2, D_HEADS, D_KDIM, D_KDIM)
    in_specs = [pl.BlockSpec((2, rows, rows), lambda b, n: (0, 0, 0)),
                blk(fwd), blk(fwd), blk(fwd), blk(bwd), blk(bwd), blk(bwd)]
    args = [tri, qs, lff, v, qs, lfb, v]
    if s0t is not None:
        in_specs.append(pl.BlockSpec((None,) + st_shape, lambda b, n: (b, 0, 0, 0, 0)))
        args.append(s0t)
    aliases = {}
    if into is not None:
        aliases = {len(args) + i: i for i in range(len(into))}
        in_specs += [pl.BlockSpec(memory_space=pl.ANY)] * len(into)
        args += list(into)
    o_shape = jax.ShapeDtypeStruct(qs.shape, F32)
    out_shape = [o_shape, o_shape]
    out_specs = [blk(fwd), blk(bwd)]
    if want_final:
        out_shape.append(jax.ShapeDtypeStruct((n_b,) + st_shape, F32))
        out_specs.append(pl.BlockSpec((None,) + st_shape, lambda b, n: (b, 0, 0, 0, 0)))
    kern = functools.partial(_hgrn_kernel, n_steps=n_steps, cps=cps, has_init=s0t is not None,
                             want_final=want_final, n_alias=len(aliases))
    return pl.pallas_call(
        kern,
        out_shape=tuple(out_shape),
        grid=(n_b, n_steps),
        in_specs=in_specs,
        out_specs=tuple(out_specs),
        scratch_shapes=[pltpu.VMEM(st_shape, F32)],
        input_output_aliases=aliases,
        compiler_params=_params(2),
        name=name,
    )(*args)


def _ffn_chunks(d_ff):
    chunks, f0 = [], 0
    while f0 < d_ff:
        fs = min(COL_CHUNK, d_ff - f0)
        chunks.append((f0, fs))
        f0 += fs
    return tuple(chunks)


def _out_ffn_kernel(*refs, tm, n_p, n_x, hgrn, split_out, f_chunks):
    refs = list(refs)
    x_refs = refs[:n_x]
    del refs[:n_x]
    n_o = 2 if hgrn else 1
    o_in = refs[:n_o]
    del refs[:n_o]
    if hgrn:
        gs_ref, gn_ref = refs[:2]
        del refs[:2]
    mod_ref, wo_ref, ng_ref, wg_ref, wu_ref, wd_ref = refs[:6]
    del refs[:6]
    n_out = 2 if split_out else 1
    out_refs = refs[:n_out]
    x1_scr, h_scr, acc_scr = refs[n_out:]
    is_sample = pl.program_id(0) * tm >= n_p
    mod = mod_ref[...]
    m = [mod[:, j * D_MODEL:(j + 1) * D_MODEL] for j in range(N_MOD)]

    if hgrn:
        o = o_in[0][...] + o_in[1][...]
        parts = []
        for hd in range(D_HEADS):
            hs = slice(hd * D_KDIM, (hd + 1) * D_KDIM)
            oh = o[:, hs]
            ms = jnp.mean(oh * oh, axis=-1, keepdims=True)
            parts.append(oh * lax.rsqrt(ms + EPS) * gn_ref[...] * gs_ref[:, hs])
        o = jnp.concatenate(parts, axis=-1).astype(BF16)
    else:
        o = o_in[0][...]

    x1 = _load_stream(x_refs, is_sample) + m[2] * _dot(o, wo_ref[...])
    x1_scr[...] = x1
    h_scr[...] = _norm_mod(x1, ng_ref[...], m[3], m[4]).astype(BF16)
    for idx, (f0, fs) in enumerate(f_chunks):
        hb = h_scr[...]
        g = _dot(hb, wg_ref[:, f0:f0 + fs])
        u = _dot(hb, wu_ref[:, f0:f0 + fs])
        y = _dot((_silu(g) * u).astype(BF16), wd_ref[f0:f0 + fs, :])
        acc_scr[...] = y if idx == 0 else acc_scr[...] + y
    res = x1_scr[...] + m[5] * acc_scr[...]
    if split_out:
        @pl.when(jnp.logical_not(is_sample))
        def _():
            out_refs[0][...] = res

        @pl.when(is_sample)
        def _():
            out_refs[1][...] = res
    else:
        out_refs[0][...] = res


def _out_ffn(geo, x, o_parts, mod, w_o, norm_gain, w_gate, w_up, w_down, gs=None, gn=None, split_out=False):
    tm = geo.tm
    hgrn = gs is not None
    d_ff = w_gate.shape[1]
    in_specs, args = geo.stream_specs(x)
    n_x = len(args)
    in_specs += [geo.row_spec()] * len(o_parts)
    args += list(o_parts)
    if hgrn:
        in_specs += [geo.row_spec(), _vec_spec(D_KDIM)]
        args += [gs, gn]
    in_specs += [geo.mod_spec(), _const_spec((D_MODEL, D_MODEL)), _vec_spec(D_MODEL),
                 _const_spec((D_MODEL, d_ff)), _const_spec((D_MODEL, d_ff)), _const_spec((d_ff, D_MODEL))]
    args += [mod, w_o, norm_gain, w_gate, w_up, w_down]
    if split_out:
        out_shape = (jax.ShapeDtypeStruct((geo.n_p, D_MODEL), F32), jax.ShapeDtypeStruct((geo.n_s, D_MODEL), F32))
        out_specs = (geo.prompt_spec(), geo.sample_spec())
    else:
        out_shape = jax.ShapeDtypeStruct((geo.n_t, D_MODEL), F32)
        out_specs = geo.row_spec()
    kern = functools.partial(_out_ffn_kernel, tm=tm, n_p=geo.n_p, n_x=n_x, hgrn=hgrn, split_out=split_out,
                             f_chunks=_ffn_chunks(d_ff))
    return pl.pallas_call(
        kern,
        out_shape=out_shape,
        grid=(geo.n_t // tm,),
        in_specs=in_specs,
        out_specs=out_specs,
        scratch_shapes=[pltpu.VMEM((tm, D_MODEL), F32), pltpu.VMEM((tm, D_MODEL), BF16),
                        pltpu.VMEM((tm, D_MODEL), F32)],
        compiler_params=_params(1),
        name="out_ffn",
    )(*args)


def _rope_tables(n_tokens, n_identity):
    rows = n_tokens // GRID_W
    row = jnp.repeat(jnp.arange(rows, dtype=F32), GRID_W)
    col = jnp.tile(jnp.arange(GRID_W, dtype=F32), rows)
    axis_dim = HEAD_DIM // 2
    inv_freq = ROPE_THETA ** (-jnp.arange(0, axis_dim, 2, dtype=F32) / axis_dim)
    ang = jnp.stack([row[:, None] * inv_freq, col[:, None] * inv_freq], axis=1)
    cos, sin = jnp.cos(ang), jnp.sin(ang)
    cos64 = jnp.concatenate([cos, cos], axis=-1).reshape(n_tokens, HEAD_DIM)
    sin64 = jnp.concatenate([-sin, sin], axis=-1).reshape(n_tokens, HEAD_DIM)
    reps = COL_CHUNK // HEAD_DIM
    ident = jnp.ones((n_identity, COL_CHUNK), F32)
    return (jnp.concatenate([jnp.tile(cos64, (1, reps)), ident], axis=0),
            jnp.concatenate([jnp.tile(sin64, (1, reps)), 0.0 * ident], axis=0))


def _group_matrix():
    g = jnp.arange(NORM_CHUNK) // HEAD_DIM
    return (g[:, None] == g[None, :]).astype(BF16)


def _tri_matrices(cps):
    r = jnp.arange(cps * CHUNK)
    same = r[None, :] // CHUNK == r[:, None] // CHUNK
    lower = jnp.logical_and(same, r[None, :] <= r[:, None]).astype(BF16)
    return jnp.stack([lower, lower.T])


def kernel(x_prompt, x_sample, c, cache_a_k, cache_a_v, cache_b_k, cache_b_v, cache_c_k, cache_c_v, state_d, c_ctx, norm_mix, norm_ffn, w_ada, b_ada, w_ffn_gate, w_ffn_up, w_ffn_down, w_qkv_a, w_o_a, qn_a, kn_a, subln_a, lam_q1_a, lam_k1_a, lam_q2_a, lam_k2_a, w_qkv_b, w_o_b, qn_b, kn_b, sink_b, w_qkv_c, w_o_c, qn_c, kn_c, w_in_d, w_o_d, gn_d, lb_logits_d):
    bp, tp, _ = x_prompt.shape
    bs, ts, _ = x_sample.shape
    past = cache_a_k.shape[2]
    depth = w_ada.shape[0]
    geo = _Geom(bp, tp, bs, ts, past)
    n_p = geo.n_p

    cond = jnp.concatenate([c_ctx[None], c, jnp.zeros((MOD_ROWS - 1 - bs, D_MODEL), F32)], axis=0)
    mod_all = _ada_all(cond, w_ada, b_ada).reshape(depth, MOD_ROWS, 1, N_MOD * D_MODEL)

    cos, sin = _rope_tables(ts, geo.tm)
    gmat = _group_matrix()
    tri = _tri_matrices(math.gcd(HGRN_CHUNKS_PER_STEP, math.gcd(tp // CHUNK, ts // CHUNK)))
    p_lb = jax.nn.softmax(lb_logits_d.astype(F32), axis=1)
    lb_all = jnp.cumsum(p_lb, axis=1) - p_lb[:, :1]
    q_scale = HEAD_DIM ** -0.5 * LOG2E
    tile_gain = lambda g, s=1.0: jnp.tile(g * s, COL_CHUNK // HEAD_DIM)[None]
    bf = lambda w: w.astype(BF16)

    x = (x_prompt.reshape(n_p, D_MODEL), x_sample.reshape(geo.n_s, D_MODEL))
    outs = {k: [] for k in ("ak", "av", "bk", "bv", "ck", "cv", "d")}
    for li in range(depth):
        kind, j = li % 4, li // 4
        mod = mod_all[li]
        gain = norm_mix[li][None]
        ffn_w = (norm_ffn[li][None], bf(w_ffn_gate[li]), bf(w_ffn_up[li]), bf(w_ffn_down[li]))
        split_out = li == depth - 1
        if kind == 0:
            lam_init = 0.8 - 0.6 * math.exp(-0.3 * li)
            lam = (jnp.exp(jnp.sum(lam_q1_a[j] * lam_k1_a[j])) - jnp.exp(jnp.sum(lam_q2_a[j] * lam_k2_a[j]))
                   + lam_init).astype(F32).reshape(1)
            q, k, v, k32, v32 = _qkv_proj(geo, x, mod, gain, bf(w_qkv_a[j]), gmat, tile_gain(qn_a[j], q_scale),
                                          tile_gain(kn_a[j]), cos, sin, n_q=D_MODEL, n_k=D_MODEL, n_v=D_MODEL,
                                          v_hd=2 * HEAD_DIM)
            common = dict(R=2, lam=lam, subg=subln_a[j][None], lam_scale=1.0 - lam_init)
            o_p = _flash(q, k, v, row0=0, n_b=bp, t_len=tp, tq=tp, inner=True, name="attn_a_prompt", **common)
            ctx = (cache_a_k[:, j].reshape(bs, past, D_MODEL), cache_a_v[:, j].reshape(bs, past, D_MODEL))
            o = _flash(q, k, v, row0=n_p, n_b=bs, t_len=ts, tq=min(1024, ts), inner=False, ctx=ctx, into=o_p,
                       name="attn_a_sample", **common)
            outs["ak"].append(k32.reshape((bp, tp) + k32.shape[1:]))
            outs["av"].append(v32.reshape((bp, tp) + v32.shape[1:]))
            x = _out_ffn(geo, x, (o,), mod, bf(w_o_a[j]), *ffn_w, split_out=split_out)
        elif kind in (1, 2):
            w_qkv, w_o, qn, kn = (w_qkv_b, w_o_b, qn_b, kn_b) if kind == 1 else (w_qkv_c, w_o_c, qn_c, kn_c)
            cache_k, cache_v = (cache_b_k, cache_b_v) if kind == 1 else (cache_c_k, cache_c_v)
            n_kv = (w_qkv.shape[-1] - D_MODEL) // 2
            q, k, v, k32, v32 = _qkv_proj(geo, x, mod, gain, bf(w_qkv[j]), gmat, tile_gain(qn[j], q_scale),
                                          tile_gain(kn[j]), cos, sin, n_q=D_MODEL, n_k=n_kv, n_v=n_kv,
                                          v_hd=HEAD_DIM)
            sink = sink_b[j].astype(F32) if kind == 1 else None
            ctx = (cache_k[:, j].reshape(bs, past, n_kv), cache_v[:, j].reshape(bs, past, n_kv))
            tag = "b" if kind == 1 else "c"
            o_p = _flash(q, k, v, row0=0, n_b=bp, t_len=tp, tq=tp, R=4, inner=True, sink=sink,
                         name=f"attn_{tag}_prompt")
            o = _flash(q, k, v, row0=n_p, n_b=bs, t_len=ts, tq=min(256 if kind == 1 else 512, ts), R=4,
                       inner=False, ctx=ctx, sink=sink, window=kind == 1, into=o_p, name=f"attn_{tag}_sample")
            outs[tag + "k"].append(k32.reshape((bp, tp) + k32.shape[1:]))
            outs[tag + "v"].append(v32.reshape((bp, tp) + v32.shape[1:]))
            x = _out_ffn(geo, x, (o,), mod, bf(w_o[j]), *ffn_w, split_out=split_out)
        else:
            if isinstance(x, tuple):
                x = jnp.concatenate(x, axis=0)
            qs, lff, lfb, v, gs = _proj_d(geo, x, mod, gain, bf(w_in_d[j]), lb_all[:, li])
            of_p, ob_p, fin = _hgrn_scan(tri, qs, lff, lfb, v, row0=0, n_b=bp, t_len=tp, want_final=True,
                                         name="hgrn_prompt")
            s0t = jnp.swapaxes(state_d[:, j], -1, -2)
            o_f, o_b = _hgrn_scan(tri, qs, lff, lfb, v, row0=n_p, n_b=bs, t_len=ts, s0t=s0t, into=(of_p, ob_p),
                                  name="hgrn_sample")
            outs["d"].append(fin)
            x = _out_ffn(geo, x, (o_f, o_b), mod, bf(w_o_d[j]), *ffn_w, gs=gs, gn=gn_d[j][None],
                         split_out=split_out)

    if not isinstance(x, tuple):
        x = (x[:n_p], x[n_p:])
    stack = lambda xs: jnp.stack(xs, axis=1)
    return (x[0].reshape(bp, tp, D_MODEL), x[1].reshape(bs, ts, D_MODEL), stack(outs["ak"]), stack(outs["av"]),
            stack(outs["bk"]), stack(outs["bv"]), stack(outs["ck"]), stack(outs["cv"]), stack(outs["d"]))
```

```python
import functools
import math

import jax
import jax.numpy as jnp
from jax import lax
from jax.experimental import pallas as pl
from jax.experimental.pallas import tpu as pltpu

F32 = jnp.float32
BF16 = jnp.bfloat16

D_MODEL = 1024
HEAD_DIM = 64
GRID_W = 64
ROPE_THETA = 10000.0
WINDOW = 128
EPS = 1e-6
NEG_INF = -1e30
LOG2E = 1.4426950408889634
CHUNK = 64
HGRN_CHUNKS_PER_STEP = 4
D_KDIM = 128
D_HEADS = D_MODEL // D_KDIM
N_MOD = 6
MOD_ROWS = 8
COL_CHUNK = 512
NORM_CHUNK = 256
SUM_ROWS = 16
V7X_VMEM_BYTES = 64 * 1024 * 1024
VMEM_LIMIT = V7X_VMEM_BYTES - 8 * 1024 * 1024


def _dot(a, b):
    return jnp.dot(a, b, preferred_element_type=F32)


def _dot_nt(a, b):
    return lax.dot_general(a, b, (((1,), (1,)), ((), ())), preferred_element_type=F32)


def _dot_tn(a, b):
    return lax.dot_general(a, b, (((0,), (0,)), ((), ())), preferred_element_type=F32)


def _silu(x):
    return x * jax.nn.sigmoid(x)


def _params(n_axes, **kw):
    return pltpu.CompilerParams(dimension_semantics=("arbitrary",) * n_axes, vmem_limit_bytes=VMEM_LIMIT, **kw)


def _const_spec(shape):
    return pl.BlockSpec(shape, lambda *_: (0,) * len(shape), pipeline_mode=pl.Buffered(1))


def _norm_mod(x, gain, shift, scale):
    ms = jnp.mean(x * x, axis=-1, keepdims=True)
    return (x * lax.rsqrt(ms + EPS) * gain) * (1.0 + scale) + shift


def _split3(x):
    hi = x.astype(BF16)
    r = x - hi.astype(F32)
    mid = r.astype(BF16)
    lo = (r - mid.astype(F32)).astype(BF16)
    return hi, mid, lo


def _ada_kernel(cond_ref, w_ref, b_ref, o_ref):
    a = _silu(cond_ref[...]).astype(BF16)
    o_ref[...] = _dot(a, w_ref[...].astype(BF16)) + b_ref[...]


def _ada_all(cond, w_ada, b_ada):
    depth, d, n = w_ada.shape
    tn = n // 4
    return pl.pallas_call(
        _ada_kernel,
        out_shape=jax.ShapeDtypeStruct((depth, MOD_ROWS, n), F32),
        grid=(depth, n // tn),
        in_specs=[
            pl.BlockSpec((MOD_ROWS, d), lambda l, j: (0, 0)),
            pl.BlockSpec((None, d, tn), lambda l, j: (l, 0, j)),
            pl.BlockSpec((None, 1, tn), lambda l, j: (l, 0, j)),
        ],
        out_specs=pl.BlockSpec((None, MOD_ROWS, tn), lambda l, j: (l, 0, j)),
        compiler_params=_params(2),
        name="ada_mod",
    )(cond, w_ada, b_ada.reshape(depth, 1, n))


class _Geom:
    def __init__(self, bp, tp, bs, ts, past):
        self.bp, self.tp, self.bs, self.ts, self.past = bp, tp, bs, ts, past
        self.n_p = bp * tp
        self.n_s = bs * ts
        self.n_t = self.n_p + self.n_s
        self.tm = math.gcd(512, math.gcd(self.n_p, ts))
        assert self.n_p % ts == 0 and ts % CHUNK == 0 and tp % CHUNK == 0
        assert bs + 1 <= MOD_ROWS

    def mod_row(self, i):
        return jnp.where(i * self.tm < self.n_p, 0, 1 + (i * self.tm - self.n_p) // self.ts)

    def prompt_tile(self, i):
        return jnp.minimum(i, self.n_p // self.tm - 1)

    def sample_tile(self, i):
        return jnp.maximum(i - self.n_p // self.tm, 0)

    def rope_tile(self, i):
        per_seq = self.ts // self.tm
        return jnp.where(i * self.tm < self.n_p, per_seq, self.sample_tile(i) % per_seq)

    def row_spec(self, width=D_MODEL):
        return pl.BlockSpec((self.tm, width), lambda i: (i, 0))

    def prompt_spec(self, width=D_MODEL):
        return pl.BlockSpec((self.tm, width), lambda i: (self.prompt_tile(i), 0))

    def sample_spec(self, width=D_MODEL):
        return pl.BlockSpec((self.tm, width), lambda i: (self.sample_tile(i), 0))

    def mod_spec(self):
        return pl.BlockSpec((None, 1, N_MOD * D_MODEL), lambda i: (self.mod_row(i), 0, 0))

    def stream_specs(self, x):
        if isinstance(x, tuple):
            return [self.prompt_spec(), self.sample_spec()], list(x)
        return [self.row_spec()], [x]


def _vec_spec(n):
    return pl.BlockSpec((1, n), lambda i: (0, 0))


def _load_stream(x_refs, is_sample):
    if len(x_refs) == 2:
        return jnp.where(is_sample, x_refs[1][...], x_refs[0][...])
    return x_refs[0][...]


def _qkv_kernel(*refs, tm, n_p, n_x, n_q, n_k, n_v):
    refs = list(refs)
    x_refs = refs[:n_x]
    del refs[:n_x]
    (mod_ref, gain_ref, w_ref, gmat_ref, qg_ref, kg_ref, cos_ref, sin_ref,
     q_ref, k_ref, v_ref, k32_ref, v32_ref, h_scr, k32_scr, v32_scr) = refs
    i = pl.program_id(0)
    is_sample = i * tm >= n_p
    is_prompt = jnp.logical_not(is_sample)
    mod = mod_ref[...]
    h = _norm_mod(_load_stream(x_refs, is_sample), gain_ref[...], mod[:, 0:D_MODEL], mod[:, D_MODEL:2 * D_MODEL])
    h_scr[...] = h.astype(BF16)

    def head_norm(y, gain):
        y2 = (y * y).astype(BF16)
        ms = jnp.concatenate([_dot(y2[:, c:c + NORM_CHUNK], gmat_ref[...])
                              for c in range(0, y.shape[1], NORM_CHUNK)], axis=1) * (1.0 / HEAD_DIM)
        return y * lax.rsqrt(ms + EPS) * gain

    def rope(y):
        w = y.shape[1]
        lane = lax.broadcasted_iota(jnp.int32, y.shape, 1)
        partner = jnp.where(lane % 32 < 16, pltpu.roll(y, w - 16, 1), pltpu.roll(y, 16, 1))
        return y * cos_ref[:, :w] + partner * sin_ref[:, :w]

    def store_qk(y, dst, stage, cs):
        dst[:, cs] = rope(y).astype(BF16)
        if stage is not None:
            stage[:, cs] = y

    def store_v(y, cs):
        v_ref[:, cs] = y.astype(BF16)
        v32_scr[:, cs] = y

    def proj(c0, width):
        return _dot(h_scr[...], w_ref[:, c0:c0 + width])

    for c0 in range(0, n_q, COL_CHUNK):
        store_qk(head_norm(proj(c0, COL_CHUNK), qg_ref[...]), q_ref, None, slice(c0, c0 + COL_CHUNK))
    if n_k + n_v == COL_CHUNK:
        y = proj(n_q, COL_CHUNK)
        store_qk(head_norm(y[:, :n_k], kg_ref[:, :n_k]), k_ref, k32_scr, slice(0, n_k))
        store_v(y[:, n_k:], slice(0, n_v))
    else:
        for c0 in range(0, n_k, COL_CHUNK):
            cs = slice(c0, c0 + COL_CHUNK)
            store_qk(head_norm(proj(n_q + c0, COL_CHUNK), kg_ref[...]), k_ref, k32_scr, cs)
        for c0 in range(0, n_v, COL_CHUNK):
            store_v(proj(n_q + n_k + c0, COL_CHUNK), slice(c0, c0 + COL_CHUNK))

    @pl.when(is_prompt)
    def _():
        k32_ref[...] = k32_scr[...]
        v32_ref[...] = v32_scr[...]


def _qkv_proj(geo, x, mod, gain, w, gmat, q_gain, k_gain, cos, sin, *, n_q, n_k, n_v, v_hd):
    tm = geo.tm
    x_specs, x_args = geo.stream_specs(x)
    rope_spec = pl.BlockSpec((tm, COL_CHUNK), lambda i: (geo.rope_tile(i), 0))
    kern = functools.partial(_qkv_kernel, tm=tm, n_p=geo.n_p, n_x=len(x_args), n_q=n_q, n_k=n_k, n_v=n_v)
    q, k, v, k32, v32 = pl.pallas_call(
        kern,
        out_shape=(
            jax.ShapeDtypeStruct((geo.n_t, n_q), BF16),
            jax.ShapeDtypeStruct((geo.n_t, n_k), BF16),
            jax.ShapeDtypeStruct((geo.n_t, n_v), BF16),
            jax.ShapeDtypeStruct((geo.n_p, n_k), F32),
            jax.ShapeDtypeStruct((geo.n_p, n_v), F32),
        ),
        grid=(geo.n_t // tm,),
        in_specs=x_specs + [
            geo.mod_spec(),
            _vec_spec(D_MODEL),
            _const_spec((D_MODEL, n_q + n_k + n_v)),
            _const_spec((NORM_CHUNK, NORM_CHUNK)),
            _vec_spec(COL_CHUNK),
            _vec_spec(COL_CHUNK),
            rope_spec,
            rope_spec,
        ],
        out_specs=(geo.row_spec(n_q), geo.row_spec(n_k), geo.row_spec(n_v),
                   geo.prompt_spec(n_k), geo.prompt_spec(n_v)),
        scratch_shapes=[pltpu.VMEM((tm, D_MODEL), BF16), pltpu.VMEM((tm, n_k), F32), pltpu.VMEM((tm, n_v), F32)],
        compiler_params=_params(1),
        name="qkv_proj",
    )(*x_args, mod, gain, w, gmat, q_gain, k_gain, cos, sin)
    return q, k, v, k32.reshape(geo.n_p, n_k // HEAD_DIM, HEAD_DIM), v32.reshape(geo.n_p, n_v // v_hd, v_hd)


def _flash_kernel(*refs, R, tq, kb, n_own, t_len, n_inner, has_ctx, use_sink, window, diff, lam_scale):
    refs = list(refs)
    q_ref, k_ref, v_ref = refs[:3]
    del refs[:3]
    if has_ctx:
        ck_ref, cv_ref = refs[:2]
        del refs[:2]
    if use_sink:
        sink_ref = refs.pop(0)
    if diff:
        lam_ref, subg_ref = refs[:2]
        del refs[:2]
    o_ref, qs_scr = refs[:2]
    del refs[:2]
    cached = n_inner == 1
    if cached:
        vt_scr = refs.pop(0)
    if has_ctx:
        ctk_scr, ctvt_scr = refs

    W = HEAD_DIM * R
    M = R * tq
    DV = 2 * HEAD_DIM if diff else HEAD_DIM
    qi = pl.program_id(2)

    def transposed(v):
        return v.astype(F32).T.astype(BF16)

    if cached:
        first = qi == 0 if diff else jnp.logical_and(qi == 0, pl.program_id(1) == 0)

        def fill(dst, vt):
            if diff:
                dst[...] = vt
            else:
                for g in range(vt.shape[0] // DV):
                    dst[g, :DV, :] = vt[g * DV:(g + 1) * DV]
                    dst[g, DV:, :] = jnp.ones((SUM_ROWS, vt.shape[1]), BF16)

        @pl.when(first)
        def _():
            fill(vt_scr, transposed(v_ref[...]))
            if has_ctx:
                ctk_scr[...] = ck_ref[...].astype(BF16)
                fill(ctvt_scr, cv_ref[...].T.astype(BF16))
    else:
        vt_all = transposed(v_ref[...])

    def scores(kblk, mask, qs):
        st = _dot_nt(kblk, qs_scr[...] if qs is None else qs)
        if mask is not None:
            st = jnp.where(mask, st, NEG_INF)
        return st

    def update(st, vt, state, sink_row):
        m_new = jnp.max(st, axis=0, keepdims=True)
        if state is not None:
            m_prev, l_prev, acc_prev = state
            m_new = jnp.maximum(m_new, m_prev)
        elif sink_row is not None:
            m_new = jnp.maximum(m_new, sink_row)
        p = jnp.exp2(st - m_new)
        acc = _dot(vt, p.astype(BF16))
        if diff:
            l_new = jnp.sum(p, axis=0, keepdims=True)
        else:
            l_new, acc = acc[DV:DV + 1], acc[:DV]
        if state is not None:
            alpha = jnp.exp2(m_prev - m_new)
            l_new = alpha * l_prev + l_new
            acc = alpha * acc_prev + acc
        elif sink_row is not None:
            l_new = l_new + jnp.exp2(sink_row - m_new)
        return m_new, l_new, acc

    def head(c):
        static = not cached
        q_cols = slice(c * W, (c + 1) * W) if static else slice(None)
        k_cols = slice(c * W, (c + 1) * W) if (static and diff) else slice(None)
        q = q_ref[:, q_cols].astype(F32)
        lane = lax.broadcasted_iota(jnp.int32, (tq, W), 1) // HEAD_DIM
        stacked = []
        for r in range(R):
            t = jnp.where(lane == r, q, 0.0)
            if not diff:
                if static:
                    if (c - r) % R:
                        t = pltpu.roll(t, HEAD_DIM * ((c - r) % R), 1)
                else:
                    t = pltpu.roll(t, HEAD_DIM * ((c - r + R) % R), 1)
            stacked.append(t.astype(BF16))
        qs = None
        if static:
            qs = jnp.concatenate(stacked, axis=0)
        else:
            for r in range(R):
                qs_scr[r * tq:(r + 1) * tq, :] = stacked[r]

        def vt_block(off):
            if static:
                vt = vt_all[c * DV:(c + 1) * DV, off:off + kb]
                return vt if diff else jnp.concatenate([vt, jnp.ones((SUM_ROWS, kb), BF16)], axis=0)
            return vt_scr[:, pl.ds(off, kb)] if diff else vt_scr[c, :, pl.ds(off, kb)]

        sink_row = None
        if use_sink:
            sink_row = jnp.concatenate(
                [jnp.full((1, tq), sink_ref[c * R + r] * LOG2E, F32) for r in range(R)], axis=1)

        blocks = []
        if has_ctx:
            blocks.append((lambda: ctk_scr[...], lambda: ctvt_scr[...] if diff else ctvt_scr[c], None))
        if window:
            q0 = qi * tq
            start = pl.multiple_of(jnp.clip(q0 - WINDOW, 0, t_len - kb), WINDOW)
            kpos = start + lax.broadcasted_iota(jnp.int32, (kb, M), 0)
            qpos = q0 + lax.broadcasted_iota(jnp.int32, (kb, M), 1) % tq
            blocks.append((lambda: k_ref[pl.ds(start, kb), k_cols], lambda: vt_block(start),
                           jnp.abs(qpos - kpos) <= WINDOW))
        else:
            for j in range(n_own):
                blocks.append((lambda j=j: k_ref[j * kb:(j + 1) * kb, k_cols], lambda j=j: vt_block(j * kb), None))

        state = None
        st_next = scores(blocks[0][0](), blocks[0][2], qs)
        for j, (_, vt_get, _) in enumerate(blocks):
            st = st_next
            if j + 1 < len(blocks):
                st_next = scores(blocks[j + 1][0](), blocks[j + 1][2], qs)
            state = update(st, vt_get(), state, sink_row if j == 0 else None)

        _, l_fin, acc = state
        a = acc * (1.0 / l_fin)
        if diff:
            ot = a[:, :tq] - lam_ref[0] * a[:, tq:]
            ms = jnp.mean(ot * ot, axis=0, keepdims=True)
            o = (ot * lax.rsqrt(ms + EPS)).T * (subg_ref[...] * lam_scale)
        else:
            o = jnp.concatenate([a[:, r * tq:(r + 1) * tq] for r in range(R)], axis=0).T
        o_ref[:, q_cols] = o.astype(BF16)

    if cached:
        head(pl.program_id(1))
    else:
        for c in range(n_inner):
            head(c)


def _flash(q, k, v, *, row0, n_b, t_len, tq, R, inner, ctx=None, sink=None, window=False,
           lam=None, subg=None, lam_scale=1.0, name):
    diff = lam is not None
    W = HEAD_DIM * R
    w_kv = k.shape[1]
    n_c = D_MODEL // W
    n_qb = t_len // tq
    if window:
        kb, n_own = tq + 2 * WINDOW, 1
    else:
        kb = min(t_len, 512)
        n_own = t_len // kb
    q_base, k_base = row0 // tq, row0 // t_len
    M = R * tq
    scratch = [pltpu.VMEM((M, W), BF16)]
    if inner:
        assert n_qb == 1 and n_own == 1 and ctx is None
        grid = (n_b, 1, 1)
        q_spec = pl.BlockSpec((tq, D_MODEL), lambda b, c, qi: (q_base + b, 0))
        kv_spec = pl.BlockSpec((t_len, w_kv), lambda b, c, qi: (k_base + b, 0))
        o_spec = pl.BlockSpec((tq, D_MODEL), lambda b, c, qi: (b, 0))
        n_inner = n_c
    else:
        grid = (n_b, n_c, n_qb)
        q_spec = pl.BlockSpec((tq, W), lambda b, c, qi: (q_base + b * n_qb + qi, c))
        o_spec = pl.BlockSpec((tq, W), lambda b, c, qi: (b * n_qb + qi, c))
        if diff:
            kv_spec = pl.BlockSpec((t_len, W), lambda b, c, qi: (k_base + b, c))
            scratch.append(pltpu.VMEM((W, t_len), BF16))
        else:
            kv_spec = pl.BlockSpec((t_len, w_kv), lambda b, c, qi: (k_base + b, 0))
            scratch.append(pltpu.VMEM((w_kv // HEAD_DIM, HEAD_DIM + SUM_ROWS, t_len), BF16))
        n_inner = 1
    in_specs = [q_spec, kv_spec, kv_spec]
    args = [q, k, v]
    if ctx is not None:
        ck, cv = ctx
        p_len = ck.shape[1]
        w_ctx = W if diff else w_kv
        cmap = (lambda b, c, qi: (b, 0, c)) if diff else (lambda b, c, qi: (b, 0, 0))
        ctvt_shape = (w_ctx, p_len) if diff else (w_ctx // HEAD_DIM, HEAD_DIM + SUM_ROWS, p_len)
        scratch += [pltpu.VMEM((p_len, w_ctx), BF16), pltpu.VMEM(ctvt_shape, BF16)]
        in_specs += [pl.BlockSpec((None, p_len, w_ctx), cmap)] * 2
        args += [ck, cv]
    if sink is not None:
        in_specs.append(pl.BlockSpec(memory_space=pltpu.SMEM))
        args.append(sink)
    if diff:
        in_specs += [pl.BlockSpec(memory_space=pltpu.SMEM), pl.BlockSpec((1, W), lambda b, c, qi: (0, 0))]
        args += [lam, subg]
    kern = functools.partial(
        _flash_kernel, R=R, tq=tq, kb=kb, n_own=n_own, t_len=t_len, n_inner=n_inner, has_ctx=ctx is not None,
        use_sink=sink is not None, window=window, diff=diff, lam_scale=lam_scale)
    return pl.pallas_call(
        kern,
        out_shape=jax.ShapeDtypeStruct((n_b * t_len, D_MODEL), BF16),
        grid=grid,
        in_specs=in_specs,
        out_specs=o_spec,
        scratch_shapes=scratch,
        compiler_params=_params(3),
        name=name,
    )(*args)


def _proj_d_kernel(x_ref, mod_ref, gain_ref, w_ref, lb_ref, qs_ref, lff_ref, lfb_ref, v_ref, gs_ref, h_scr):
    mod = mod_ref[...]
    h = _norm_mod(x_ref[...], gain_ref[...], mod[:, 0:D_MODEL], mod[:, D_MODEL:2 * D_MODEL])
    h_scr[...] = h.astype(BF16)

    def proj(part, c0):
        return _dot(h_scr[...], w_ref[:, part * D_MODEL + c0:part * D_MODEL + c0 + COL_CHUNK])

    for c0 in range(0, D_MODEL, COL_CHUNK):
        cs = slice(c0, c0 + COL_CHUNK)
        qs_ref[:, cs] = _silu(proj(0, c0))
        for d, ref in ((0, lff_ref), (1, lfb_ref)):
            lb = lb_ref[d:d + 1, cs]
            ref[:, cs] = jnp.log(lb + (1.0 - lb) * jax.nn.sigmoid(proj(1 + d, c0)))
        v_ref[:, cs] = proj(3, c0).astype(BF16)
        gs_ref[:, cs] = _silu(proj(4, c0))


def _proj_d(geo, x, mod, gain, w, lb):
    spec = geo.row_spec()
    f32_out = jax.ShapeDtypeStruct((geo.n_t, D_MODEL), F32)
    return pl.pallas_call(
        _proj_d_kernel,
        out_shape=(f32_out, f32_out, f32_out, jax.ShapeDtypeStruct((geo.n_t, D_MODEL), BF16), f32_out),
        grid=(geo.n_t // geo.tm,),
        in_specs=[spec, geo.mod_spec(), _vec_spec(D_MODEL),
                  _const_spec((D_MODEL, 5 * D_MODEL)), pl.BlockSpec((2, D_MODEL), lambda i: (0, 0))],
        out_specs=(spec,) * 5,
        scratch_shapes=[pltpu.VMEM((geo.tm, D_MODEL), BF16)],
        compiler_params=_params(1),
        name="hgrn_proj",
    )(x, mod, gain, w, lb)


class _ScanPlan:
    def __init__(self, geo, rows):
        self.npp, self.nps = geo.tp // rows, geo.ts // rows
        self.s_p, self.s_s = geo.bp * self.npp, geo.bs * self.nps
        self.bp = geo.bp

    def split(self, s):
        lat = s >= self.s_p
        r = s - self.s_p
        seq = jnp.where(lat, r // self.nps, s // self.npp)
        n = jnp.where(lat, r % self.nps, s % self.npp)
        per = jnp.where(lat, self.nps, self.npp)
        base = jnp.where(lat, self.s_p + seq * self.nps, seq * self.npp)
        return lat, seq, n, per, base

    def fwd_block(self, s):
        _, _, n, _, base = self.split(s)
        return base + n

    def bwd_block(self, s):
        _, _, n, per, base = self.split(s)
        return base + per - 1 - n


def _hgrn_kernel(tri_ref, qf_ref, lff_ref, vf_ref, qb_ref, lfb_ref, vb_ref, s0_ref, of_ref, ob_ref, fin_ref,
                 st_scr, *, plan, cps):
    ins = [(qf_ref, lff_ref, vf_ref), (qb_ref, lfb_ref, vb_ref)]
    o_refs = [of_ref, ob_ref]
    lat, _, n, per, _ = plan.split(pl.program_id(0))

    @pl.when(n == 0)
    def _():
        st_scr[...] = jnp.where(lat, s0_ref[...], 0.0)

    rows = cps * CHUNK
    row = lax.broadcasted_iota(jnp.int32, (rows, rows), 0)
    col = lax.broadcasted_iota(jnp.int32, (rows, rows), 1)
    same_chunk = row // CHUNK == col // CHUNK
    prep = []
    for d in range(2):
        qs_ref, lf_ref, v_ref = ins[d]
        tri = tri_ref[d]
        lf = lf_ref[...]
        cum = sum(_dot(tri, part) for part in _split3(lf))
        q_dec = (qs_ref[...] * jnp.exp(cum)).astype(BF16)
        k_in = (1.0 - jnp.exp(lf)) * jnp.exp(-cum)
        order = list(range(cps)) if d == 0 else list(reversed(range(cps)))
        e_last = [jnp.exp(cum[i * CHUNK + CHUNK - 1:(i + 1) * CHUNK] if d == 0 else cum[i * CHUNK:i * CHUNK + 1])
                  for i in range(cps)]
        k_out = jnp.concatenate([k_in[i * CHUNK:(i + 1) * CHUNK] * e_last[i] for i in range(cps)],
                                axis=0).astype(BF16)
        k_in = k_in.astype(BF16)
        keep = jnp.logical_and(same_chunk, (col <= row) if d == 0 else (col >= row))
        prep.append((q_dec, k_in, k_out, e_last, keep, order, v_ref))

    chunk_rows = [slice(i * CHUNK, (i + 1) * CHUNK) for i in range(cps)]
    heads = [slice(hd * D_KDIM, (hd + 1) * D_KDIM) for hd in range(D_HEADS)]
    for d, (q_dec, k_in, _, _, keep, _, v_ref) in enumerate(prep):
        for hs in heads:
            att = jnp.where(keep, _dot_nt(q_dec[:, hs], k_in[:, hs]), 0.0).astype(BF16)
            o_refs[d][:, hs] = _dot(att, v_ref[:, hs])
    for t in range(cps):
        for d, (q_dec, _, k_out, e_last, _, order, v_ref) in enumerate(prep):
            rs = chunk_rows[order[t]]
            for hd, hs in enumerate(heads):
                st = st_scr[d, hd]
                o_refs[d][rs, hs] += _dot_nt(q_dec[rs, hs], st.astype(BF16))
                st_scr[d, hd] = st * e_last[order[t]][:, hs] + _dot_tn(v_ref[rs, hs], k_out[rs, hs])

    @pl.when(jnp.logical_and(n == per - 1, jnp.logical_not(lat)))
    def _():
        for d in range(2):
            for hd in range(D_HEADS):
                fin_ref[d, hd] = st_scr[d, hd].T


def _hgrn_scan(geo, tri, qs, lff, lfb, v, s0t):
    cps = tri.shape[1] // CHUNK
    rows = cps * CHUNK
    plan = _ScanPlan(geo, rows)
    fwd = pl.BlockSpec((rows, D_MODEL), lambda s: (plan.fwd_block(s), 0))
    bwd = pl.BlockSpec((rows, D_MODEL), lambda s: (plan.bwd_block(s), 0))
    st_shape = (2, D_HEADS, D_KDIM, D_KDIM)
    s0_spec = pl.BlockSpec((None,) + st_shape,
                           lambda s: (jnp.where(plan.split(s)[0], plan.split(s)[1], 0), 0, 0, 0, 0))
    fin_spec = pl.BlockSpec((None,) + st_shape,
                            lambda s: (jnp.where(plan.split(s)[0], plan.bp - 1, plan.split(s)[1]), 0, 0, 0, 0))
    o_shape = jax.ShapeDtypeStruct(qs.shape, F32)
    return pl.pallas_call(
        functools.partial(_hgrn_kernel, plan=plan, cps=cps),
        out_shape=(o_shape, o_shape, jax.ShapeDtypeStruct((geo.bp,) + st_shape, F32)),
        grid=(plan.s_p + plan.s_s,),
        in_specs=[pl.BlockSpec((2, rows, rows), lambda s: (0, 0, 0)), fwd, fwd, fwd, bwd, bwd, bwd, s0_spec],
        out_specs=(fwd, bwd, fin_spec),
        scratch_shapes=[pltpu.VMEM(st_shape, F32)],
        compiler_params=_params(1),
        name="hgrn_scan",
    )(tri, qs, lff, v, qs, lfb, v, s0t)


def _ffn_chunks(d_ff):
    chunks, f0 = [], 0
    while f0 < d_ff:
        fs = min(COL_CHUNK, d_ff - f0)
        chunks.append((f0, fs))
        f0 += fs
    return tuple(chunks)


def _out_ffn_kernel(*refs, tm, n_p, n_x, hgrn, split_out, f_chunks):
    refs = list(refs)
    x_refs = refs[:n_x]
    del refs[:n_x]
    o_in = refs[:2]
    del refs[:2]
    if hgrn:
        gs_ref, gn_ref = refs[:2]
        del refs[:2]
    mod_ref, wo_ref, ng_ref, wg_ref, wu_ref, wd_ref = refs[:6]
    del refs[:6]
    n_out = 2 if split_out else 1
    out_refs = refs[:n_out]
    x1_scr, h_scr, acc_scr = refs[n_out:]
    is_sample = pl.program_id(0) * tm >= n_p
    mod = mod_ref[...]
    m = [mod[:, j * D_MODEL:(j + 1) * D_MODEL] for j in range(N_MOD)]

    if hgrn:
        o = o_in[0][...] + o_in[1][...]
        parts = []
        for hd in range(D_HEADS):
            hs = slice(hd * D_KDIM, (hd + 1) * D_KDIM)
            oh = o[:, hs]
            ms = jnp.mean(oh * oh, axis=-1, keepdims=True)
            parts.append(oh * lax.rsqrt(ms + EPS) * gn_ref[...] * gs_ref[:, hs])
        o = jnp.concatenate(parts, axis=-1).astype(BF16)
    else:
        o = _load_stream(o_in, is_sample)

    x1 = _load_stream(x_refs, is_sample) + m[2] * _dot(o, wo_ref[...])
    x1_scr[...] = x1
    h_scr[...] = _norm_mod(x1, ng_ref[...], m[3], m[4]).astype(BF16)
    for idx, (f0, fs) in enumerate(f_chunks):
        hb = h_scr[...]
        g = _dot(hb, wg_ref[:, f0:f0 + fs])
        u = _dot(hb, wu_ref[:, f0:f0 + fs])
        y = _dot((_silu(g) * u).astype(BF16), wd_ref[f0:f0 + fs, :])
        acc_scr[...] = y if idx == 0 else acc_scr[...] + y
    res = x1_scr[...] + m[5] * acc_scr[...]
    if split_out:
        @pl.when(jnp.logical_not(is_sample))
        def _():
            out_refs[0][...] = res

        @pl.when(is_sample)
        def _():
            out_refs[1][...] = res
    else:
        out_refs[0][...] = res


def _out_ffn(geo, x, o_parts, mod, w_o, norm_gain, w_gate, w_up, w_down, gs=None, gn=None, split_out=False):
    tm = geo.tm
    hgrn = gs is not None
    d_ff = w_gate.shape[1]
    in_specs, args = geo.stream_specs(x)
    n_x = len(args)
    args += list(o_parts)
    if hgrn:
        in_specs += [geo.row_spec(), geo.row_spec(), geo.row_spec(), _vec_spec(D_KDIM)]
        args += [gs, gn]
    else:
        in_specs += [geo.prompt_spec(), geo.sample_spec()]
    in_specs += [geo.mod_spec(), _const_spec((D_MODEL, D_MODEL)), _vec_spec(D_MODEL),
                 _const_spec((D_MODEL, d_ff)), _const_spec((D_MODEL, d_ff)), _const_spec((d_ff, D_MODEL))]
    args += [mod, w_o, norm_gain, w_gate, w_up, w_down]
    if split_out:
        out_shape = (jax.ShapeDtypeStruct((geo.n_p, D_MODEL), F32), jax.ShapeDtypeStruct((geo.n_s, D_MODEL), F32))
        out_specs = (geo.prompt_spec(), geo.sample_spec())
    else:
        out_shape = jax.ShapeDtypeStruct((geo.n_t, D_MODEL), F32)
        out_specs = geo.row_spec()
    kern = functools.partial(_out_ffn_kernel, tm=tm, n_p=geo.n_p, n_x=n_x, hgrn=hgrn, split_out=split_out,
                             f_chunks=_ffn_chunks(d_ff))
    return pl.pallas_call(
        kern,
        out_shape=out_shape,
        grid=(geo.n_t // tm,),
        in_specs=in_specs,
        out_specs=out_specs,
        scratch_shapes=[pltpu.VMEM((tm, D_MODEL), F32), pltpu.VMEM((tm, D_MODEL), BF16),
                        pltpu.VMEM((tm, D_MODEL), F32)],
        compiler_params=_params(1),
        name="out_ffn",
    )(*args)


def _rope_tables(n_tokens, n_identity):
    rows = n_tokens // GRID_W
    row = jnp.repeat(jnp.arange(rows, dtype=F32), GRID_W)
    col = jnp.tile(jnp.arange(GRID_W, dtype=F32), rows)
    axis_dim = HEAD_DIM // 2
    inv_freq = ROPE_THETA ** (-jnp.arange(0, axis_dim, 2, dtype=F32) / axis_dim)
    ang = jnp.stack([row[:, None] * inv_freq, col[:, None] * inv_freq], axis=1)
    cos, sin = jnp.cos(ang), jnp.sin(ang)
    cos64 = jnp.concatenate([cos, cos], axis=-1).reshape(n_tokens, HEAD_DIM)
    sin64 = jnp.concatenate([-sin, sin], axis=-1).reshape(n_tokens, HEAD_DIM)
    reps = COL_CHUNK // HEAD_DIM
    ident = jnp.ones((n_identity, COL_CHUNK), F32)
    return (jnp.concatenate([jnp.tile(cos64, (1, reps)), ident], axis=0),
            jnp.concatenate([jnp.tile(sin64, (1, reps)), 0.0 * ident], axis=0))


def _group_matrix():
    g = jnp.arange(NORM_CHUNK) // HEAD_DIM
    return (g[:, None] == g[None, :]).astype(BF16)


def _tri_matrices(cps):
    r = jnp.arange(cps * CHUNK)
    same = r[None, :] // CHUNK == r[:, None] // CHUNK
    lower = jnp.logical_and(same, r[None, :] <= r[:, None]).astype(BF16)
    return jnp.stack([lower, lower.T])


def kernel(x_prompt, x_sample, c, cache_a_k, cache_a_v, cache_b_k, cache_b_v, cache_c_k, cache_c_v, state_d, c_ctx, norm_mix, norm_ffn, w_ada, b_ada, w_ffn_gate, w_ffn_up, w_ffn_down, w_qkv_a, w_o_a, qn_a, kn_a, subln_a, lam_q1_a, lam_k1_a, lam_q2_a, lam_k2_a, w_qkv_b, w_o_b, qn_b, kn_b, sink_b, w_qkv_c, w_o_c, qn_c, kn_c, w_in_d, w_o_d, gn_d, lb_logits_d):
    bp, tp, _ = x_prompt.shape
    bs, ts, _ = x_sample.shape
    past = cache_a_k.shape[2]
    depth = w_ada.shape[0]
    geo = _Geom(bp, tp, bs, ts, past)
    n_p = geo.n_p

    cond = jnp.concatenate([c_ctx[None], c, jnp.zeros((MOD_ROWS - 1 - bs, D_MODEL), F32)], axis=0)
    mod_all = _ada_all(cond, w_ada, b_ada).reshape(depth, MOD_ROWS, 1, N_MOD * D_MODEL)

    cos, sin = _rope_tables(ts, geo.tm)
    gmat = _group_matrix()
    tri = _tri_matrices(math.gcd(HGRN_CHUNKS_PER_STEP, math.gcd(tp // CHUNK, ts // CHUNK)))
    p_lb = jax.nn.softmax(lb_logits_d.astype(F32), axis=1)
    lb_all = jnp.cumsum(p_lb, axis=1) - p_lb[:, :1]
    q_scale = HEAD_DIM ** -0.5 * LOG2E
    tile_gain = lambda g, s=1.0: jnp.tile(g * s, COL_CHUNK // HEAD_DIM)[None]
    bf = lambda w: w.astype(BF16)

    x = (x_prompt.reshape(n_p, D_MODEL), x_sample.reshape(geo.n_s, D_MODEL))
    outs = {k: [] for k in ("ak", "av", "bk", "bv", "ck", "cv", "d")}
    for li in range(depth):
        kind, j = li % 4, li // 4
        mod = mod_all[li]
        gain = norm_mix[li][None]
        ffn_w = (norm_ffn[li][None], bf(w_ffn_gate[li]), bf(w_ffn_up[li]), bf(w_ffn_down[li]))
        split_out = li == depth - 1
        if kind == 0:
            lam_init = 0.8 - 0.6 * math.exp(-0.3 * li)
            lam = (jnp.exp(jnp.sum(lam_q1_a[j] * lam_k1_a[j])) - jnp.exp(jnp.sum(lam_q2_a[j] * lam_k2_a[j]))
                   + lam_init).astype(F32).reshape(1)
            q, k, v, k32, v32 = _qkv_proj(geo, x, mod, gain, bf(w_qkv_a[j]), gmat, tile_gain(qn_a[j], q_scale),
                                          tile_gain(kn_a[j]), cos, sin, n_q=D_MODEL, n_k=D_MODEL, n_v=D_MODEL,
                                          v_hd=2 * HEAD_DIM)
            common = dict(R=2, lam=lam, subg=subln_a[j][None], lam_scale=1.0 - lam_init)
            o_p = _flash(q, k, v, row0=0, n_b=bp, t_len=tp, tq=tp, inner=True, name="attn_a_prompt", **common)
            ctx = (cache_a_k[:, j].reshape(bs, past, D_MODEL), cache_a_v[:, j].reshape(bs, past, D_MODEL))
            o_s = _flash(q, k, v, row0=n_p, n_b=bs, t_len=ts, tq=min(1024, ts), inner=False, ctx=ctx,
                         name="attn_a_sample", **common)
            outs["ak"].append(k32.reshape((bp, tp) + k32.shape[1:]))
            outs["av"].append(v32.reshape((bp, tp) + v32.shape[1:]))
            x = _out_ffn(geo, x, (o_p, o_s), mod, bf(w_o_a[j]), *ffn_w, split_out=split_out)
        elif kind in (1, 2):
            w_qkv, w_o, qn, kn = (w_qkv_b, w_o_b, qn_b, kn_b) if kind == 1 else (w_qkv_c, w_o_c, qn_c, kn_c)
            cache_k, cache_v = (cache_b_k, cache_b_v) if kind == 1 else (cache_c_k, cache_c_v)
            n_kv = (w_qkv.shape[-1] - D_MODEL) // 2
            q, k, v, k32, v32 = _qkv_proj(geo, x, mod, gain, bf(w_qkv[j]), gmat, tile_gain(qn[j], q_scale),
                                          tile_gain(kn[j]), cos, sin, n_q=D_MODEL, n_k=n_kv, n_v=n_kv,
                                          v_hd=HEAD_DIM)
            sink = sink_b[j].astype(F32) if kind == 1 else None
            ctx = (cache_k[:, j].reshape(bs, past, n_kv), cache_v[:, j].reshape(bs, past, n_kv))
            tag = "b" if kind == 1 else "c"
            o_p = _flash(q, k, v, row0=0, n_b=bp, t_len=tp, tq=tp, R=4, inner=True, sink=sink,
                         name=f"attn_{tag}_prompt")
            o_s = _flash(q, k, v, row0=n_p, n_b=bs, t_len=ts, tq=min(256 if kind == 1 else 512, ts), R=4,
                         inner=False, ctx=ctx, sink=sink, window=kind == 1, name=f"attn_{tag}_sample")
            outs[tag + "k"].append(k32.reshape((bp, tp) + k32.shape[1:]))
            outs[tag + "v"].append(v32.reshape((bp, tp) + v32.shape[1:]))
            x = _out_ffn(geo, x, (o_p, o_s), mod, bf(w_o[j]), *ffn_w, split_out=split_out)
        else:
            if isinstance(x, tuple):
                x = jnp.concatenate(x, axis=0)
            qs, lff, lfb, v, gs = _proj_d(geo, x, mod, gain, bf(w_in_d[j]), lb_all[:, li])
            o_f, o_b, fin = _hgrn_scan(geo, tri, qs, lff, lfb, v, jnp.swapaxes(state_d[:, j], -1, -2))
            outs["d"].append(fin)
            x = _out_ffn(geo, x, (o_f, o_b), mod, bf(w_o_d[j]), *ffn_w, gs=gs, gn=gn_d[j][None],
                         split_out=split_out)

    if not isinstance(x, tuple):
        x = (x[:n_p], x[n_p:])
    stack = lambda xs: jnp.stack(xs, axis=1)
    return (x[0].reshape(bp, tp, D_MODEL), x[1].reshape(bs, ts, D_MODEL), stack(outs["ak"]), stack(outs["av"]),
            stack(outs["bk"]), stack(outs["bv"]), stack(outs["ck"]), stack(outs["cv"]), stack(outs["d"]))
```

```python
import functools
import math

import jax
import jax.numpy as jnp
from jax import lax
from jax.experimental import pallas as pl
from jax.experimental.pallas import tpu as pltpu

F32 = jnp.float32
BF16 = jnp.bfloat16

D_MODEL = 1024
HEAD_DIM = 64
GRID_W = 64
ROPE_THETA = 10000.0
WINDOW = 128
EPS = 1e-6
NEG_INF = -1e30
LOG2E = 1.4426950408889634
CHUNK = 64
HGRN_CHUNKS_PER_STEP = 4
D_KDIM = 128
D_HEADS = D_MODEL // D_KDIM
N_MOD = 6
MOD_ROWS = 8
COL_CHUNK = 512
NORM_CHUNK = 256
SUM_ROWS = 16
V7X_VMEM_BYTES = 64 * 1024 * 1024
VMEM_LIMIT = V7X_VMEM_BYTES - 8 * 1024 * 1024


def _dot(a, b):
    return jnp.dot(a, b, preferred_element_type=F32)


def _dot_nt(a, b):
    return lax.dot_general(a, b, (((1,), (1,)), ((), ())), preferred_element_type=F32)


def _dot_tn(a, b):
    return lax.dot_general(a, b, (((0,), (0,)), ((), ())), preferred_element_type=F32)


def _silu(x):
    return x * jax.nn.sigmoid(x)


def _params(n_axes, **kw):
    return pltpu.CompilerParams(dimension_semantics=("arbitrary",) * n_axes, vmem_limit_bytes=VMEM_LIMIT, **kw)


def _const_spec(shape):
    return pl.BlockSpec(shape, lambda *_: (0,) * len(shape), pipeline_mode=pl.Buffered(1))


def _norm_mod(x, gain, shift, scale):
    ms = jnp.mean(x * x, axis=-1, keepdims=True)
    return (x * lax.rsqrt(ms + EPS) * gain) * (1.0 + scale) + shift


def _split3(x):
    hi = x.astype(BF16)
    r = x - hi.astype(F32)
    mid = r.astype(BF16)
    lo = (r - mid.astype(F32)).astype(BF16)
    return hi, mid, lo


def _ada_kernel(cond_ref, w_ref, b_ref, o_ref):
    a = _silu(cond_ref[...]).astype(BF16)
    o_ref[...] = _dot(a, w_ref[...].astype(BF16)) + b_ref[...]


def _ada_all(cond, w_ada, b_ada):
    depth, d, n = w_ada.shape
    tn = n // 4
    return pl.pallas_call(
        _ada_kernel,
        out_shape=jax.ShapeDtypeStruct((depth, MOD_ROWS, n), F32),
        grid=(depth, n // tn),
        in_specs=[
            pl.BlockSpec((MOD_ROWS, d), lambda l, j: (0, 0)),
            pl.BlockSpec((None, d, tn), lambda l, j: (l, 0, j)),
            pl.BlockSpec((None, 1, tn), lambda l, j: (l, 0, j)),
        ],
        out_specs=pl.BlockSpec((None, MOD_ROWS, tn), lambda l, j: (l, 0, j)),
        compiler_params=_params(2),
        name="ada_mod",
    )(cond, w_ada, b_ada.reshape(depth, 1, n))


class _Geom:
    def __init__(self, bp, tp, bs, ts, past):
        self.bp, self.tp, self.bs, self.ts, self.past = bp, tp, bs, ts, past
        self.n_p = bp * tp
        self.n_s = bs * ts
        self.n_t = self.n_p + self.n_s
        self.tm = math.gcd(512, math.gcd(self.n_p, ts))
        assert self.n_p % ts == 0 and ts % CHUNK == 0 and tp % CHUNK == 0
        assert bs + 1 <= MOD_ROWS

    def mod_row(self, i):
        return jnp.where(i * self.tm < self.n_p, 0, 1 + (i * self.tm - self.n_p) // self.ts)

    def prompt_tile(self, i):
        return jnp.minimum(i, self.n_p // self.tm - 1)

    def sample_tile(self, i):
        return jnp.maximum(i - self.n_p // self.tm, 0)

    def rope_tile(self, i):
        per_seq = self.ts // self.tm
        return jnp.where(i * self.tm < self.n_p, per_seq, self.sample_tile(i) % per_seq)

    def row_spec(self, width=D_MODEL):
        return pl.BlockSpec((self.tm, width), lambda i: (i, 0))

    def prompt_spec(self, width=D_MODEL):
        return pl.BlockSpec((self.tm, width), lambda i: (self.prompt_tile(i), 0))

    def sample_spec(self, width=D_MODEL):
        return pl.BlockSpec((self.tm, width), lambda i: (self.sample_tile(i), 0))

    def mod_spec(self):
        return pl.BlockSpec((None, 1, N_MOD * D_MODEL), lambda i: (self.mod_row(i), 0, 0))

    def stream_specs(self, x):
        if isinstance(x, tuple):
            return [self.prompt_spec(), self.sample_spec()], list(x)
        return [self.row_spec()], [x]


def _vec_spec(n):
    return pl.BlockSpec((1, n), lambda i: (0, 0))


def _load_stream(x_refs, is_sample):
    if len(x_refs) == 2:
        return jnp.where(is_sample, x_refs[1][...], x_refs[0][...])
    return x_refs[0][...]


def _qkv_kernel(*refs, tm, n_p, n_x, n_q, n_k, n_v):
    refs = list(refs)
    x_refs = refs[:n_x]
    del refs[:n_x]
    (mod_ref, gain_ref, w_ref, gmat_ref, qg_ref, kg_ref, cos_ref, sin_ref,
     q_ref, k_ref, v_ref, k32_ref, v32_ref, h_scr, k32_scr, v32_scr) = refs
    i = pl.program_id(0)
    is_sample = i * tm >= n_p
    is_prompt = jnp.logical_not(is_sample)
    mod = mod_ref[...]
    h = _norm_mod(_load_stream(x_refs, is_sample), gain_ref[...], mod[:, 0:D_MODEL], mod[:, D_MODEL:2 * D_MODEL])
    h_scr[...] = h.astype(BF16)

    def head_norm(y, gain):
        y2 = (y * y).astype(BF16)
        ms = jnp.concatenate([_dot(y2[:, c:c + NORM_CHUNK], gmat_ref[...])
                              for c in range(0, y.shape[1], NORM_CHUNK)], axis=1) * (1.0 / HEAD_DIM)
        return y * lax.rsqrt(ms + EPS) * gain

    def rope(y):
        w = y.shape[1]
        lane = lax.broadcasted_iota(jnp.int32, y.shape, 1)
        partner = jnp.where(lane % 32 < 16, pltpu.roll(y, w - 16, 1), pltpu.roll(y, 16, 1))
        return y * cos_ref[:, :w] + partner * sin_ref[:, :w]

    def store_qk(y, dst, stage, cs):
        dst[:, cs] = rope(y).astype(BF16)
        if stage is not None:
            stage[:, cs] = y

    def store_v(y, cs):
        v_ref[:, cs] = y.astype(BF16)
        v32_scr[:, cs] = y

    def proj(c0, width):
        return _dot(h_scr[...], w_ref[:, c0:c0 + width])

    for c0 in range(0, n_q, COL_CHUNK):
        store_qk(head_norm(proj(c0, COL_CHUNK), qg_ref[...]), q_ref, None, slice(c0, c0 + COL_CHUNK))
    if n_k + n_v == COL_CHUNK:
        y = proj(n_q, COL_CHUNK)
        store_qk(head_norm(y[:, :n_k], kg_ref[:, :n_k]), k_ref, k32_scr, slice(0, n_k))
        store_v(y[:, n_k:], slice(0, n_v))
    else:
        for c0 in range(0, n_k, COL_CHUNK):
            cs = slice(c0, c0 + COL_CHUNK)
            store_qk(head_norm(proj(n_q + c0, COL_CHUNK), kg_ref[...]), k_ref, k32_scr, cs)
        for c0 in range(0, n_v, COL_CHUNK):
            store_v(proj(n_q + n_k + c0, COL_CHUNK), slice(c0, c0 + COL_CHUNK))

    @pl.when(is_prompt)
    def _():
        k32_ref[...] = k32_scr[...]
        v32_ref[...] = v32_scr[...]


def _qkv_proj(geo, x, mod, gain, w, gmat, q_gain, k_gain, cos, sin, *, n_q, n_k, n_v, v_hd):
    tm = geo.tm
    x_specs, x_args = geo.stream_specs(x)
    rope_spec = pl.BlockSpec((tm, COL_CHUNK), lambda i: (geo.rope_tile(i), 0))
    kern = functools.partial(_qkv_kernel, tm=tm, n_p=geo.n_p, n_x=len(x_args), n_q=n_q, n_k=n_k, n_v=n_v)
    q, k, v, k32, v32 = pl.pallas_call(
        kern,
        out_shape=(
            jax.ShapeDtypeStruct((geo.n_t, n_q), BF16),
            jax.ShapeDtypeStruct((geo.n_t, n_k), BF16),
            jax.ShapeDtypeStruct((geo.n_t, n_v), BF16),
            jax.ShapeDtypeStruct((geo.n_p, n_k), F32),
            jax.ShapeDtypeStruct((geo.n_p, n_v), F32),
        ),
        grid=(geo.n_t // tm,),
        in_specs=x_specs + [
            geo.mod_spec(),
            _vec_spec(D_MODEL),
            _const_spec((D_MODEL, n_q + n_k + n_v)),
            _const_spec((NORM_CHUNK, NORM_CHUNK)),
            _vec_spec(COL_CHUNK),
            _vec_spec(COL_CHUNK),
            rope_spec,
            rope_spec,
        ],
        out_specs=(geo.row_spec(n_q), geo.row_spec(n_k), geo.row_spec(n_v),
                   geo.prompt_spec(n_k), geo.prompt_spec(n_v)),
        scratch_shapes=[pltpu.VMEM((tm, D_MODEL), BF16), pltpu.VMEM((tm, n_k), F32), pltpu.VMEM((tm, n_v), F32)],
        compiler_params=_params(1),
        name="qkv_proj",
    )(*x_args, mod, gain, w, gmat, q_gain, k_gain, cos, sin)
    return q, k, v, k32.reshape(geo.n_p, n_k // HEAD_DIM, HEAD_DIM), v32.reshape(geo.n_p, n_v // v_hd, v_hd)


def _flash_kernel(*refs, R, tq, kb, n_own, t_len, n_inner, hps, has_ctx, use_sink, window, diff, lam_scale):
    refs = list(refs)
    q_ref, k_ref, v_ref = refs[:3]
    del refs[:3]
    if has_ctx:
        ck_ref, cv_ref = refs[:2]
        del refs[:2]
    if use_sink:
        sink_ref = refs.pop(0)
    if diff:
        lam_ref, subg_ref = refs[:2]
        del refs[:2]
    o_ref, qs_scr = refs[:2]
    del refs[:2]
    cached = n_inner == 1
    if cached:
        vt_scr = refs.pop(0)
    if has_ctx:
        ctk_scr, ctvt_scr = refs

    W = HEAD_DIM * R
    M = R * tq
    DV = 2 * HEAD_DIM if diff else HEAD_DIM
    qi = pl.program_id(2)

    def transposed(v):
        return v.astype(F32).T.astype(BF16)

    if cached:
        first = qi == 0 if diff else jnp.logical_and(qi == 0, pl.program_id(1) == 0)

        def fill(dst, vt):
            if diff:
                dst[...] = vt
            else:
                for g in range(vt.shape[0] // DV):
                    dst[g, :DV, :] = vt[g * DV:(g + 1) * DV]
                    dst[g, DV:, :] = jnp.ones((SUM_ROWS, vt.shape[1]), BF16)

        @pl.when(first)
        def _():
            fill(vt_scr, transposed(v_ref[...]))
            if has_ctx:
                p_len = ctk_scr.shape[0]

                def cache_heads(ref, first_head, n_heads):
                    per_key = ref.shape[0] // p_len
                    return jnp.concatenate(
                        [ref[pl.ds(first_head + i, p_len, stride=per_key), :] for i in range(n_heads)], axis=1)

                if diff:
                    h0 = pl.program_id(1) * hps
                    ck, cv = cache_heads(ck_ref, 2 * h0, 2 * hps), cache_heads(cv_ref, h0, hps)
                else:
                    n_kv = ck_ref.shape[0] // p_len
                    ck, cv = cache_heads(ck_ref, 0, n_kv), cache_heads(cv_ref, 0, n_kv)
                ctk_scr[...] = ck.astype(BF16)
                fill(ctvt_scr, cv.T.astype(BF16))
    else:
        vt_all = transposed(v_ref[...])

    def scores(kblk, mask, qs):
        st = _dot_nt(kblk, qs[...])
        if mask is not None:
            st = jnp.where(mask, st, NEG_INF)
        return st

    def update(st, vt, state, sink_row):
        m_new = jnp.max(st, axis=0, keepdims=True)
        if state is not None:
            m_prev, l_prev, acc_prev = state
            m_new = jnp.maximum(m_new, m_prev)
        elif sink_row is not None:
            m_new = jnp.maximum(m_new, sink_row)
        p = jnp.exp2(st - m_new)
        acc = _dot(vt, p.astype(BF16))
        if diff:
            l_new = jnp.sum(p, axis=0, keepdims=True)
        else:
            l_new, acc = acc[DV:DV + 1], acc[:DV]
        if state is not None:
            alpha = jnp.exp2(m_prev - m_new)
            l_new = alpha * l_prev + l_new
            acc = alpha * acc_prev + acc
        elif sink_row is not None:
            l_new = l_new + jnp.exp2(sink_row - m_new)
        return m_new, l_new, acc

    def head(c, i):
        static = not cached
        q_cols = slice(i * W, (i + 1) * W)
        k_cols = slice(i * W, (i + 1) * W) if diff else slice(None)
        v_rows = slice(i * DV, (i + 1) * DV)
        q = q_ref[:, q_cols].astype(F32)
        lane = lax.broadcasted_iota(jnp.int32, q.shape, 1) // HEAD_DIM
        stacked = []
        for r in range(R):
            t = jnp.where(lane == r, q, 0.0)
            if not diff:
                if static:
                    if (c - r) % R:
                        t = pltpu.roll(t, HEAD_DIM * ((c - r) % R), 1)
                else:
                    t = pltpu.roll(t, HEAD_DIM * ((c - r + R) % R), 1)
            stacked.append(t.astype(BF16))
        if static:
            qs = jnp.concatenate(stacked, axis=0)
        else:
            qs = qs_scr.at[i]
            for r in range(R):
                qs[r * tq:(r + 1) * tq, :] = stacked[r]

        def vt_block(off):
            if static:
                vt = vt_all[c * DV:(c + 1) * DV, off:off + kb]
                return vt if diff else jnp.concatenate([vt, jnp.ones((SUM_ROWS, kb), BF16)], axis=0)
            return vt_scr[v_rows, pl.ds(off, kb)] if diff else vt_scr[c, :, pl.ds(off, kb)]

        sink_row = None
        if use_sink:
            sink_row = jnp.concatenate(
                [jnp.full((1, tq), sink_ref[c * R + r] * LOG2E, F32) for r in range(R)], axis=1)

        blocks = []
        if has_ctx:
            blocks.append((lambda: ctk_scr[:, k_cols], lambda: ctvt_scr[v_rows, :] if diff else ctvt_scr[c], None))
        if window:
            q0 = qi * tq
            start = pl.multiple_of(jnp.clip(q0 - WINDOW, 0, t_len - kb), WINDOW)
            kpos = start + lax.broadcasted_iota(jnp.int32, (kb, M), 0)
            qpos = q0 + lax.broadcasted_iota(jnp.int32, (kb, M), 1) % tq
            blocks.append((lambda: k_ref[pl.ds(start, kb), k_cols], lambda: vt_block(start),
                           jnp.abs(qpos - kpos) <= WINDOW))
        else:
            for j in range(n_own):
                blocks.append((lambda j=j: k_ref[j * kb:(j + 1) * kb, k_cols], lambda j=j: vt_block(j * kb), None))

        state = None
        st_next = scores(blocks[0][0](), blocks[0][2], qs)
        yield
        for j, (_, vt_get, _) in enumerate(blocks):
            st = st_next
            if j + 1 < len(blocks):
                st_next = scores(blocks[j + 1][0](), blocks[j + 1][2], qs)
            state = update(st, vt_get(), state, sink_row if j == 0 else None)
            yield

        _, l_fin, acc = state
        a = acc * (1.0 / l_fin)
        if diff:
            ot = a[:, :tq] - lam_ref[0] * a[:, tq:]
            ms = jnp.mean(ot * ot, axis=0, keepdims=True)
            o = (ot * lax.rsqrt(ms + EPS)).T * (subg_ref[...] * lam_scale)
        else:
            o = jnp.concatenate([a[:, r * tq:(r + 1) * tq] for r in range(R)], axis=0).T
        o_ref[:, q_cols] = o.astype(BF16)

    if cached:
        running = [head(pl.program_id(1) * hps + i, i) for i in range(hps)]
    else:
        running = [head(c, c) for c in range(n_inner)]
    while running:
        running = [g for g in running if next(g, "done") != "done"]


def _flash(q, k, v, *, row0, n_b, t_len, tq, R, inner, ctx=None, sink=None, window=False,
           lam=None, subg=None, lam_scale=1.0, hps=1, name):
    diff = lam is not None
    W = HEAD_DIM * R
    w_kv = k.shape[1]
    n_c = D_MODEL // W
    n_qb = t_len // tq
    if window:
        kb, n_own = tq + 2 * WINDOW, 1
    else:
        kb = min(t_len, 512)
        n_own = t_len // kb
    q_base, k_base = row0 // tq, row0 // t_len
    M = R * tq
    assert hps == 1 or not inner
    scratch = [pltpu.VMEM((hps, M, W), BF16)]
    if inner:
        assert n_qb == 1 and n_own == 1 and ctx is None
        grid = (n_b, 1, 1)
        q_spec = pl.BlockSpec((tq, D_MODEL), lambda b, c, qi: (q_base + b, 0))
        kv_spec = pl.BlockSpec((t_len, w_kv), lambda b, c, qi: (k_base + b, 0))
        o_spec = pl.BlockSpec((tq, D_MODEL), lambda b, c, qi: (b, 0))
        n_inner = n_c
    else:
        grid = (n_b, n_c // hps, n_qb)
        q_spec = pl.BlockSpec((tq, W * hps), lambda b, c, qi: (q_base + b * n_qb + qi, c))
        o_spec = pl.BlockSpec((tq, W * hps), lambda b, c, qi: (b * n_qb + qi, c))
        if diff:
            kv_spec = pl.BlockSpec((t_len, W * hps), lambda b, c, qi: (k_base + b, c))
            scratch.append(pltpu.VMEM((W * hps, t_len), BF16))
        else:
            kv_spec = pl.BlockSpec((t_len, w_kv), lambda b, c, qi: (k_base + b, 0))
            scratch.append(pltpu.VMEM((w_kv // HEAD_DIM, HEAD_DIM + SUM_ROWS, t_len), BF16))
        n_inner = 1
    in_specs = [q_spec, kv_spec, kv_spec]
    args = [q, k, v]
    if ctx is not None:
        ck, cv = ctx
        p_len = ck.shape[1] * ck.shape[2] // (D_MODEL if diff else w_kv)
        w_ctx = W * hps if diff else w_kv
        ctvt_shape = (w_ctx, p_len) if diff else (w_ctx // HEAD_DIM, HEAD_DIM + SUM_ROWS, p_len)
        scratch += [pltpu.VMEM((p_len, w_ctx), BF16), pltpu.VMEM(ctvt_shape, BF16)]
        in_specs += [pl.BlockSpec((None,) + a.shape[1:], lambda b, c, qi: (b, 0, 0)) for a in (ck, cv)]
        args += [ck, cv]
    if sink is not None:
        in_specs.append(pl.BlockSpec(memory_space=pltpu.SMEM))
        args.append(sink)
    if diff:
        in_specs += [pl.BlockSpec(memory_space=pltpu.SMEM), pl.BlockSpec((1, W), lambda b, c, qi: (0, 0))]
        args += [lam, subg]
    kern = functools.partial(
        _flash_kernel, R=R, tq=tq, kb=kb, n_own=n_own, t_len=t_len, n_inner=n_inner, hps=hps,
        has_ctx=ctx is not None,
        use_sink=sink is not None, window=window, diff=diff, lam_scale=lam_scale)
    return pl.pallas_call(
        kern,
        out_shape=jax.ShapeDtypeStruct((n_b * t_len, D_MODEL), BF16),
        grid=grid,
        in_specs=in_specs,
        out_specs=o_spec,
        scratch_shapes=scratch,
        compiler_params=_params(3),
        name=name,
    )(*args)


def _proj_d_kernel(x_ref, mod_ref, gain_ref, w_ref, lb_ref, qs_ref, lff_ref, lfb_ref, v_ref, gs_ref, h_scr):
    mod = mod_ref[...]
    h = _norm_mod(x_ref[...], gain_ref[...], mod[:, 0:D_MODEL], mod[:, D_MODEL:2 * D_MODEL])
    h_scr[...] = h.astype(BF16)

    def proj(part, c0):
        return _dot(h_scr[...], w_ref[:, part * D_MODEL + c0:part * D_MODEL + c0 + COL_CHUNK])

    for c0 in range(0, D_MODEL, COL_CHUNK):
        cs = slice(c0, c0 + COL_CHUNK)
        qs_ref[:, cs] = _silu(proj(0, c0))
        for d, ref in ((0, lff_ref), (1, lfb_ref)):
            lb = lb_ref[d:d + 1, cs]
            ref[:, cs] = jnp.log(lb + (1.0 - lb) * jax.nn.sigmoid(proj(1 + d, c0)))
        v_ref[:, cs] = proj(3, c0).astype(BF16)
        gs_ref[:, cs] = _silu(proj(4, c0))


def _proj_d(geo, x, mod, gain, w, lb):
    spec = geo.row_spec()
    f32_out = jax.ShapeDtypeStruct((geo.n_t, D_MODEL), F32)
    return pl.pallas_call(
        _proj_d_kernel,
        out_shape=(f32_out, f32_out, f32_out, jax.ShapeDtypeStruct((geo.n_t, D_MODEL), BF16), f32_out),
        grid=(geo.n_t // geo.tm,),
        in_specs=[spec, geo.mod_spec(), _vec_spec(D_MODEL),
                  _const_spec((D_MODEL, 5 * D_MODEL)), pl.BlockSpec((2, D_MODEL), lambda i: (0, 0))],
        out_specs=(spec,) * 5,
        scratch_shapes=[pltpu.VMEM((geo.tm, D_MODEL), BF16)],
        compiler_params=_params(1),
        name="hgrn_proj",
    )(x, mod, gain, w, lb)


class _ScanPlan:
    def __init__(self, geo, rows):
        self.npp, self.nps = geo.tp // rows, geo.ts // rows
        self.s_p, self.s_s = geo.bp * self.npp, geo.bs * self.nps
        self.bp = geo.bp

    def split(self, s):
        lat = s >= self.s_p
        r = s - self.s_p
        seq = jnp.where(lat, r // self.nps, s // self.npp)
        n = jnp.where(lat, r % self.nps, s % self.npp)
        per = jnp.where(lat, self.nps, self.npp)
        base = jnp.where(lat, self.s_p + seq * self.nps, seq * self.npp)
        return lat, seq, n, per, base

    def fwd_block(self, s):
        _, _, n, _, base = self.split(s)
        return base + n

    def bwd_block(self, s):
        _, _, n, per, base = self.split(s)
        return base + per - 1 - n


def _hgrn_kernel(tri_ref, qf_ref, lff_ref, vf_ref, qb_ref, lfb_ref, vb_ref, s0_ref, of_ref, ob_ref, fin_ref,
                 st_scr, *, plan, cps):
    ins = [(qf_ref, lff_ref, vf_ref), (qb_ref, lfb_ref, vb_ref)]
    o_refs = [of_ref, ob_ref]
    lat, _, n, per, _ = plan.split(pl.program_id(0))

    @pl.when(n == 0)
    def _():
        st_scr[...] = jnp.where(lat, s0_ref[...], 0.0)

    rows = cps * CHUNK
    row = lax.broadcasted_iota(jnp.int32, (rows, rows), 0)
    col = lax.broadcasted_iota(jnp.int32, (rows, rows), 1)
    same_chunk = row // CHUNK == col // CHUNK
    prep = []
    for d in range(2):
        qs_ref, lf_ref, v_ref = ins[d]
        tri = tri_ref[d]
        lf = lf_ref[...]
        cum = sum(_dot(tri, part) for part in _split3(lf))
        q_dec = (qs_ref[...] * jnp.exp(cum)).astype(BF16)
        k_in = (1.0 - jnp.exp(lf)) * jnp.exp(-cum)
        order = list(range(cps)) if d == 0 else list(reversed(range(cps)))
        e_last = [jnp.exp(cum[i * CHUNK + CHUNK - 1:(i + 1) * CHUNK] if d == 0 else cum[i * CHUNK:i * CHUNK + 1])
                  for i in range(cps)]
        k_out = jnp.concatenate([k_in[i * CHUNK:(i + 1) * CHUNK] * e_last[i] for i in range(cps)],
                                axis=0).astype(BF16)
        k_in = k_in.astype(BF16)
        keep = jnp.logical_and(same_chunk, (col <= row) if d == 0 else (col >= row))
        prep.append((q_dec, k_in, k_out, e_last, keep, order, v_ref))

    chunk_rows = [slice(i * CHUNK, (i + 1) * CHUNK) for i in range(cps)]
    heads = [slice(hd * D_KDIM, (hd + 1) * D_KDIM) for hd in range(D_HEADS)]
    for d, (q_dec, k_in, _, _, keep, _, v_ref) in enumerate(prep):
        for hs in heads:
            att = jnp.where(keep, _dot_nt(q_dec[:, hs], k_in[:, hs]), 0.0).astype(BF16)
            o_refs[d][:, hs] = _dot(att, v_ref[:, hs])
    for t in range(cps):
        for d, (q_dec, _, k_out, e_last, _, order, v_ref) in enumerate(prep):
            rs = chunk_rows[order[t]]
            for hd, hs in enumerate(heads):
                st = st_scr[d, hd]
                o_refs[d][rs, hs] += _dot_nt(q_dec[rs, hs], st.astype(BF16))
                st_scr[d, hd] = st * e_last[order[t]][:, hs] + _dot_tn(v_ref[rs, hs], k_out[rs, hs])

    @pl.when(jnp.logical_and(n == per - 1, jnp.logical_not(lat)))
    def _():
        for d in range(2):
            for hd in range(D_HEADS):
                fin_ref[d, hd] = st_scr[d, hd].T


def _hgrn_scan(geo, tri, qs, lff, lfb, v, s0):
    cps = tri.shape[1] // CHUNK
    rows = cps * CHUNK
    plan = _ScanPlan(geo, rows)
    fwd = pl.BlockSpec((rows, D_MODEL), lambda s: (plan.fwd_block(s), 0))
    bwd = pl.BlockSpec((rows, D_MODEL), lambda s: (plan.bwd_block(s), 0))
    st_shape = (2, D_HEADS, D_KDIM, D_KDIM)
    s0_spec = pl.BlockSpec((None,) + st_shape,
                           lambda s: (jnp.where(plan.split(s)[0], plan.split(s)[1], 0), 0, 0, 0, 0))
    fin_spec = pl.BlockSpec((None,) + st_shape,
                            lambda s: (jnp.where(plan.split(s)[0], plan.bp - 1, plan.split(s)[1]), 0, 0, 0, 0))
    o_shape = jax.ShapeDtypeStruct(qs.shape, F32)
    return pl.pallas_call(
        functools.partial(_hgrn_kernel, plan=plan, cps=cps),
        out_shape=(o_shape, o_shape, jax.ShapeDtypeStruct((geo.bp,) + st_shape, F32)),
        grid=(plan.s_p + plan.s_s,),
        in_specs=[pl.BlockSpec((2, rows, rows), lambda s: (0, 0, 0)), fwd, fwd, fwd, bwd, bwd, bwd, s0_spec],
        out_specs=(fwd, bwd, fin_spec),
        scratch_shapes=[pltpu.VMEM(st_shape, F32)],
        compiler_params=_params(1),
        name="hgrn_scan",
    )(tri, qs, lff, v, qs, lfb, v, s0)


def _ffn_chunks(d_ff):
    chunks, f0 = [], 0
    while f0 < d_ff:
        fs = min(COL_CHUNK, d_ff - f0)
        chunks.append((f0, fs))
        f0 += fs
    return tuple(chunks)


def _out_ffn_kernel(*refs, tm, n_p, n_x, hgrn, split_out, f_chunks):
    refs = list(refs)
    x_refs = refs[:n_x]
    del refs[:n_x]
    o_in = refs[:2]
    del refs[:2]
    if hgrn:
        gs_ref, gn_ref = refs[:2]
        del refs[:2]
    mod_ref, wo_ref, ng_ref, wg_ref, wu_ref, wd_ref = refs[:6]
    del refs[:6]
    n_out = 2 if split_out else 1
    out_refs = refs[:n_out]
    x1_scr, h_scr, acc_scr = refs[n_out:]
    is_sample = pl.program_id(0) * tm >= n_p
    mod = mod_ref[...]
    m = [mod[:, j * D_MODEL:(j + 1) * D_MODEL] for j in range(N_MOD)]

    if hgrn:
        o = o_in[0][...] + o_in[1][...]
        parts = []
        for hd in range(D_HEADS):
            hs = slice(hd * D_KDIM, (hd + 1) * D_KDIM)
            oh = o[:, hs]
            ms = jnp.mean(oh * oh, axis=-1, keepdims=True)
            parts.append(oh * lax.rsqrt(ms + EPS) * gn_ref[...] * gs_ref[:, hs])
        o = jnp.concatenate(parts, axis=-1).astype(BF16)
    else:
        o = _load_stream(o_in, is_sample)

    x1 = _load_stream(x_refs, is_sample) + m[2] * _dot(o, wo_ref[...])
    x1_scr[...] = x1
    h_scr[...] = _norm_mod(x1, ng_ref[...], m[3], m[4]).astype(BF16)
    for idx, (f0, fs) in enumerate(f_chunks):
        hb = h_scr[...]
        g = _dot(hb, wg_ref[:, f0:f0 + fs])
        u = _dot(hb, wu_ref[:, f0:f0 + fs])
        y = _dot((_silu(g) * u).astype(BF16), wd_ref[f0:f0 + fs, :])
        acc_scr[...] = y if idx == 0 else acc_scr[...] + y
    res = x1_scr[...] + m[5] * acc_scr[...]
    if split_out:
        @pl.when(jnp.logical_not(is_sample))
        def _():
            out_refs[0][...] = res

        @pl.when(is_sample)
        def _():
            out_refs[1][...] = res
    else:
        out_refs[0][...] = res


def _out_ffn(geo, x, o_parts, mod, w_o, norm_gain, w_gate, w_up, w_down, gs=None, gn=None, split_out=False):
    tm = geo.tm
    hgrn = gs is not None
    d_ff = w_gate.shape[1]
    in_specs, args = geo.stream_specs(x)
    n_x = len(args)
    args += list(o_parts)
    if hgrn:
        in_specs += [geo.row_spec(), geo.row_spec(), geo.row_spec(), _vec_spec(D_KDIM)]
        args += [gs, gn]
    else:
        in_specs += [geo.prompt_spec(), geo.sample_spec()]
    in_specs += [geo.mod_spec(), _const_spec((D_MODEL, D_MODEL)), _vec_spec(D_MODEL),
                 _const_spec((D_MODEL, d_ff)), _const_spec((D_MODEL, d_ff)), _const_spec((d_ff, D_MODEL))]
    args += [mod, w_o, norm_gain, w_gate, w_up, w_down]
    if split_out:
        out_shape = (jax.ShapeDtypeStruct((geo.n_p, D_MODEL), F32), jax.ShapeDtypeStruct((geo.n_s, D_MODEL), F32))
        out_specs = (geo.prompt_spec(), geo.sample_spec())
    else:
        out_shape = jax.ShapeDtypeStruct((geo.n_t, D_MODEL), F32)
        out_specs = geo.row_spec()
    kern = functools.partial(_out_ffn_kernel, tm=tm, n_p=geo.n_p, n_x=n_x, hgrn=hgrn, split_out=split_out,
                             f_chunks=_ffn_chunks(d_ff))
    return pl.pallas_call(
        kern,
        out_shape=out_shape,
        grid=(geo.n_t // tm,),
        in_specs=in_specs,
        out_specs=out_specs,
        scratch_shapes=[pltpu.VMEM((tm, D_MODEL), F32), pltpu.VMEM((tm, D_MODEL), BF16),
                        pltpu.VMEM((tm, D_MODEL), F32)],
        compiler_params=_params(1),
        name="out_ffn",
    )(*args)


def _rope_tables(n_tokens, n_identity):
    rows = n_tokens // GRID_W
    row = jnp.repeat(jnp.arange(rows, dtype=F32), GRID_W)
    col = jnp.tile(jnp.arange(GRID_W, dtype=F32), rows)
    axis_dim = HEAD_DIM // 2
    inv_freq = ROPE_THETA ** (-jnp.arange(0, axis_dim, 2, dtype=F32) / axis_dim)
    ang = jnp.stack([row[:, None] * inv_freq, col[:, None] * inv_freq], axis=1)
    cos, sin = jnp.cos(ang), jnp.sin(ang)
    cos64 = jnp.concatenate([cos, cos], axis=-1).reshape(n_tokens, HEAD_DIM)
    sin64 = jnp.concatenate([-sin, sin], axis=-1).reshape(n_tokens, HEAD_DIM)
    reps = COL_CHUNK // HEAD_DIM
    ident = jnp.ones((n_identity, COL_CHUNK), F32)
    return (jnp.concatenate([jnp.tile(cos64, (1, reps)), ident], axis=0),
            jnp.concatenate([jnp.tile(sin64, (1, reps)), 0.0 * ident], axis=0))


def _group_matrix():
    g = jnp.arange(NORM_CHUNK) // HEAD_DIM
    return (g[:, None] == g[None, :]).astype(BF16)


def _tri_matrices(cps):
    r = jnp.arange(cps * CHUNK)
    same = r[None, :] // CHUNK == r[:, None] // CHUNK
    lower = jnp.logical_and(same, r[None, :] <= r[:, None]).astype(BF16)
    return jnp.stack([lower, lower.T])


def kernel(x_prompt, x_sample, c, cache_a_k, cache_a_v, cache_b_k, cache_b_v, cache_c_k, cache_c_v, state_d, c_ctx, norm_mix, norm_ffn, w_ada, b_ada, w_ffn_gate, w_ffn_up, w_ffn_down, w_qkv_a, w_o_a, qn_a, kn_a, subln_a, lam_q1_a, lam_k1_a, lam_q2_a, lam_k2_a, w_qkv_b, w_o_b, qn_b, kn_b, sink_b, w_qkv_c, w_o_c, qn_c, kn_c, w_in_d, w_o_d, gn_d, lb_logits_d):
    bp, tp, _ = x_prompt.shape
    bs, ts, _ = x_sample.shape
    past = cache_a_k.shape[2]
    depth = w_ada.shape[0]
    geo = _Geom(bp, tp, bs, ts, past)
    n_p = geo.n_p

    cond = jnp.concatenate([c_ctx[None], c, jnp.zeros((MOD_ROWS - 1 - bs, D_MODEL), F32)], axis=0)
    mod_all = _ada_all(cond, w_ada, b_ada).reshape(depth, MOD_ROWS, 1, N_MOD * D_MODEL)

    cos, sin = _rope_tables(ts, geo.tm)
    gmat = _group_matrix()
    tri = _tri_matrices(math.gcd(HGRN_CHUNKS_PER_STEP, math.gcd(tp // CHUNK, ts // CHUNK)))
    p_lb = jax.nn.softmax(lb_logits_d.astype(F32), axis=1)
    lb_all = jnp.cumsum(p_lb, axis=1) - p_lb[:, :1]
    q_scale = HEAD_DIM ** -0.5 * LOG2E
    tile_gain = lambda g, s=1.0: jnp.tile(g * s, COL_CHUNK // HEAD_DIM)[None]
    bf = lambda w: w.astype(BF16)

    x = (x_prompt.reshape(n_p, D_MODEL), x_sample.reshape(geo.n_s, D_MODEL))
    outs = {k: [] for k in ("ak", "av", "bk", "bv", "ck", "cv", "d")}
    for li in range(depth):
        kind, j = li % 4, li // 4
        mod = mod_all[li]
        gain = norm_mix[li][None]
        ffn_w = (norm_ffn[li][None], bf(w_ffn_gate[li]), bf(w_ffn_up[li]), bf(w_ffn_down[li]))
        split_out = li == depth - 1
        if kind == 0:
            lam_init = 0.8 - 0.6 * math.exp(-0.3 * li)
            lam = (jnp.exp(jnp.sum(lam_q1_a[j] * lam_k1_a[j])) - jnp.exp(jnp.sum(lam_q2_a[j] * lam_k2_a[j]))
                   + lam_init).astype(F32).reshape(1)
            q, k, v, k32, v32 = _qkv_proj(geo, x, mod, gain, bf(w_qkv_a[j]), gmat, tile_gain(qn_a[j], q_scale),
                                          tile_gain(kn_a[j]), cos, sin, n_q=D_MODEL, n_k=D_MODEL, n_v=D_MODEL,
                                          v_hd=2 * HEAD_DIM)
            common = dict(R=2, lam=lam, subg=subln_a[j][None], lam_scale=1.0 - lam_init)
            o_p = _flash(q, k, v, row0=0, n_b=bp, t_len=tp, tq=tp, inner=True, name="attn_a_prompt", **common)
            ctx = (cache_a_k[:, j].reshape(bs, -1, HEAD_DIM), cache_a_v[:, j].reshape(bs, -1, 2 * HEAD_DIM))
            o_s = _flash(q, k, v, row0=n_p, n_b=bs, t_len=ts, tq=min(1024, ts), inner=False, ctx=ctx,
                         name="attn_a_sample", **common)
            outs["ak"].append(k32.reshape((bp, tp) + k32.shape[1:]))
            outs["av"].append(v32.reshape((bp, tp) + v32.shape[1:]))
            x = _out_ffn(geo, x, (o_p, o_s), mod, bf(w_o_a[j]), *ffn_w, split_out=split_out)
        elif kind in (1, 2):
            w_qkv, w_o, qn, kn = (w_qkv_b, w_o_b, qn_b, kn_b) if kind == 1 else (w_qkv_c, w_o_c, qn_c, kn_c)
            cache_k, cache_v = (cache_b_k, cache_b_v) if kind == 1 else (cache_c_k, cache_c_v)
            n_kv = (w_qkv.shape[-1] - D_MODEL) // 2
            q, k, v, k32, v32 = _qkv_proj(geo, x, mod, gain, bf(w_qkv[j]), gmat, tile_gain(qn[j], q_scale),
                                          tile_gain(kn[j]), cos, sin, n_q=D_MODEL, n_k=n_kv, n_v=n_kv,
                                          v_hd=HEAD_DIM)
            sink = sink_b[j].astype(F32) if kind == 1 else None
            ctx = (cache_k[:, j].reshape(bs, -1, HEAD_DIM), cache_v[:, j].reshape(bs, -1, HEAD_DIM))
            tag = "b" if kind == 1 else "c"
            o_p = _flash(q, k, v, row0=0, n_b=bp, t_len=tp, tq=tp, R=4, inner=True, sink=sink,
                         name=f"attn_{tag}_prompt")
            o_s = _flash(q, k, v, row0=n_p, n_b=bs, t_len=ts, tq=min(256, ts), R=4,
                         inner=False, ctx=ctx, sink=sink, window=kind == 1, hps=4, name=f"attn_{tag}_sample")
            outs[tag + "k"].append(k32.reshape((bp, tp) + k32.shape[1:]))
            outs[tag + "v"].append(v32.reshape((bp, tp) + v32.shape[1:]))
            x = _out_ffn(geo, x, (o_p, o_s), mod, bf(w_o[j]), *ffn_w, split_out=split_out)
        else:
            if isinstance(x, tuple):
                x = jnp.concatenate(x, axis=0)
            qs, lff, lfb, v, gs = _proj_d(geo, x, mod, gain, bf(w_in_d[j]), lb_all[:, li])
            o_f, o_b, fin = _hgrn_scan(geo, tri, qs, lff, lfb, v, jnp.swapaxes(state_d[:, j], -1, -2))
            outs["d"].append(fin)
            x = _out_ffn(geo, x, (o_f, o_b), mod, bf(w_o_d[j]), *ffn_w, gs=gs, gn=gn_d[j][None],
                         split_out=split_out)

    if not isinstance(x, tuple):
        x = (x[:n_p], x[n_p:])
    stack = lambda xs: jnp.stack(xs, axis=1)
    return (x[0].reshape(bp, tp, D_MODEL), x[1].reshape(bs, ts, D_MODEL), stack(outs["ak"]), stack(outs["av"]),
            stack(outs["bk"]), stack(outs["bv"]), stack(outs["ck"]), stack(outs["cv"]), stack(outs["d"]))
```

```python
import functools
import math

import jax
import jax.numpy as jnp
from jax import lax
from jax.experimental import pallas as pl
from jax.experimental.pallas import tpu as pltpu

F32 = jnp.float32
BF16 = jnp.bfloat16

D_MODEL = 1024
HEAD_DIM = 64
GRID_W = 64
ROPE_THETA = 10000.0
WINDOW = 128
EPS = 1e-6
NEG_INF = -1e30
LOG2E = 1.4426950408889634
CHUNK = 64
HGRN_CHUNKS_PER_STEP = 4
D_KDIM = 128
D_HEADS = D_MODEL // D_KDIM
N_MOD = 6
MOD_ROWS = 8
COL_CHUNK = 512
NORM_CHUNK = 256
SUM_ROWS = 16
V7X_VMEM_BYTES = 64 * 1024 * 1024
VMEM_LIMIT = V7X_VMEM_BYTES - 8 * 1024 * 1024


def _dot(a, b):
    return jnp.dot(a, b, preferred_element_type=F32)


def _dot_nt(a, b):
    return lax.dot_general(a, b, (((1,), (1,)), ((), ())), preferred_element_type=F32)


def _dot_tn(a, b):
    return lax.dot_general(a, b, (((0,), (0,)), ((), ())), preferred_element_type=F32)


def _silu(x):
    return x * jax.nn.sigmoid(x)


def _params(n_axes, **kw):
    return pltpu.CompilerParams(dimension_semantics=("arbitrary",) * n_axes, vmem_limit_bytes=VMEM_LIMIT, **kw)


def _const_spec(shape):
    return pl.BlockSpec(shape, lambda *_: (0,) * len(shape), pipeline_mode=pl.Buffered(1))


def _layer_spec(shape, layer):
    return pl.BlockSpec((None,) + shape, lambda *_: (layer, 0, 0), pipeline_mode=pl.Buffered(1))


def _norm_mod(x, gain, shift, scale):
    ms = jnp.mean(x * x, axis=-1, keepdims=True)
    return (x * lax.rsqrt(ms + EPS) * gain) * (1.0 + scale) + shift


def _split3(x):
    hi = x.astype(BF16)
    r = x - hi.astype(F32)
    mid = r.astype(BF16)
    lo = (r - mid.astype(F32)).astype(BF16)
    return hi, mid, lo


def _ada_kernel(cond_ref, w_ref, b_ref, o_ref):
    a = _silu(cond_ref[...]).astype(BF16)
    o_ref[...] = _dot(a, w_ref[...].astype(BF16)) + b_ref[...]


def _ada_all(cond, w_ada, b_ada):
    depth, d, n = w_ada.shape
    tn = n // 4
    return pl.pallas_call(
        _ada_kernel,
        out_shape=jax.ShapeDtypeStruct((depth, MOD_ROWS, n), F32),
        grid=(depth, n // tn),
        in_specs=[
            pl.BlockSpec((MOD_ROWS, d), lambda l, j: (0, 0)),
            pl.BlockSpec((None, d, tn), lambda l, j: (l, 0, j)),
            pl.BlockSpec((None, 1, tn), lambda l, j: (l, 0, j)),
        ],
        out_specs=pl.BlockSpec((None, MOD_ROWS, tn), lambda l, j: (l, 0, j)),
        compiler_params=_params(2),
        name="ada_mod",
    )(cond, w_ada, b_ada.reshape(depth, 1, n))


class _Geom:
    def __init__(self, bp, tp, bs, ts, past):
        self.bp, self.tp, self.bs, self.ts, self.past = bp, tp, bs, ts, past
        self.n_p = bp * tp
        self.n_s = bs * ts
        self.n_t = self.n_p + self.n_s
        self.tm = math.gcd(512, math.gcd(self.n_p, ts))
        assert self.n_p % ts == 0 and ts % CHUNK == 0 and tp % CHUNK == 0
        assert bs + 1 <= MOD_ROWS

    def mod_row(self, i):
        return jnp.where(i * self.tm < self.n_p, 0, 1 + (i * self.tm - self.n_p) // self.ts)

    def prompt_tile(self, i):
        return jnp.minimum(i, self.n_p // self.tm - 1)

    def sample_tile(self, i):
        return jnp.maximum(i - self.n_p // self.tm, 0)

    def rope_tile(self, i):
        per_seq = self.ts // self.tm
        return jnp.where(i * self.tm < self.n_p, per_seq, self.sample_tile(i) % per_seq)

    def row_spec(self, width=D_MODEL):
        return pl.BlockSpec((self.tm, width), lambda i: (i, 0))

    def prompt_spec(self, width=D_MODEL):
        return pl.BlockSpec((self.tm, width), lambda i: (self.prompt_tile(i), 0))

    def sample_spec(self, width=D_MODEL):
        return pl.BlockSpec((self.tm, width), lambda i: (self.sample_tile(i), 0))

    def mod_spec(self):
        return pl.BlockSpec((None, 1, N_MOD * D_MODEL), lambda i: (self.mod_row(i), 0, 0))

    def stream_specs(self, x):
        if isinstance(x, tuple):
            return [self.prompt_spec(), self.sample_spec()], list(x)
        return [self.row_spec()], [x]


def _vec_spec(n):
    return pl.BlockSpec((1, n), lambda i: (0, 0))


def _load_stream(x_refs, is_sample):
    if len(x_refs) == 2:
        return jnp.where(is_sample, x_refs[1][...], x_refs[0][...])
    return x_refs[0][...]


def _qkv_kernel(*refs, tm, n_p, n_x, n_q, n_k, n_v):
    refs = list(refs)
    x_refs = refs[:n_x]
    del refs[:n_x]
    (mod_ref, gain_ref, w_ref, gmat_ref, qg_ref, kg_ref, cos_ref, sin_ref,
     q_ref, k_ref, v_ref, k32_ref, v32_ref, h_scr, k32_scr, v32_scr) = refs
    i = pl.program_id(0)
    is_sample = i * tm >= n_p
    is_prompt = jnp.logical_not(is_sample)
    mod = mod_ref[...]
    h = _norm_mod(_load_stream(x_refs, is_sample), gain_ref[...], mod[:, 0:D_MODEL], mod[:, D_MODEL:2 * D_MODEL])
    h_scr[...] = h.astype(BF16)

    def head_norm(y, gain):
        y2 = (y * y).astype(BF16)
        ms = jnp.concatenate([_dot(y2[:, c:c + NORM_CHUNK], gmat_ref[...])
                              for c in range(0, y.shape[1], NORM_CHUNK)], axis=1) * (1.0 / HEAD_DIM)
        return y * lax.rsqrt(ms + EPS) * gain

    def rope(y):
        w = y.shape[1]
        lane = lax.broadcasted_iota(jnp.int32, y.shape, 1)
        partner = jnp.where(lane % 32 < 16, pltpu.roll(y, w - 16, 1), pltpu.roll(y, 16, 1))
        return y * cos_ref[:, :w] + partner * sin_ref[:, :w]

    def store_qk(y, dst, stage, cs):
        dst[:, cs] = rope(y).astype(BF16)
        if stage is not None:
            stage[:, cs] = y

    def store_v(y, cs):
        v_ref[:, cs] = y.astype(BF16)
        v32_scr[:, cs] = y

    def proj(c0, width):
        return _dot(h_scr[...], w_ref[:, c0:c0 + width])

    for c0 in range(0, n_q, COL_CHUNK):
        store_qk(head_norm(proj(c0, COL_CHUNK), qg_ref[...]), q_ref, None, slice(c0, c0 + COL_CHUNK))
    if n_k + n_v == COL_CHUNK:
        y = proj(n_q, COL_CHUNK)
        store_qk(head_norm(y[:, :n_k], kg_ref[:, :n_k]), k_ref, k32_scr, slice(0, n_k))
        store_v(y[:, n_k:], slice(0, n_v))
    else:
        for c0 in range(0, n_k, COL_CHUNK):
            cs = slice(c0, c0 + COL_CHUNK)
            store_qk(head_norm(proj(n_q + c0, COL_CHUNK), kg_ref[...]), k_ref, k32_scr, cs)
        for c0 in range(0, n_v, COL_CHUNK):
            store_v(proj(n_q + n_k + c0, COL_CHUNK), slice(c0, c0 + COL_CHUNK))

    @pl.when(is_prompt)
    def _():
        k32_ref[...] = k32_scr[...]
        v32_ref[...] = v32_scr[...]


def _qkv_proj(geo, x, mod, gain, w, gmat, q_gain, k_gain, cos, sin, *, n_q, n_k, n_v, v_hd):
    tm = geo.tm
    x_specs, x_args = geo.stream_specs(x)
    rope_spec = pl.BlockSpec((tm, COL_CHUNK), lambda i: (geo.rope_tile(i), 0))
    kern = functools.partial(_qkv_kernel, tm=tm, n_p=geo.n_p, n_x=len(x_args), n_q=n_q, n_k=n_k, n_v=n_v)
    q, k, v, k32, v32 = pl.pallas_call(
        kern,
        out_shape=(
            jax.ShapeDtypeStruct((geo.n_t, n_q), BF16),
            jax.ShapeDtypeStruct((geo.n_t, n_k), BF16),
            jax.ShapeDtypeStruct((geo.n_t, n_v), BF16),
            jax.ShapeDtypeStruct((geo.n_p, n_k), F32),
            jax.ShapeDtypeStruct((geo.n_p, n_v), F32),
        ),
        grid=(geo.n_t // tm,),
        in_specs=x_specs + [
            geo.mod_spec(),
            _vec_spec(D_MODEL),
            _const_spec((D_MODEL, n_q + n_k + n_v)),
            _const_spec((NORM_CHUNK, NORM_CHUNK)),
            _vec_spec(COL_CHUNK),
            _vec_spec(COL_CHUNK),
            rope_spec,
            rope_spec,
        ],
        out_specs=(geo.row_spec(n_q), geo.row_spec(n_k), geo.row_spec(n_v),
                   geo.prompt_spec(n_k), geo.prompt_spec(n_v)),
        scratch_shapes=[pltpu.VMEM((tm, D_MODEL), BF16), pltpu.VMEM((tm, n_k), F32), pltpu.VMEM((tm, n_v), F32)],
        compiler_params=_params(1),
        name="qkv_proj",
    )(*x_args, mod, gain, w, gmat, q_gain, k_gain, cos, sin)
    return q, k, v, k32.reshape(geo.n_p, n_k // HEAD_DIM, HEAD_DIM), v32.reshape(geo.n_p, n_v // v_hd, v_hd)


def _flash_kernel(*refs, R, tq, kb, n_own, t_len, n_inner, hps, has_ctx, use_sink, window, diff, lam_scale):
    refs = list(refs)
    q_ref, k_ref, v_ref = refs[:3]
    del refs[:3]
    if has_ctx:
        ck_ref, cv_ref = refs[:2]
        del refs[:2]
    if use_sink:
        sink_ref = refs.pop(0)
    if diff:
        lam_ref, subg_ref = refs[:2]
        del refs[:2]
    o_ref, qs_scr = refs[:2]
    del refs[:2]
    cached = n_inner == 1
    if cached:
        vt_scr = refs.pop(0)
    if has_ctx:
        ctk_scr, ctvt_scr = refs

    W = HEAD_DIM * R
    M = R * tq
    DV = 2 * HEAD_DIM if diff else HEAD_DIM
    qi = pl.program_id(2)

    def transposed(v):
        return v.astype(F32).T.astype(BF16)

    if cached:
        first = qi == 0 if diff else jnp.logical_and(qi == 0, pl.program_id(1) == 0)

        def fill(dst, vt):
            if diff:
                dst[...] = vt
            else:
                for g in range(vt.shape[0] // DV):
                    dst[g, :DV, :] = vt[g * DV:(g + 1) * DV]
                    dst[g, DV:, :] = jnp.ones((SUM_ROWS, vt.shape[1]), BF16)

        @pl.when(first)
        def _():
            fill(vt_scr, transposed(v_ref[...]))
            if has_ctx:
                ctk_scr[...] = ck_ref[...].astype(BF16)
                fill(ctvt_scr, cv_ref[...].T.astype(BF16))
    else:
        vt_all = transposed(v_ref[...])

    def scores(kblk, mask, qs):
        st = _dot_nt(kblk, qs[...])
        if mask is not None:
            st = jnp.where(mask, st, NEG_INF)
        return st

    def update(st, vt, state, sink_row):
        m_new = jnp.max(st, axis=0, keepdims=True)
        if state is not None:
            m_prev, l_prev, acc_prev = state
            m_new = jnp.maximum(m_new, m_prev)
        elif sink_row is not None:
            m_new = jnp.maximum(m_new, sink_row)
        p = jnp.exp2(st - m_new)
        acc = _dot(vt, p.astype(BF16))
        if diff:
            l_new = jnp.sum(p, axis=0, keepdims=True)
        else:
            l_new, acc = acc[DV:DV + 1], acc[:DV]
        if state is not None:
            alpha = jnp.exp2(m_prev - m_new)
            l_new = alpha * l_prev + l_new
            acc = alpha * acc_prev + acc
        elif sink_row is not None:
            l_new = l_new + jnp.exp2(sink_row - m_new)
        return m_new, l_new, acc

    def head(c, i):
        static = not cached
        q_cols = slice(i * W, (i + 1) * W)
        k_cols = slice(i * W, (i + 1) * W) if diff else slice(None)
        v_rows = slice(i * DV, (i + 1) * DV)
        q = q_ref[:, q_cols].astype(F32)
        lane = lax.broadcasted_iota(jnp.int32, q.shape, 1) // HEAD_DIM
        stacked = []
        for r in range(R):
            t = jnp.where(lane == r, q, 0.0)
            if not diff:
                if static:
                    if (c - r) % R:
                        t = pltpu.roll(t, HEAD_DIM * ((c - r) % R), 1)
                else:
                    t = pltpu.roll(t, HEAD_DIM * ((c - r + R) % R), 1)
            stacked.append(t.astype(BF16))
        if static:
            qs = jnp.concatenate(stacked, axis=0)
        else:
            qs = qs_scr.at[i]
            for r in range(R):
                qs[r * tq:(r + 1) * tq, :] = stacked[r]

        def vt_block(off):
            if static:
                vt = vt_all[c * DV:(c + 1) * DV, off:off + kb]
                return vt if diff else jnp.concatenate([vt, jnp.ones((SUM_ROWS, kb), BF16)], axis=0)
            return vt_scr[v_rows, pl.ds(off, kb)] if diff else vt_scr[c, :, pl.ds(off, kb)]

        sink_row = None
        if use_sink:
            sink_row = jnp.concatenate(
                [jnp.full((1, tq), sink_ref[c * R + r] * LOG2E, F32) for r in range(R)], axis=1)

        blocks = []
        if has_ctx:
            blocks.append((lambda: ctk_scr[:, k_cols], lambda: ctvt_scr[v_rows, :] if diff else ctvt_scr[c], None))
        if window:
            q0 = qi * tq
            start = pl.multiple_of(jnp.clip(q0 - WINDOW, 0, t_len - kb), WINDOW)
            kpos = start + lax.broadcasted_iota(jnp.int32, (kb, M), 0)
            qpos = q0 + lax.broadcasted_iota(jnp.int32, (kb, M), 1) % tq
            blocks.append((lambda: k_ref[pl.ds(start, kb), k_cols], lambda: vt_block(start),
                           jnp.abs(qpos - kpos) <= WINDOW))
        else:
            for j in range(n_own):
                blocks.append((lambda j=j: k_ref[j * kb:(j + 1) * kb, k_cols], lambda j=j: vt_block(j * kb), None))

        state = None
        st_next = scores(blocks[0][0](), blocks[0][2], qs)
        yield
        for j, (_, vt_get, _) in enumerate(blocks):
            st = st_next
            if j + 1 < len(blocks):
                st_next = scores(blocks[j + 1][0](), blocks[j + 1][2], qs)
            state = update(st, vt_get(), state, sink_row if j == 0 else None)
            yield

        _, l_fin, acc = state
        a = acc * (1.0 / l_fin)
        if diff:
            ot = a[:, :tq] - lam_ref[0] * a[:, tq:]
            ms = jnp.mean(ot * ot, axis=0, keepdims=True)
            o = (ot * lax.rsqrt(ms + EPS)).T * (subg_ref[...] * lam_scale)
        else:
            o = jnp.concatenate([a[:, r * tq:(r + 1) * tq] for r in range(R)], axis=0).T
        o_ref[:, q_cols] = o.astype(BF16)

    if cached:
        running = [head(pl.program_id(1) * hps + i, i) for i in range(hps)]
    else:
        running = [head(c, c) for c in range(n_inner)]
    while running:
        running = [g for g in running if next(g, "done") != "done"]


def _flash(q, k, v, *, row0, n_b, t_len, tq, R, inner, ctx=None, sink=None, window=False,
           lam=None, subg=None, lam_scale=1.0, hps=1, name):
    diff = lam is not None
    W = HEAD_DIM * R
    w_kv = k.shape[1]
    n_c = D_MODEL // W
    n_qb = t_len // tq
    if window:
        kb, n_own = tq + 2 * WINDOW, 1
    else:
        kb = min(t_len, 512)
        n_own = t_len // kb
    q_base, k_base = row0 // tq, row0 // t_len
    M = R * tq
    assert hps == 1 or not inner
    scratch = [pltpu.VMEM((hps, M, W), BF16)]
    if inner:
        assert n_qb == 1 and n_own == 1 and ctx is None
        grid = (n_b, 1, 1)
        q_spec = pl.BlockSpec((tq, D_MODEL), lambda b, c, qi: (q_base + b, 0))
        kv_spec = pl.BlockSpec((t_len, w_kv), lambda b, c, qi: (k_base + b, 0))
        o_spec = pl.BlockSpec((tq, D_MODEL), lambda b, c, qi: (b, 0))
        n_inner = n_c
    else:
        grid = (n_b, n_c // hps, n_qb)
        q_spec = pl.BlockSpec((tq, W * hps), lambda b, c, qi: (q_base + b * n_qb + qi, c))
        o_spec = pl.BlockSpec((tq, W * hps), lambda b, c, qi: (b * n_qb + qi, c))
        if diff:
            kv_spec = pl.BlockSpec((t_len, W * hps), lambda b, c, qi: (k_base + b, c))
            scratch.append(pltpu.VMEM((W * hps, t_len), BF16))
        else:
            kv_spec = pl.BlockSpec((t_len, w_kv), lambda b, c, qi: (k_base + b, 0))
            scratch.append(pltpu.VMEM((w_kv // HEAD_DIM, HEAD_DIM + SUM_ROWS, t_len), BF16))
        n_inner = 1
    in_specs = [q_spec, kv_spec, kv_spec]
    args = [q, k, v]
    if ctx is not None:
        ck, cv = ctx
        p_len = ck.shape[1]
        w_ctx = W * hps if diff else w_kv
        cmap = (lambda b, c, qi: (b, 0, c)) if diff else (lambda b, c, qi: (b, 0, 0))
        ctvt_shape = (w_ctx, p_len) if diff else (w_ctx // HEAD_DIM, HEAD_DIM + SUM_ROWS, p_len)
        scratch += [pltpu.VMEM((p_len, w_ctx), BF16), pltpu.VMEM(ctvt_shape, BF16)]
        in_specs += [pl.BlockSpec((None, p_len, w_ctx), cmap)] * 2
        args += [ck, cv]
    if sink is not None:
        in_specs.append(pl.BlockSpec(memory_space=pltpu.SMEM))
        args.append(sink)
    if diff:
        in_specs += [pl.BlockSpec(memory_space=pltpu.SMEM), pl.BlockSpec((1, W), lambda b, c, qi: (0, 0))]
        args += [lam, subg]
    kern = functools.partial(
        _flash_kernel, R=R, tq=tq, kb=kb, n_own=n_own, t_len=t_len, n_inner=n_inner, hps=hps,
        has_ctx=ctx is not None,
        use_sink=sink is not None, window=window, diff=diff, lam_scale=lam_scale)
    return pl.pallas_call(
        kern,
        out_shape=jax.ShapeDtypeStruct((n_b * t_len, D_MODEL), BF16),
        grid=grid,
        in_specs=in_specs,
        out_specs=o_spec,
        scratch_shapes=scratch,
        compiler_params=_params(3),
        name=name,
    )(*args)


def _proj_d_kernel(x_ref, mod_ref, gain_ref, w_ref, lb_ref, qs_ref, lff_ref, lfb_ref, v_ref, gs_ref, h_scr):
    mod = mod_ref[...]
    h = _norm_mod(x_ref[...], gain_ref[...], mod[:, 0:D_MODEL], mod[:, D_MODEL:2 * D_MODEL])
    h_scr[...] = h.astype(BF16)

    def proj(part, c0):
        return _dot(h_scr[...], w_ref[:, part * D_MODEL + c0:part * D_MODEL + c0 + COL_CHUNK])

    for c0 in range(0, D_MODEL, COL_CHUNK):
        cs = slice(c0, c0 + COL_CHUNK)
        qs_ref[:, cs] = _silu(proj(0, c0))
        for d, ref in ((0, lff_ref), (1, lfb_ref)):
            lb = lb_ref[d:d + 1, cs]
            ref[:, cs] = jnp.log(lb + (1.0 - lb) * jax.nn.sigmoid(proj(1 + d, c0)))
        v_ref[:, cs] = proj(3, c0).astype(BF16)
        gs_ref[:, cs] = _silu(proj(4, c0))


def _proj_d(geo, x, mod, gain, w, lb):
    spec = geo.row_spec()
    f32_out = jax.ShapeDtypeStruct((geo.n_t, D_MODEL), F32)
    return pl.pallas_call(
        _proj_d_kernel,
        out_shape=(f32_out, f32_out, f32_out, jax.ShapeDtypeStruct((geo.n_t, D_MODEL), BF16), f32_out),
        grid=(geo.n_t // geo.tm,),
        in_specs=[spec, geo.mod_spec(), _vec_spec(D_MODEL),
                  _const_spec((D_MODEL, 5 * D_MODEL)), pl.BlockSpec((2, D_MODEL), lambda i: (0, 0))],
        out_specs=(spec,) * 5,
        scratch_shapes=[pltpu.VMEM((geo.tm, D_MODEL), BF16)],
        compiler_params=_params(1),
        name="hgrn_proj",
    )(x, mod, gain, w, lb)


class _ScanPlan:
    def __init__(self, geo, rows):
        self.npp, self.nps = geo.tp // rows, geo.ts // rows
        self.s_p, self.s_s = geo.bp * self.npp, geo.bs * self.nps
        self.bp = geo.bp

    def split(self, s):
        lat = s >= self.s_p
        r = s - self.s_p
        seq = jnp.where(lat, r // self.nps, s // self.npp)
        n = jnp.where(lat, r % self.nps, s % self.npp)
        per = jnp.where(lat, self.nps, self.npp)
        base = jnp.where(lat, self.s_p + seq * self.nps, seq * self.npp)
        return lat, seq, n, per, base

    def fwd_block(self, s):
        _, _, n, _, base = self.split(s)
        return base + n

    def bwd_block(self, s):
        _, _, n, per, base = self.split(s)
        return base + per - 1 - n


def _hgrn_kernel(tri_ref, qf_ref, lff_ref, vf_ref, qb_ref, lfb_ref, vb_ref, s0_ref, of_ref, ob_ref, fin_ref,
                 st_scr, *, plan, cps):
    ins = [(qf_ref, lff_ref, vf_ref), (qb_ref, lfb_ref, vb_ref)]
    o_refs = [of_ref, ob_ref]
    lat, _, n, per, _ = plan.split(pl.program_id(0))

    @pl.when(n == 0)
    def _():
        st_scr[...] = jnp.where(lat, s0_ref[...], 0.0)

    rows = cps * CHUNK
    row = lax.broadcasted_iota(jnp.int32, (rows, rows), 0)
    col = lax.broadcasted_iota(jnp.int32, (rows, rows), 1)
    same_chunk = row // CHUNK == col // CHUNK
    prep = []
    for d in range(2):
        qs_ref, lf_ref, v_ref = ins[d]
        tri = tri_ref[d]
        lf = lf_ref[...]
        cum = sum(_dot(tri, part) for part in _split3(lf))
        q_dec = (qs_ref[...] * jnp.exp(cum)).astype(BF16)
        k_in = (1.0 - jnp.exp(lf)) * jnp.exp(-cum)
        order = list(range(cps)) if d == 0 else list(reversed(range(cps)))
        e_last = [jnp.exp(cum[i * CHUNK + CHUNK - 1:(i + 1) * CHUNK] if d == 0 else cum[i * CHUNK:i * CHUNK + 1])
                  for i in range(cps)]
        k_out = jnp.concatenate([k_in[i * CHUNK:(i + 1) * CHUNK] * e_last[i] for i in range(cps)],
                                axis=0).astype(BF16)
        k_in = k_in.astype(BF16)
        keep = jnp.logical_and(same_chunk, (col <= row) if d == 0 else (col >= row))
        prep.append((q_dec, k_in, k_out, e_last, keep, order, v_ref))

    chunk_rows = [slice(i * CHUNK, (i + 1) * CHUNK) for i in range(cps)]
    heads = [slice(hd * D_KDIM, (hd + 1) * D_KDIM) for hd in range(D_HEADS)]
    for d, (q_dec, k_in, _, _, keep, _, v_ref) in enumerate(prep):
        for hs in heads:
            att = jnp.where(keep, _dot_nt(q_dec[:, hs], k_in[:, hs]), 0.0).astype(BF16)
            o_refs[d][:, hs] = _dot(att, v_ref[:, hs])
    for t in range(cps):
        for d, (q_dec, _, k_out, e_last, _, order, v_ref) in enumerate(prep):
            rs = chunk_rows[order[t]]
            for hd, hs in enumerate(heads):
                st = st_scr[d, hd]
                o_refs[d][rs, hs] += _dot_nt(q_dec[rs, hs], st.astype(BF16))
                st_scr[d, hd] = st * e_last[order[t]][:, hs] + _dot_tn(v_ref[rs, hs], k_out[rs, hs])

    @pl.when(jnp.logical_and(n == per - 1, jnp.logical_not(lat)))
    def _():
        for d in range(2):
            for hd in range(D_HEADS):
                fin_ref[d, hd] = st_scr[d, hd].T


def _hgrn_scan(geo, tri, qs, lff, lfb, v, s0):
    cps = tri.shape[1] // CHUNK
    rows = cps * CHUNK
    plan = _ScanPlan(geo, rows)
    fwd = pl.BlockSpec((rows, D_MODEL), lambda s: (plan.fwd_block(s), 0))
    bwd = pl.BlockSpec((rows, D_MODEL), lambda s: (plan.bwd_block(s), 0))
    st_shape = (2, D_HEADS, D_KDIM, D_KDIM)
    s0_spec = pl.BlockSpec((None,) + st_shape,
                           lambda s: (jnp.where(plan.split(s)[0], plan.split(s)[1], 0), 0, 0, 0, 0))
    fin_spec = pl.BlockSpec((None,) + st_shape,
                            lambda s: (jnp.where(plan.split(s)[0], plan.bp - 1, plan.split(s)[1]), 0, 0, 0, 0))
    o_shape = jax.ShapeDtypeStruct(qs.shape, F32)
    return pl.pallas_call(
        functools.partial(_hgrn_kernel, plan=plan, cps=cps),
        out_shape=(o_shape, o_shape, jax.ShapeDtypeStruct((geo.bp,) + st_shape, F32)),
        grid=(plan.s_p + plan.s_s,),
        in_specs=[pl.BlockSpec((2, rows, rows), lambda s: (0, 0, 0)), fwd, fwd, fwd, bwd, bwd, bwd, s0_spec],
        out_specs=(fwd, bwd, fin_spec),
        scratch_shapes=[pltpu.VMEM(st_shape, F32)],
        compiler_params=_params(1),
        name="hgrn_scan",
    )(tri, qs, lff, v, qs, lfb, v, s0)


def _ffn_chunks(d_ff):
    chunks, f0 = [], 0
    while f0 < d_ff:
        fs = min(COL_CHUNK, d_ff - f0)
        chunks.append((f0, fs))
        f0 += fs
    return tuple(chunks)


def _out_ffn_kernel(*refs, tm, n_p, n_x, hgrn, split_out, f_chunks):
    refs = list(refs)
    x_refs = refs[:n_x]
    del refs[:n_x]
    o_in = refs[:2]
    del refs[:2]
    if hgrn:
        gs_ref, gn_ref = refs[:2]
        del refs[:2]
    mod_ref, wo_ref, ng_ref, wg_ref, wu_ref, wd_ref = refs[:6]
    del refs[:6]
    n_out = 2 if split_out else 1
    out_refs = refs[:n_out]
    x1_scr, h_scr, acc_scr = refs[n_out:]
    is_sample = pl.program_id(0) * tm >= n_p
    mod = mod_ref[...]
    m = [mod[:, j * D_MODEL:(j + 1) * D_MODEL] for j in range(N_MOD)]

    if hgrn:
        o = o_in[0][...] + o_in[1][...]
        parts = []
        for hd in range(D_HEADS):
            hs = slice(hd * D_KDIM, (hd + 1) * D_KDIM)
            oh = o[:, hs]
            ms = jnp.mean(oh * oh, axis=-1, keepdims=True)
            parts.append(oh * lax.rsqrt(ms + EPS) * gn_ref[...] * gs_ref[:, hs])
        o = jnp.concatenate(parts, axis=-1).astype(BF16)
    else:
        o = _load_stream(o_in, is_sample)

    x1 = _load_stream(x_refs, is_sample) + m[2] * _dot(o, wo_ref[...])
    x1_scr[...] = x1
    h_scr[...] = _norm_mod(x1, ng_ref[...], m[3], m[4]).astype(BF16)
    for idx, (f0, fs) in enumerate(f_chunks):
        hb = h_scr[...]
        g = _dot(hb, wg_ref[:, f0:f0 + fs])
        u = _dot(hb, wu_ref[:, f0:f0 + fs])
        y = _dot((_silu(g) * u).astype(BF16), wd_ref[f0:f0 + fs, :])
        acc_scr[...] = y if idx == 0 else acc_scr[...] + y
    res = x1_scr[...] + m[5] * acc_scr[...]
    if split_out:
        @pl.when(jnp.logical_not(is_sample))
        def _():
            out_refs[0][...] = res

        @pl.when(is_sample)
        def _():
            out_refs[1][...] = res
    else:
        out_refs[0][...] = res


def _out_ffn(geo, x, o_parts, mod, w_o, norm_gain, layer, w_gate, w_up, w_down, gs=None, gn=None, split_out=False):
    tm = geo.tm
    hgrn = gs is not None
    d_ff = w_gate.shape[2]
    in_specs, args = geo.stream_specs(x)
    n_x = len(args)
    args += list(o_parts)
    if hgrn:
        in_specs += [geo.row_spec(), geo.row_spec(), geo.row_spec(), _vec_spec(D_KDIM)]
        args += [gs, gn]
    else:
        in_specs += [geo.prompt_spec(), geo.sample_spec()]
    in_specs += [geo.mod_spec(), _const_spec((D_MODEL, D_MODEL)), _vec_spec(D_MODEL),
                 _layer_spec((D_MODEL, d_ff), layer), _layer_spec((D_MODEL, d_ff), layer),
                 _layer_spec((d_ff, D_MODEL), layer)]
    args += [mod, w_o, norm_gain, w_gate, w_up, w_down]
    if split_out:
        out_shape = (jax.ShapeDtypeStruct((geo.n_p, D_MODEL), F32), jax.ShapeDtypeStruct((geo.n_s, D_MODEL), F32))
        out_specs = (geo.prompt_spec(), geo.sample_spec())
    else:
        out_shape = jax.ShapeDtypeStruct((geo.n_t, D_MODEL), F32)
        out_specs = geo.row_spec()
    kern = functools.partial(_out_ffn_kernel, tm=tm, n_p=geo.n_p, n_x=n_x, hgrn=hgrn, split_out=split_out,
                             f_chunks=_ffn_chunks(d_ff))
    return pl.pallas_call(
        kern,
        out_shape=out_shape,
        grid=(geo.n_t // tm,),
        in_specs=in_specs,
        out_specs=out_specs,
        scratch_shapes=[pltpu.VMEM((tm, D_MODEL), F32), pltpu.VMEM((tm, D_MODEL), BF16),
                        pltpu.VMEM((tm, D_MODEL), F32)],
        compiler_params=_params(1),
        name="out_ffn",
    )(*args)


def _rope_tables(n_tokens, n_identity):
    rows = n_tokens // GRID_W
    row = jnp.repeat(jnp.arange(rows, dtype=F32), GRID_W)
    col = jnp.tile(jnp.arange(GRID_W, dtype=F32), rows)
    axis_dim = HEAD_DIM // 2
    inv_freq = ROPE_THETA ** (-jnp.arange(0, axis_dim, 2, dtype=F32) / axis_dim)
    ang = jnp.stack([row[:, None] * inv_freq, col[:, None] * inv_freq], axis=1)
    cos, sin = jnp.cos(ang), jnp.sin(ang)
    cos64 = jnp.concatenate([cos, cos], axis=-1).reshape(n_tokens, HEAD_DIM)
    sin64 = jnp.concatenate([-sin, sin], axis=-1).reshape(n_tokens, HEAD_DIM)
    reps = COL_CHUNK // HEAD_DIM
    ident = jnp.ones((n_identity, COL_CHUNK), F32)
    return (jnp.concatenate([jnp.tile(cos64, (1, reps)), ident], axis=0),
            jnp.concatenate([jnp.tile(sin64, (1, reps)), 0.0 * ident], axis=0))


def _group_matrix():
    g = jnp.arange(NORM_CHUNK) // HEAD_DIM
    return (g[:, None] == g[None, :]).astype(BF16)


def _tri_matrices(cps):
    r = jnp.arange(cps * CHUNK)
    same = r[None, :] // CHUNK == r[:, None] // CHUNK
    lower = jnp.logical_and(same, r[None, :] <= r[:, None]).astype(BF16)
    return jnp.stack([lower, lower.T])


def kernel(x_prompt, x_sample, c, cache_a_k, cache_a_v, cache_b_k, cache_b_v, cache_c_k, cache_c_v, state_d, c_ctx, norm_mix, norm_ffn, w_ada, b_ada, w_ffn_gate, w_ffn_up, w_ffn_down, w_qkv_a, w_o_a, qn_a, kn_a, subln_a, lam_q1_a, lam_k1_a, lam_q2_a, lam_k2_a, w_qkv_b, w_o_b, qn_b, kn_b, sink_b, w_qkv_c, w_o_c, qn_c, kn_c, w_in_d, w_o_d, gn_d, lb_logits_d):
    bp, tp, _ = x_prompt.shape
    bs, ts, _ = x_sample.shape
    past = cache_a_k.shape[2]
    depth = w_ada.shape[0]
    geo = _Geom(bp, tp, bs, ts, past)
    n_p = geo.n_p

    cond = jnp.concatenate([c_ctx[None], c, jnp.zeros((MOD_ROWS - 1 - bs, D_MODEL), F32)], axis=0)
    mod_all = _ada_all(cond, w_ada, b_ada).reshape(depth, MOD_ROWS, 1, N_MOD * D_MODEL)

    cos, sin = _rope_tables(ts, geo.tm)
    gmat = _group_matrix()
    tri = _tri_matrices(math.gcd(HGRN_CHUNKS_PER_STEP, math.gcd(tp // CHUNK, ts // CHUNK)))
    p_lb = jax.nn.softmax(lb_logits_d.astype(F32), axis=1)
    lb_all = jnp.cumsum(p_lb, axis=1) - p_lb[:, :1]
    q_scale = HEAD_DIM ** -0.5 * LOG2E
    tile_gain = lambda g, s=1.0: jnp.tile(g * s, COL_CHUNK // HEAD_DIM)[None]
    bf = lambda w: w.astype(BF16)
    ffn_stacks = (bf(w_ffn_gate), bf(w_ffn_up), bf(w_ffn_down))

    x = (x_prompt.reshape(n_p, D_MODEL), x_sample.reshape(geo.n_s, D_MODEL))
    outs = {k: [] for k in ("ak", "av", "bk", "bv", "ck", "cv", "d")}
    for li in range(depth):
        kind, j = li % 4, li // 4
        mod = mod_all[li]
        gain = norm_mix[li][None]
        ffn_w = (norm_ffn[li][None], li) + ffn_stacks
        split_out = li == depth - 1
        if kind == 0:
            lam_init = 0.8 - 0.6 * math.exp(-0.3 * li)
            lam = (jnp.exp(jnp.sum(lam_q1_a[j] * lam_k1_a[j])) - jnp.exp(jnp.sum(lam_q2_a[j] * lam_k2_a[j]))
                   + lam_init).astype(F32).reshape(1)
            q, k, v, k32, v32 = _qkv_proj(geo, x, mod, gain, bf(w_qkv_a[j]), gmat, tile_gain(qn_a[j], q_scale),
                                          tile_gain(kn_a[j]), cos, sin, n_q=D_MODEL, n_k=D_MODEL, n_v=D_MODEL,
                                          v_hd=2 * HEAD_DIM)
            common = dict(R=2, lam=lam, subg=subln_a[j][None], lam_scale=1.0 - lam_init)
            o_p = _flash(q, k, v, row0=0, n_b=bp, t_len=tp, tq=tp, inner=True, name="attn_a_prompt", **common)
            ctx = (cache_a_k[:, j].reshape(bs, past, D_MODEL), cache_a_v[:, j].reshape(bs, past, D_MODEL))
            o_s = _flash(q, k, v, row0=n_p, n_b=bs, t_len=ts, tq=min(1024, ts), inner=False, ctx=ctx,
                         name="attn_a_sample", **common)
            outs["ak"].append(k32.reshape((bp, tp) + k32.shape[1:]))
            outs["av"].append(v32.reshape((bp, tp) + v32.shape[1:]))
            x = _out_ffn(geo, x, (o_p, o_s), mod, bf(w_o_a[j]), *ffn_w, split_out=split_out)
        elif kind in (1, 2):
            w_qkv, w_o, qn, kn = (w_qkv_b, w_o_b, qn_b, kn_b) if kind == 1 else (w_qkv_c, w_o_c, qn_c, kn_c)
            cache_k, cache_v = (cache_b_k, cache_b_v) if kind == 1 else (cache_c_k, cache_c_v)
            n_kv = (w_qkv.shape[-1] - D_MODEL) // 2
            q, k, v, k32, v32 = _qkv_proj(geo, x, mod, gain, bf(w_qkv[j]), gmat, tile_gain(qn[j], q_scale),
                                          tile_gain(kn[j]), cos, sin, n_q=D_MODEL, n_k=n_kv, n_v=n_kv,
                                          v_hd=HEAD_DIM)
            sink = sink_b[j].astype(F32) if kind == 1 else None
            ctx = (cache_k[:, j].reshape(bs, past, n_kv), cache_v[:, j].reshape(bs, past, n_kv))
            tag = "b" if kind == 1 else "c"
            o_p = _flash(q, k, v, row0=0, n_b=bp, t_len=tp, tq=tp, R=4, inner=True, sink=sink,
                         name=f"attn_{tag}_prompt")
            o_s = _flash(q, k, v, row0=n_p, n_b=bs, t_len=ts, tq=min(256, ts), R=4,
                         inner=False, ctx=ctx, sink=sink, window=kind == 1, hps=4, name=f"attn_{tag}_sample")
            outs[tag + "k"].append(k32.reshape((bp, tp) + k32.shape[1:]))
            outs[tag + "v"].append(v32.reshape((bp, tp) + v32.shape[1:]))
            x = _out_ffn(geo, x, (o_p, o_s), mod, bf(w_o[j]), *ffn_w, split_out=split_out)
        else:
            if isinstance(x, tuple):
                x = jnp.concatenate(x, axis=0)
            qs, lff, lfb, v, gs = _proj_d(geo, x, mod, gain, bf(w_in_d[j]), lb_all[:, li])
            o_f, o_b, fin = _hgrn_scan(geo, tri, qs, lff, lfb, v, jnp.swapaxes(state_d[:, j], -1, -2))
            outs["d"].append(fin)
            x = _out_ffn(geo, x, (o_f, o_b), mod, bf(w_o_d[j]), *ffn_w, gs=gs, gn=gn_d[j][None],
                         split_out=split_out)

    if not isinstance(x, tuple):
        x = (x[:n_p], x[n_p:])
    stack = lambda xs: jnp.stack(xs, axis=1)
    return (x[0].reshape(bp, tp, D_MODEL), x[1].reshape(bs, ts, D_MODEL), stack(outs["ak"]), stack(outs["av"]),
            stack(outs["bk"]), stack(outs["bv"]), stack(outs["ck"]), stack(outs["cv"]), stack(outs["d"]))
```

```python
import functools
import math

import jax
import jax.numpy as jnp
import numpy as np
from jax import lax
from jax.experimental import pallas as pl
from jax.experimental.pallas import tpu as pltpu

F32 = jnp.float32
BF16 = jnp.bfloat16

D_MODEL = 1024
HEAD_DIM = 64
GRID_W = 64
ROPE_THETA = 10000.0
WINDOW = 128
EPS = 1e-6
NEG_INF = -1e30
LOG2E = 1.4426950408889634
CHUNK = 64
HGRN_CHUNKS_PER_STEP = 4
D_KDIM = 128
D_HEADS = D_MODEL // D_KDIM
N_MOD = 6
MOD_ROWS = 8
COL_CHUNK = 512
NORM_CHUNK = 256
SUM_ROWS = 16
V7X_VMEM_BYTES = 64 * 1024 * 1024
VMEM_LIMIT = V7X_VMEM_BYTES - 8 * 1024 * 1024


def _dot(a, b):
    return jnp.dot(a, b, preferred_element_type=F32)


def _dot_nt(a, b):
    return lax.dot_general(a, b, (((1,), (1,)), ((), ())), preferred_element_type=F32)


def _dot_tn(a, b):
    return lax.dot_general(a, b, (((0,), (0,)), ((), ())), preferred_element_type=F32)


def _silu(x):
    return x * jax.nn.sigmoid(x)


def _params(n_axes, **kw):
    return pltpu.CompilerParams(dimension_semantics=("arbitrary",) * n_axes, vmem_limit_bytes=VMEM_LIMIT, **kw)


def _const_spec(shape):
    return pl.BlockSpec(shape, lambda *_: (0,) * len(shape), pipeline_mode=pl.Buffered(1))


def _layer_spec(shape, layer):
    return pl.BlockSpec((None,) + shape, lambda *_: (layer, 0, 0), pipeline_mode=pl.Buffered(1))


def _norm_mod(x, gain, shift, scale):
    ms = jnp.mean(x * x, axis=-1, keepdims=True)
    return (x * lax.rsqrt(ms + EPS) * gain) * (1.0 + scale) + shift


def _split3(x):
    hi = x.astype(BF16)
    r = x - hi.astype(F32)
    mid = r.astype(BF16)
    lo = (r - mid.astype(F32)).astype(BF16)
    return hi, mid, lo


def _ada_kernel(cond_ref, w_ref, b_ref, o_ref):
    a = _silu(cond_ref[...]).astype(BF16)
    o_ref[...] = _dot(a, w_ref[...].astype(BF16)) + b_ref[...]


def _ada_all(cond, w_ada, b_ada):
    depth, d, n = w_ada.shape
    tn = n // 4
    return pl.pallas_call(
        _ada_kernel,
        out_shape=jax.ShapeDtypeStruct((depth, MOD_ROWS, n), F32),
        grid=(depth, n // tn),
        in_specs=[
            pl.BlockSpec((MOD_ROWS, d), lambda l, j: (0, 0)),
            pl.BlockSpec((None, d, tn), lambda l, j: (l, 0, j)),
            pl.BlockSpec((None, 1, tn), lambda l, j: (l, 0, j)),
        ],
        out_specs=pl.BlockSpec((None, MOD_ROWS, tn), lambda l, j: (l, 0, j)),
        compiler_params=_params(2),
        name="ada_mod",
    )(cond, w_ada, b_ada.reshape(depth, 1, n))


class _Geom:
    def __init__(self, bp, tp, bs, ts, past):
        self.bp, self.tp, self.bs, self.ts, self.past = bp, tp, bs, ts, past
        self.n_p = bp * tp
        self.n_s = bs * ts
        self.n_t = self.n_p + self.n_s
        self.tm = math.gcd(512, math.gcd(self.n_p, ts))
        assert self.n_p % ts == 0 and ts % CHUNK == 0 and tp % CHUNK == 0
        assert bs + 1 <= MOD_ROWS

    def mod_row(self, i):
        return jnp.where(i * self.tm < self.n_p, 0, 1 + (i * self.tm - self.n_p) // self.ts)

    def prompt_tile(self, i):
        return jnp.minimum(i, self.n_p // self.tm - 1)

    def sample_tile(self, i):
        return jnp.maximum(i - self.n_p // self.tm, 0)

    def rope_tile(self, i):
        per_seq = self.ts // self.tm
        return jnp.where(i * self.tm < self.n_p, per_seq, self.sample_tile(i) % per_seq)

    def row_spec(self, width=D_MODEL):
        return pl.BlockSpec((self.tm, width), lambda i: (i, 0))

    def prompt_spec(self, width=D_MODEL):
        return pl.BlockSpec((self.tm, width), lambda i: (self.prompt_tile(i), 0))

    def sample_spec(self, width=D_MODEL):
        return pl.BlockSpec((self.tm, width), lambda i: (self.sample_tile(i), 0))

    def mod_spec(self):
        return pl.BlockSpec((None, 1, N_MOD * D_MODEL), lambda i: (self.mod_row(i), 0, 0))

    def stream_specs(self, x):
        if isinstance(x, tuple):
            return [self.prompt_spec(), self.sample_spec()], list(x)
        return [self.row_spec()], [x]


def _vec_spec(n):
    return pl.BlockSpec((1, n), lambda i: (0, 0))


def _load_stream(x_refs, is_sample):
    if len(x_refs) == 2:
        return jnp.where(is_sample, x_refs[1][...], x_refs[0][...])
    return x_refs[0][...]


def _qkv_kernel(*refs, tm, n_p, n_x, n_q, n_k, n_v):
    refs = list(refs)
    x_refs = refs[:n_x]
    del refs[:n_x]
    (mod_ref, gain_ref, w_ref, gmat_ref, qg_ref, kg_ref, cos_ref, sin_ref,
     q_ref, k_ref, v_ref, k32_ref, v32_ref, h_scr, k32_scr, v32_scr) = refs
    i = pl.program_id(0)
    is_sample = i * tm >= n_p
    is_prompt = jnp.logical_not(is_sample)
    mod = mod_ref[...]
    h = _norm_mod(_load_stream(x_refs, is_sample), gain_ref[...], mod[:, 0:D_MODEL], mod[:, D_MODEL:2 * D_MODEL])
    h_scr[...] = h.astype(BF16)

    def head_norm(y, gain):
        y2 = (y * y).astype(BF16)
        ms = jnp.concatenate([_dot(y2[:, c:c + NORM_CHUNK], gmat_ref[...])
                              for c in range(0, y.shape[1], NORM_CHUNK)], axis=1) * (1.0 / HEAD_DIM)
        return y * lax.rsqrt(ms + EPS) * gain

    def rope(y):
        w = y.shape[1]
        lane = lax.broadcasted_iota(jnp.int32, y.shape, 1)
        partner = jnp.where(lane % 32 < 16, pltpu.roll(y, w - 16, 1), pltpu.roll(y, 16, 1))
        return y * cos_ref[:, :w] + partner * sin_ref[:, :w]

    def store_qk(y, dst, stage, cs):
        dst[:, cs] = rope(y).astype(BF16)
        if stage is not None:
            stage[:, cs] = y

    def store_v(y, cs):
        v_ref[:, cs] = y.astype(BF16)
        v32_scr[:, cs] = y

    def proj(c0, width):
        return _dot(h_scr[...], w_ref[:, c0:c0 + width])

    for c0 in range(0, n_q, COL_CHUNK):
        store_qk(head_norm(proj(c0, COL_CHUNK), qg_ref[...]), q_ref, None, slice(c0, c0 + COL_CHUNK))
    if n_k + n_v == COL_CHUNK:
        y = proj(n_q, COL_CHUNK)
        store_qk(head_norm(y[:, :n_k], kg_ref[:, :n_k]), k_ref, k32_scr, slice(0, n_k))
        store_v(y[:, n_k:], slice(0, n_v))
    else:
        for c0 in range(0, n_k, COL_CHUNK):
            cs = slice(c0, c0 + COL_CHUNK)
            store_qk(head_norm(proj(n_q + c0, COL_CHUNK), kg_ref[...]), k_ref, k32_scr, cs)
        for c0 in range(0, n_v, COL_CHUNK):
            store_v(proj(n_q + n_k + c0, COL_CHUNK), slice(c0, c0 + COL_CHUNK))

    @pl.when(is_prompt)
    def _():
        k32_ref[...] = k32_scr[...]
        v32_ref[...] = v32_scr[...]


def _qkv_proj(geo, x, mod, gain, w, gmat, q_gain, k_gain, cos, sin, *, n_q, n_k, n_v, v_hd):
    tm = geo.tm
    x_specs, x_args = geo.stream_specs(x)
    rope_spec = pl.BlockSpec((tm, COL_CHUNK), lambda i: (geo.rope_tile(i), 0))
    kern = functools.partial(_qkv_kernel, tm=tm, n_p=geo.n_p, n_x=len(x_args), n_q=n_q, n_k=n_k, n_v=n_v)
    q, k, v, k32, v32 = pl.pallas_call(
        kern,
        out_shape=(
            jax.ShapeDtypeStruct((geo.n_t, n_q), BF16),
            jax.ShapeDtypeStruct((geo.n_t, n_k), BF16),
            jax.ShapeDtypeStruct((geo.n_t, n_v), BF16),
            jax.ShapeDtypeStruct((geo.n_p, n_k), F32),
            jax.ShapeDtypeStruct((geo.n_p, n_v), F32),
        ),
        grid=(geo.n_t // tm,),
        in_specs=x_specs + [
            geo.mod_spec(),
            _vec_spec(D_MODEL),
            _const_spec((D_MODEL, n_q + n_k + n_v)),
            _const_spec((NORM_CHUNK, NORM_CHUNK)),
            _vec_spec(COL_CHUNK),
            _vec_spec(COL_CHUNK),
            rope_spec,
            rope_spec,
        ],
        out_specs=(geo.row_spec(n_q), geo.row_spec(n_k), geo.row_spec(n_v),
                   geo.prompt_spec(n_k), geo.prompt_spec(n_v)),
        scratch_shapes=[pltpu.VMEM((tm, D_MODEL), BF16), pltpu.VMEM((tm, n_k), F32), pltpu.VMEM((tm, n_v), F32)],
        compiler_params=_params(1),
        name="qkv_proj",
    )(*x_args, mod, gain, w, gmat, q_gain, k_gain, cos, sin)
    return q, k, v, k32.reshape(geo.n_p, n_k // HEAD_DIM, HEAD_DIM), v32.reshape(geo.n_p, n_v // v_hd, v_hd)


def _flash_kernel(*refs, R, tq, kb, n_own, t_len, n_inner, hps, has_ctx, use_sink, window, diff, lam_scale):
    refs = list(refs)
    q_ref, k_ref, v_ref = refs[:3]
    del refs[:3]
    if has_ctx:
        ck_ref, cv_ref = refs[:2]
        del refs[:2]
    if use_sink:
        sink_ref = refs.pop(0)
    if diff:
        lam_ref, subg_ref = refs[:2]
        del refs[:2]
    o_ref, qs_scr = refs[:2]
    del refs[:2]
    cached = n_inner == 1
    if cached:
        vt_scr = refs.pop(0)
    if has_ctx:
        ctk_scr, ctvt_scr = refs

    W = HEAD_DIM * R
    M = R * tq
    DV = 2 * HEAD_DIM if diff else HEAD_DIM
    qi = pl.program_id(2)

    def transposed(v):
        return v.astype(F32).T.astype(BF16)

    if cached:
        first = qi == 0 if diff else jnp.logical_and(qi == 0, pl.program_id(1) == 0)

        def fill(dst, vt):
            if diff:
                dst[...] = vt
            else:
                for g in range(vt.shape[0] // DV):
                    dst[g, :DV, :] = vt[g * DV:(g + 1) * DV]
                    dst[g, DV:, :] = jnp.ones((SUM_ROWS, vt.shape[1]), BF16)

        @pl.when(first)
        def _():
            fill(vt_scr, transposed(v_ref[...]))
            if has_ctx:
                ctk_scr[...] = ck_ref[...].astype(BF16)
                fill(ctvt_scr, cv_ref[...].T.astype(BF16))
    else:
        vt_all = transposed(v_ref[...])

    def scores(kblk, mask, qs):
        st = _dot_nt(kblk, qs[...])
        if mask is not None:
            st = jnp.where(mask, st, NEG_INF)
        return st

    def update(st, vt, state, sink_row):
        m_new = jnp.max(st, axis=0, keepdims=True)
        if state is not None:
            m_prev, l_prev, acc_prev = state
            m_new = jnp.maximum(m_new, m_prev)
        elif sink_row is not None:
            m_new = jnp.maximum(m_new, sink_row)
        p = jnp.exp2(st - m_new)
        acc = _dot(vt, p.astype(BF16))
        if diff:
            l_new = jnp.sum(p, axis=0, keepdims=True)
        else:
            l_new, acc = acc[DV:DV + 1], acc[:DV]
        if state is not None:
            alpha = jnp.exp2(m_prev - m_new)
            l_new = alpha * l_prev + l_new
            acc = alpha * acc_prev + acc
        elif sink_row is not None:
            l_new = l_new + jnp.exp2(sink_row - m_new)
        return m_new, l_new, acc

    def head(c, i):
        static = not cached
        q_cols = slice(i * W, (i + 1) * W)
        k_cols = slice(i * W, (i + 1) * W) if diff else slice(None)
        v_rows = slice(i * DV, (i + 1) * DV)
        q = q_ref[:, q_cols].astype(F32)
        lane = lax.broadcasted_iota(jnp.int32, q.shape, 1) // HEAD_DIM
        stacked = []
        for r in range(R):
            t = jnp.where(lane == r, q, 0.0)
            if not diff:
                if static:
                    if (c - r) % R:
                        t = pltpu.roll(t, HEAD_DIM * ((c - r) % R), 1)
                else:
                    t = pltpu.roll(t, HEAD_DIM * ((c - r + R) % R), 1)
            stacked.append(t.astype(BF16))
        if static:
            qs = jnp.concatenate(stacked, axis=0)
        else:
            qs = qs_scr.at[i]
            for r in range(R):
                qs[r * tq:(r + 1) * tq, :] = stacked[r]

        def vt_block(off):
            if static:
                vt = vt_all[c * DV:(c + 1) * DV, off:off + kb]
                return vt if diff else jnp.concatenate([vt, jnp.ones((SUM_ROWS, kb), BF16)], axis=0)
            return vt_scr[v_rows, pl.ds(off, kb)] if diff else vt_scr[c, :, pl.ds(off, kb)]

        sink_row = None
        if use_sink:
            sink_row = jnp.concatenate(
                [jnp.full((1, tq), sink_ref[c * R + r] * LOG2E, F32) for r in range(R)], axis=1)

        blocks = []
        if has_ctx:
            blocks.append((lambda: ctk_scr[:, k_cols], lambda: ctvt_scr[v_rows, :] if diff else ctvt_scr[c], None))
        if window:
            q0 = qi * tq
            start = pl.multiple_of(jnp.clip(q0 - WINDOW, 0, t_len - kb), WINDOW)
            kpos = start + lax.broadcasted_iota(jnp.int32, (kb, M), 0)
            qpos = q0 + lax.broadcasted_iota(jnp.int32, (kb, M), 1) % tq
            blocks.append((lambda: k_ref[pl.ds(start, kb), k_cols], lambda: vt_block(start),
                           jnp.abs(qpos - kpos) <= WINDOW))
        else:
            for j in range(n_own):
                blocks.append((lambda j=j: k_ref[j * kb:(j + 1) * kb, k_cols], lambda j=j: vt_block(j * kb), None))

        state = None
        st_next = scores(blocks[0][0](), blocks[0][2], qs)
        yield
        for j, (_, vt_get, _) in enumerate(blocks):
            st = st_next
            if j + 1 < len(blocks):
                st_next = scores(blocks[j + 1][0](), blocks[j + 1][2], qs)
            state = update(st, vt_get(), state, sink_row if j == 0 else None)
            yield

        _, l_fin, acc = state
        a = acc * (1.0 / l_fin)
        if diff:
            ot = a[:, :tq] - lam_ref[0] * a[:, tq:]
            ms = jnp.mean(ot * ot, axis=0, keepdims=True)
            o = (ot * lax.rsqrt(ms + EPS)).T * (subg_ref[...] * lam_scale)
        else:
            o = jnp.concatenate([a[:, r * tq:(r + 1) * tq] for r in range(R)], axis=0).T
        o_ref[:, q_cols] = o.astype(BF16)

    if cached:
        running = [head(pl.program_id(1) * hps + i, i) for i in range(hps)]
    else:
        running = [head(c, c) for c in range(n_inner)]
    while running:
        running = [g for g in running if next(g, "done") != "done"]


def _flash(q, k, v, *, row0, n_b, t_len, tq, R, inner, ctx=None, sink=None, window=False,
           lam=None, subg=None, lam_scale=1.0, hps=1, name):
    diff = lam is not None
    W = HEAD_DIM * R
    w_kv = k.shape[1]
    n_c = D_MODEL // W
    n_qb = t_len // tq
    if window:
        kb, n_own = tq + 2 * WINDOW, 1
    else:
        kb = min(t_len, 512)
        n_own = t_len // kb
    q_base, k_base = row0 // tq, row0 // t_len
    M = R * tq
    assert hps == 1 or not inner
    scratch = [pltpu.VMEM((hps, M, W), BF16)]
    if inner:
        assert n_qb == 1 and n_own == 1 and ctx is None
        grid = (n_b, 1, 1)
        q_spec = pl.BlockSpec((tq, D_MODEL), lambda b, c, qi: (q_base + b, 0))
        kv_spec = pl.BlockSpec((t_len, w_kv), lambda b, c, qi: (k_base + b, 0))
        o_spec = pl.BlockSpec((tq, D_MODEL), lambda b, c, qi: (b, 0))
        n_inner = n_c
    else:
        grid = (n_b, n_c // hps, n_qb)
        q_spec = pl.BlockSpec((tq, W * hps), lambda b, c, qi: (q_base + b * n_qb + qi, c))
        o_spec = pl.BlockSpec((tq, W * hps), lambda b, c, qi: (b * n_qb + qi, c))
        if diff:
            kv_spec = pl.BlockSpec((t_len, W * hps), lambda b, c, qi: (k_base + b, c))
            scratch.append(pltpu.VMEM((W * hps, t_len), BF16))
        else:
            kv_spec = pl.BlockSpec((t_len, w_kv), lambda b, c, qi: (k_base + b, 0))
            scratch.append(pltpu.VMEM((w_kv // HEAD_DIM, HEAD_DIM + SUM_ROWS, t_len), BF16))
        n_inner = 1
    in_specs = [q_spec, kv_spec, kv_spec]
    args = [q, k, v]
    if ctx is not None:
        ck, cv = ctx
        p_len = ck.shape[1]
        w_ctx = W * hps if diff else w_kv
        cmap = (lambda b, c, qi: (b, 0, c)) if diff else (lambda b, c, qi: (b, 0, 0))
        ctvt_shape = (w_ctx, p_len) if diff else (w_ctx // HEAD_DIM, HEAD_DIM + SUM_ROWS, p_len)
        scratch += [pltpu.VMEM((p_len, w_ctx), BF16), pltpu.VMEM(ctvt_shape, BF16)]
        in_specs += [pl.BlockSpec((None, p_len, w_ctx), cmap)] * 2
        args += [ck, cv]
    if sink is not None:
        in_specs.append(pl.BlockSpec(memory_space=pltpu.SMEM))
        args.append(sink)
    if diff:
        in_specs += [pl.BlockSpec(memory_space=pltpu.SMEM), pl.BlockSpec((1, W), lambda b, c, qi: (0, 0))]
        args += [lam, subg]
    kern = functools.partial(
        _flash_kernel, R=R, tq=tq, kb=kb, n_own=n_own, t_len=t_len, n_inner=n_inner, hps=hps,
        has_ctx=ctx is not None,
        use_sink=sink is not None, window=window, diff=diff, lam_scale=lam_scale)
    return pl.pallas_call(
        kern,
        out_shape=jax.ShapeDtypeStruct((n_b * t_len, D_MODEL), BF16),
        grid=grid,
        in_specs=in_specs,
        out_specs=o_spec,
        scratch_shapes=scratch,
        compiler_params=_params(3),
        name=name,
    )(*args)


def _proj_d_kernel(x_ref, mod_ref, gain_ref, w_ref, lb_ref, qs_ref, lff_ref, lfb_ref, v_ref, gs_ref, h_scr):
    mod = mod_ref[...]
    h = _norm_mod(x_ref[...], gain_ref[...], mod[:, 0:D_MODEL], mod[:, D_MODEL:2 * D_MODEL])
    h_scr[...] = h.astype(BF16)

    def proj(part, c0):
        return _dot(h_scr[...], w_ref[:, part * D_MODEL + c0:part * D_MODEL + c0 + COL_CHUNK])

    for c0 in range(0, D_MODEL, COL_CHUNK):
        cs = slice(c0, c0 + COL_CHUNK)
        qs_ref[:, cs] = _silu(proj(0, c0))
        for d, ref in ((0, lff_ref), (1, lfb_ref)):
            lb = lb_ref[d:d + 1, cs]
            ref[:, cs] = jnp.log(lb + (1.0 - lb) * jax.nn.sigmoid(proj(1 + d, c0)))
        v_ref[:, cs] = proj(3, c0).astype(BF16)
        gs_ref[:, cs] = _silu(proj(4, c0))


def _proj_d(geo, x, mod, gain, w, lb):
    spec = geo.row_spec()
    f32_out = jax.ShapeDtypeStruct((geo.n_t, D_MODEL), F32)
    return pl.pallas_call(
        _proj_d_kernel,
        out_shape=(f32_out, f32_out, f32_out, jax.ShapeDtypeStruct((geo.n_t, D_MODEL), BF16), f32_out),
        grid=(geo.n_t // geo.tm,),
        in_specs=[spec, geo.mod_spec(), _vec_spec(D_MODEL),
                  _const_spec((D_MODEL, 5 * D_MODEL)), pl.BlockSpec((2, D_MODEL), lambda i: (0, 0))],
        out_specs=(spec,) * 5,
        scratch_shapes=[pltpu.VMEM((geo.tm, D_MODEL), BF16)],
        compiler_params=_params(1),
        name="hgrn_proj",
    )(x, mod, gain, w, lb)


class _ScanPlan:
    def __init__(self, geo, rows):
        self.npp, self.nps = geo.tp // rows, geo.ts // rows
        self.s_p, self.s_s = geo.bp * self.npp, geo.bs * self.nps
        self.bp = geo.bp

    def split(self, s):
        lat = s >= self.s_p
        r = s - self.s_p
        seq = jnp.where(lat, r // self.nps, s // self.npp)
        n = jnp.where(lat, r % self.nps, s % self.npp)
        per = jnp.where(lat, self.nps, self.npp)
        base = jnp.where(lat, self.s_p + seq * self.nps, seq * self.npp)
        return lat, seq, n, per, base

    def fwd_block(self, s):
        _, _, n, _, base = self.split(s)
        return base + n

    def bwd_block(self, s):
        _, _, n, per, base = self.split(s)
        return base + per - 1 - n


def _hgrn_kernel(tri_ref, qf_ref, lff_ref, vf_ref, qb_ref, lfb_ref, vb_ref, s0_ref, of_ref, ob_ref, fin_ref,
                 st_scr, *, plan, cps):
    ins = [(qf_ref, lff_ref, vf_ref), (qb_ref, lfb_ref, vb_ref)]
    o_refs = [of_ref, ob_ref]
    lat, _, n, per, _ = plan.split(pl.program_id(0))

    @pl.when(n == 0)
    def _():
        st_scr[...] = jnp.where(lat, s0_ref[...], 0.0)

    rows = cps * CHUNK
    row = lax.broadcasted_iota(jnp.int32, (rows, rows), 0)
    col = lax.broadcasted_iota(jnp.int32, (rows, rows), 1)
    same_chunk = row // CHUNK == col // CHUNK
    prep = []
    for d in range(2):
        qs_ref, lf_ref, v_ref = ins[d]
        tri = tri_ref[d]
        lf = lf_ref[...]
        cum = sum(_dot(tri, part) for part in _split3(lf))
        q_dec = (qs_ref[...] * jnp.exp(cum)).astype(BF16)
        k_in = (1.0 - jnp.exp(lf)) * jnp.exp(-cum)
        order = list(range(cps)) if d == 0 else list(reversed(range(cps)))
        e_last = [jnp.exp(cum[i * CHUNK + CHUNK - 1:(i + 1) * CHUNK] if d == 0 else cum[i * CHUNK:i * CHUNK + 1])
                  for i in range(cps)]
        k_out = jnp.concatenate([k_in[i * CHUNK:(i + 1) * CHUNK] * e_last[i] for i in range(cps)],
                                axis=0).astype(BF16)
        k_in = k_in.astype(BF16)
        keep = jnp.logical_and(same_chunk, (col <= row) if d == 0 else (col >= row))
        prep.append((q_dec, k_in, k_out, e_last, keep, order, v_ref))

    chunk_rows = [slice(i * CHUNK, (i + 1) * CHUNK) for i in range(cps)]
    heads = [slice(hd * D_KDIM, (hd + 1) * D_KDIM) for hd in range(D_HEADS)]
    for d, (q_dec, k_in, _, _, keep, _, v_ref) in enumerate(prep):
        for hs in heads:
            att = jnp.where(keep, _dot_nt(q_dec[:, hs], k_in[:, hs]), 0.0).astype(BF16)
            o_refs[d][:, hs] = _dot(att, v_ref[:, hs])
    for t in range(cps):
        for d, (q_dec, _, k_out, e_last, _, order, v_ref) in enumerate(prep):
            rs = chunk_rows[order[t]]
            for hd, hs in enumerate(heads):
                st = st_scr[d, hd]
                o_refs[d][rs, hs] += _dot_nt(q_dec[rs, hs], st.astype(BF16))
                st_scr[d, hd] = st * e_last[order[t]][:, hs] + _dot_tn(v_ref[rs, hs], k_out[rs, hs])

    @pl.when(jnp.logical_and(n == per - 1, jnp.logical_not(lat)))
    def _():
        for d in range(2):
            for hd in range(D_HEADS):
                fin_ref[d, hd] = st_scr[d, hd].T


def _hgrn_scan(geo, tri, qs, lff, lfb, v, s0):
    cps = tri.shape[1] // CHUNK
    rows = cps * CHUNK
    plan = _ScanPlan(geo, rows)
    fwd = pl.BlockSpec((rows, D_MODEL), lambda s: (plan.fwd_block(s), 0))
    bwd = pl.BlockSpec((rows, D_MODEL), lambda s: (plan.bwd_block(s), 0))
    st_shape = (2, D_HEADS, D_KDIM, D_KDIM)
    s0_spec = pl.BlockSpec((None,) + st_shape,
                           lambda s: (jnp.where(plan.split(s)[0], plan.split(s)[1], 0), 0, 0, 0, 0))
    fin_spec = pl.BlockSpec((None,) + st_shape,
                            lambda s: (jnp.where(plan.split(s)[0], plan.bp - 1, plan.split(s)[1]), 0, 0, 0, 0))
    o_shape = jax.ShapeDtypeStruct(qs.shape, F32)
    return pl.pallas_call(
        functools.partial(_hgrn_kernel, plan=plan, cps=cps),
        out_shape=(o_shape, o_shape, jax.ShapeDtypeStruct((geo.bp,) + st_shape, F32)),
        grid=(plan.s_p + plan.s_s,),
        in_specs=[pl.BlockSpec((2, rows, rows), lambda s: (0, 0, 0)), fwd, fwd, fwd, bwd, bwd, bwd, s0_spec],
        out_specs=(fwd, bwd, fin_spec),
        scratch_shapes=[pltpu.VMEM(st_shape, F32)],
        compiler_params=_params(1),
        name="hgrn_scan",
    )(tri, qs, lff, v, qs, lfb, v, s0)


def _ffn_chunks(d_ff):
    chunks, f0 = [], 0
    while f0 < d_ff:
        fs = min(COL_CHUNK, d_ff - f0)
        chunks.append((f0, fs))
        f0 += fs
    return tuple(chunks)


def _out_ffn_kernel(*refs, tm, n_p, n_x, hgrn, split_out, f_chunks):
    refs = list(refs)
    x_refs = refs[:n_x]
    del refs[:n_x]
    o_in = refs[:2]
    del refs[:2]
    if hgrn:
        gs_ref, gn_ref = refs[:2]
        del refs[:2]
    mod_ref, wo_ref, ng_ref, wg_ref, wu_ref, wd_ref = refs[:6]
    del refs[:6]
    n_out = 2 if split_out else 1
    out_refs = refs[:n_out]
    x1_scr, h_scr, acc_scr = refs[n_out:]
    is_sample = pl.program_id(0) * tm >= n_p
    mod = mod_ref[...]
    m = [mod[:, j * D_MODEL:(j + 1) * D_MODEL] for j in range(N_MOD)]

    if hgrn:
        o = o_in[0][...] + o_in[1][...]
        parts = []
        for hd in range(D_HEADS):
            hs = slice(hd * D_KDIM, (hd + 1) * D_KDIM)
            oh = o[:, hs]
            ms = jnp.mean(oh * oh, axis=-1, keepdims=True)
            parts.append(oh * lax.rsqrt(ms + EPS) * gn_ref[...] * gs_ref[:, hs])
        o = jnp.concatenate(parts, axis=-1).astype(BF16)
    else:
        o = _load_stream(o_in, is_sample)

    x1 = _load_stream(x_refs, is_sample) + m[2] * _dot(o, wo_ref[...])
    x1_scr[...] = x1
    h_scr[...] = _norm_mod(x1, ng_ref[...], m[3], m[4]).astype(BF16)
    for idx, (f0, fs) in enumerate(f_chunks):
        hb = h_scr[...]
        g = _dot(hb, wg_ref[:, f0:f0 + fs])
        u = _dot(hb, wu_ref[:, f0:f0 + fs])
        y = _dot((_silu(g) * u).astype(BF16), wd_ref[f0:f0 + fs, :])
        acc_scr[...] = y if idx == 0 else acc_scr[...] + y
    res = x1_scr[...] + m[5] * acc_scr[...]
    if split_out:
        @pl.when(jnp.logical_not(is_sample))
        def _():
            out_refs[0][...] = res

        @pl.when(is_sample)
        def _():
            out_refs[1][...] = res
    else:
        out_refs[0][...] = res


def _out_ffn(geo, x, o_parts, mod, w_o, norm_gain, layer, w_gate, w_up, w_down, gs=None, gn=None, split_out=False):
    tm = geo.tm
    hgrn = gs is not None
    d_ff = w_gate.shape[2]
    in_specs, args = geo.stream_specs(x)
    n_x = len(args)
    args += list(o_parts)
    if hgrn:
        in_specs += [geo.row_spec(), geo.row_spec(), geo.row_spec(), _vec_spec(D_KDIM)]
        args += [gs, gn]
    else:
        in_specs += [geo.prompt_spec(), geo.sample_spec()]
    in_specs += [geo.mod_spec(), _const_spec((D_MODEL, D_MODEL)), _vec_spec(D_MODEL),
                 _layer_spec((D_MODEL, d_ff), layer), _layer_spec((D_MODEL, d_ff), layer),
                 _layer_spec((d_ff, D_MODEL), layer)]
    args += [mod, w_o, norm_gain, w_gate, w_up, w_down]
    if split_out:
        out_shape = (jax.ShapeDtypeStruct((geo.n_p, D_MODEL), F32), jax.ShapeDtypeStruct((geo.n_s, D_MODEL), F32))
        out_specs = (geo.prompt_spec(), geo.sample_spec())
    else:
        out_shape = jax.ShapeDtypeStruct((geo.n_t, D_MODEL), F32)
        out_specs = geo.row_spec()
    kern = functools.partial(_out_ffn_kernel, tm=tm, n_p=geo.n_p, n_x=n_x, hgrn=hgrn, split_out=split_out,
                             f_chunks=_ffn_chunks(d_ff))
    return pl.pallas_call(
        kern,
        out_shape=out_shape,
        grid=(geo.n_t // tm,),
        in_specs=in_specs,
        out_specs=out_specs,
        scratch_shapes=[pltpu.VMEM((tm, D_MODEL), F32), pltpu.VMEM((tm, D_MODEL), BF16),
                        pltpu.VMEM((tm, D_MODEL), F32)],
        compiler_params=_params(1),
        name="out_ffn",
    )(*args)


def _rope_tables(n_tokens, n_identity):
    rows = n_tokens // GRID_W
    row = np.repeat(np.arange(rows, dtype=np.float32), GRID_W)
    col = np.tile(np.arange(GRID_W, dtype=np.float32), rows)
    axis_dim = HEAD_DIM // 2
    inv_freq = np.float32(ROPE_THETA) ** (-np.arange(0, axis_dim, 2, dtype=np.float32) / np.float32(axis_dim))
    ang = np.stack([row[:, None] * inv_freq, col[:, None] * inv_freq], axis=1).astype(np.float32)
    cos, sin = np.cos(ang), np.sin(ang)
    cos64 = np.concatenate([cos, cos], axis=-1).reshape(n_tokens, HEAD_DIM)
    sin64 = np.concatenate([-sin, sin], axis=-1).reshape(n_tokens, HEAD_DIM)
    reps = COL_CHUNK // HEAD_DIM
    ident = np.ones((n_identity, COL_CHUNK), np.float32)
    return (np.concatenate([np.tile(cos64, (1, reps)), ident], axis=0).astype(np.float32),
            np.concatenate([np.tile(sin64, (1, reps)), 0.0 * ident], axis=0).astype(np.float32))


def _group_matrix():
    g = np.arange(NORM_CHUNK) // HEAD_DIM
    return jnp.asarray(g[:, None] == g[None, :], BF16)


def _tri_matrices(cps):
    r = np.arange(cps * CHUNK)
    same = r[None, :] // CHUNK == r[:, None] // CHUNK
    lower = np.logical_and(same, r[None, :] <= r[:, None])
    return jnp.asarray(np.stack([lower, lower.T]), BF16)


def kernel(x_prompt, x_sample, c, cache_a_k, cache_a_v, cache_b_k, cache_b_v, cache_c_k, cache_c_v, state_d, c_ctx, norm_mix, norm_ffn, w_ada, b_ada, w_ffn_gate, w_ffn_up, w_ffn_down, w_qkv_a, w_o_a, qn_a, kn_a, subln_a, lam_q1_a, lam_k1_a, lam_q2_a, lam_k2_a, w_qkv_b, w_o_b, qn_b, kn_b, sink_b, w_qkv_c, w_o_c, qn_c, kn_c, w_in_d, w_o_d, gn_d, lb_logits_d):
    bp, tp, _ = x_prompt.shape
    bs, ts, _ = x_sample.shape
    past = cache_a_k.shape[2]
    depth = w_ada.shape[0]
    geo = _Geom(bp, tp, bs, ts, past)
    n_p = geo.n_p

    cond = jnp.concatenate([c_ctx[None], c, jnp.zeros((MOD_ROWS - 1 - bs, D_MODEL), F32)], axis=0)
    mod_all = _ada_all(cond, w_ada, b_ada).reshape(depth, MOD_ROWS, 1, N_MOD * D_MODEL)

    cos, sin = _rope_tables(ts, geo.tm)
    gmat = _group_matrix()
    tri = _tri_matrices(math.gcd(HGRN_CHUNKS_PER_STEP, math.gcd(tp // CHUNK, ts // CHUNK)))
    p_lb = jax.nn.softmax(lb_logits_d.astype(F32), axis=1)
    lb_all = jnp.cumsum(p_lb, axis=1) - p_lb[:, :1]
    q_scale = HEAD_DIM ** -0.5 * LOG2E
    tile_gain = lambda g, s=1.0: jnp.tile(g * s, COL_CHUNK // HEAD_DIM)[None]
    bf = lambda w: w.astype(BF16)
    ffn_stacks = (bf(w_ffn_gate), bf(w_ffn_up), bf(w_ffn_down))

    x = (x_prompt.reshape(n_p, D_MODEL), x_sample.reshape(geo.n_s, D_MODEL))
    outs = {k: [] for k in ("ak", "av", "bk", "bv", "ck", "cv", "d")}
    for li in range(depth):
        kind, j = li % 4, li // 4
        mod = mod_all[li]
        gain = norm_mix[li][None]
        ffn_w = (norm_ffn[li][None], li) + ffn_stacks
        split_out = li == depth - 1
        if kind == 0:
            lam_init = 0.8 - 0.6 * math.exp(-0.3 * li)
            lam = (jnp.exp(jnp.sum(lam_q1_a[j] * lam_k1_a[j])) - jnp.exp(jnp.sum(lam_q2_a[j] * lam_k2_a[j]))
                   + lam_init).astype(F32).reshape(1)
            q, k, v, k32, v32 = _qkv_proj(geo, x, mod, gain, bf(w_qkv_a[j]), gmat, tile_gain(qn_a[j], q_scale),
                                          tile_gain(kn_a[j]), cos, sin, n_q=D_MODEL, n_k=D_MODEL, n_v=D_MODEL,
                                          v_hd=2 * HEAD_DIM)
            common = dict(R=2, lam=lam, subg=subln_a[j][None], lam_scale=1.0 - lam_init)
            o_p = _flash(q, k, v, row0=0, n_b=bp, t_len=tp, tq=tp, inner=True, name="attn_a_prompt", **common)
            ctx = (cache_a_k[:, j].reshape(bs, past, D_MODEL), cache_a_v[:, j].reshape(bs, past, D_MODEL))
            o_s = _flash(q, k, v, row0=n_p, n_b=bs, t_len=ts, tq=min(1024, ts), inner=False, ctx=ctx,
                         name="attn_a_sample", **common)
            outs["ak"].append(k32.reshape((bp, tp) + k32.shape[1:]))
            outs["av"].append(v32.reshape((bp, tp) + v32.shape[1:]))
            x = _out_ffn(geo, x, (o_p, o_s), mod, bf(w_o_a[j]), *ffn_w, split_out=split_out)
        elif kind in (1, 2):
            w_qkv, w_o, qn, kn = (w_qkv_b, w_o_b, qn_b, kn_b) if kind == 1 else (w_qkv_c, w_o_c, qn_c, kn_c)
            cache_k, cache_v = (cache_b_k, cache_b_v) if kind == 1 else (cache_c_k, cache_c_v)
            n_kv = (w_qkv.shape[-1] - D_MODEL) // 2
            q, k, v, k32, v32 = _qkv_proj(geo, x, mod, gain, bf(w_qkv[j]), gmat, tile_gain(qn[j], q_scale),
                                          tile_gain(kn[j]), cos, sin, n_q=D_MODEL, n_k=n_kv, n_v=n_kv,
                                          v_hd=HEAD_DIM)
            sink = sink_b[j].astype(F32) if kind == 1 else None
            ctx = (cache_k[:, j].reshape(bs, past, n_kv), cache_v[:, j].reshape(bs, past, n_kv))
            tag = "b" if kind == 1 else "c"
            o_p = _flash(q, k, v, row0=0, n_b=bp, t_len=tp, tq=tp, R=4, inner=True, sink=sink,
                         name=f"attn_{tag}_prompt")
            o_s = _flash(q, k, v, row0=n_p, n_b=bs, t_len=ts, tq=min(256, ts), R=4,
                         inner=False, ctx=ctx, sink=sink, window=kind == 1, hps=4, name=f"attn_{tag}_sample")
            outs[tag + "k"].append(k32.reshape((bp, tp) + k32.shape[1:]))
            outs[tag + "v"].append(v32.reshape((bp, tp) + v32.shape[1:]))
            x = _out_ffn(geo, x, (o_p, o_s), mod, bf(w_o[j]), *ffn_w, split_out=split_out)
        else:
            if isinstance(x, tuple):
                x = jnp.concatenate(x, axis=0)
            qs, lff, lfb, v, gs = _proj_d(geo, x, mod, gain, bf(w_in_d[j]), lb_all[:, li])
            o_f, o_b, fin = _hgrn_scan(geo, tri, qs, lff, lfb, v, jnp.swapaxes(state_d[:, j], -1, -2))
            outs["d"].append(fin)
            x = _out_ffn(geo, x, (o_f, o_b), mod, bf(w_o_d[j]), *ffn_w, gs=gs, gn=gn_d[j][None],
                         split_out=split_out)

    if not isinstance(x, tuple):
        x = (x[:n_p], x[n_p:])
    stack = lambda xs: jnp.stack(xs, axis=1)
    return (x[0].reshape(bp, tp, D_MODEL), x[1].reshape(bs, ts, D_MODEL), stack(outs["ak"]), stack(outs["av"]),
            stack(outs["bk"]), stack(outs["bv"]), stack(outs["ck"]), stack(outs["cv"]), stack(outs["d"]))
```

```python
import functools
import math

import jax
import jax.numpy as jnp
import numpy as np
from jax import lax
from jax.experimental import pallas as pl
from jax.experimental.pallas import tpu as pltpu

F32 = jnp.float32
BF16 = jnp.bfloat16

D_MODEL = 1024
HEAD_DIM = 64
GRID_W = 64
ROPE_THETA = 10000.0
WINDOW = 128
EPS = 1e-6
NEG_INF = -1e30
LOG2E = 1.4426950408889634
CHUNK = 64
HGRN_CHUNKS_PER_STEP = 4
D_KDIM = 128
D_HEADS = D_MODEL // D_KDIM
N_MOD = 6
MOD_ROWS = 8
COL_CHUNK = 512
NORM_CHUNK = 256
SUM_ROWS = 16
V7X_VMEM_BYTES = 64 * 1024 * 1024
VMEM_LIMIT = V7X_VMEM_BYTES - 8 * 1024 * 1024


def _dot(a, b):
    return jnp.dot(a, b, preferred_element_type=F32)


def _dot_nt(a, b):
    return lax.dot_general(a, b, (((1,), (1,)), ((), ())), preferred_element_type=F32)


def _dot_tn(a, b):
    return lax.dot_general(a, b, (((0,), (0,)), ((), ())), preferred_element_type=F32)


def _silu(x):
    return x * jax.nn.sigmoid(x)


def _params(n_axes, **kw):
    return pltpu.CompilerParams(dimension_semantics=("arbitrary",) * n_axes, vmem_limit_bytes=VMEM_LIMIT, **kw)


def _const_spec(shape):
    return pl.BlockSpec(shape, lambda *_: (0,) * len(shape), pipeline_mode=pl.Buffered(1))


def _layer_spec(shape, layer):
    return pl.BlockSpec((None,) + shape, lambda *_: (layer, 0, 0), pipeline_mode=pl.Buffered(1))


def _norm_mod(x, gain, shift, scale):
    ms = jnp.mean(x * x, axis=-1, keepdims=True)
    return (x * lax.rsqrt(ms + EPS) * gain) * (1.0 + scale) + shift


def _split3(x):
    hi = x.astype(BF16)
    r = x - hi.astype(F32)
    mid = r.astype(BF16)
    lo = (r - mid.astype(F32)).astype(BF16)
    return hi, mid, lo


def _ada_kernel(cond_ref, w_ref, b_ref, o_ref):
    a = _silu(cond_ref[...]).astype(BF16)
    o_ref[...] = _dot(a, w_ref[...].astype(BF16)) + b_ref[...]


def _ada_all(cond, w_ada, b_ada):
    depth, d, n = w_ada.shape
    tn = n // 4
    return pl.pallas_call(
        _ada_kernel,
        out_shape=jax.ShapeDtypeStruct((depth, MOD_ROWS, n), F32),
        grid=(depth, n // tn),
        in_specs=[
            pl.BlockSpec((MOD_ROWS, d), lambda l, j: (0, 0)),
            pl.BlockSpec((None, d, tn), lambda l, j: (l, 0, j)),
            pl.BlockSpec((None, 1, tn), lambda l, j: (l, 0, j)),
        ],
        out_specs=pl.BlockSpec((None, MOD_ROWS, tn), lambda l, j: (l, 0, j)),
        compiler_params=_params(2),
        name="ada_mod",
    )(cond, w_ada, b_ada.reshape(depth, 1, n))


class _Geom:
    def __init__(self, bp, tp, bs, ts, past):
        self.bp, self.tp, self.bs, self.ts, self.past = bp, tp, bs, ts, past
        self.n_p = bp * tp
        self.n_s = bs * ts
        self.n_t = self.n_p + self.n_s
        self.tm = math.gcd(512, math.gcd(self.n_p, ts))
        assert self.n_p % ts == 0 and ts % CHUNK == 0 and tp % CHUNK == 0
        assert bs + 1 <= MOD_ROWS

    def mod_row(self, i):
        return jnp.where(i * self.tm < self.n_p, 0, 1 + (i * self.tm - self.n_p) // self.ts)

    def prompt_tile(self, i):
        return jnp.minimum(i, self.n_p // self.tm - 1)

    def sample_tile(self, i):
        return jnp.maximum(i - self.n_p // self.tm, 0)

    def rope_tile(self, i):
        per_seq = self.ts // self.tm
        return jnp.where(i * self.tm < self.n_p, per_seq, self.sample_tile(i) % per_seq)

    def row_spec(self, width=D_MODEL):
        return pl.BlockSpec((self.tm, width), lambda i: (i, 0))

    def prompt_spec(self, width=D_MODEL):
        return pl.BlockSpec((self.tm, width), lambda i: (self.prompt_tile(i), 0))

    def sample_spec(self, width=D_MODEL):
        return pl.BlockSpec((self.tm, width), lambda i: (self.sample_tile(i), 0))

    def mod_spec(self):
        return pl.BlockSpec((None, 1, N_MOD * D_MODEL), lambda i: (self.mod_row(i), 0, 0))

    def stream_specs(self, x):
        if isinstance(x, tuple):
            return [self.prompt_spec(), self.sample_spec()], list(x)
        return [self.row_spec()], [x]


def _vec_spec(n):
    return pl.BlockSpec((1, n), lambda i: (0, 0))


def _load_stream(x_refs, is_sample):
    if len(x_refs) == 2:
        return jnp.where(is_sample, x_refs[1][...], x_refs[0][...])
    return x_refs[0][...]


def _qkv_kernel(*refs, tm, n_p, n_x, n_q, n_k, n_v):
    refs = list(refs)
    x_refs = refs[:n_x]
    del refs[:n_x]
    (mod_ref, gain_ref, w_ref, gmat_ref, qg_ref, kg_ref, cos_ref, sin_ref,
     q_ref, k_ref, v_ref, k32_ref, v32_ref, h_scr, k32_scr, v32_scr) = refs
    i = pl.program_id(0)
    is_sample = i * tm >= n_p
    is_prompt = jnp.logical_not(is_sample)
    mod = mod_ref[...]
    h = _norm_mod(_load_stream(x_refs, is_sample), gain_ref[...], mod[:, 0:D_MODEL], mod[:, D_MODEL:2 * D_MODEL])
    h_scr[...] = h.astype(BF16)

    def head_norm(y, gain):
        y2 = (y * y).astype(BF16)
        ms = jnp.concatenate([_dot(y2[:, c:c + NORM_CHUNK], gmat_ref[...])
                              for c in range(0, y.shape[1], NORM_CHUNK)], axis=1) * (1.0 / HEAD_DIM)
        return y * lax.rsqrt(ms + EPS) * gain

    def rope(y):
        w = y.shape[1]
        lane = lax.broadcasted_iota(jnp.int32, y.shape, 1)
        partner = jnp.where(lane % 32 < 16, pltpu.roll(y, w - 16, 1), pltpu.roll(y, 16, 1))
        return y * cos_ref[:, :w] + partner * sin_ref[:, :w]

    def store_qk(y, dst, stage, cs):
        dst[:, cs] = rope(y).astype(BF16)
        if stage is not None:
            stage[:, cs] = y

    def store_v(y, cs):
        v_ref[:, cs] = y.astype(BF16)
        v32_scr[:, cs] = y

    def proj(c0, width):
        return _dot(h_scr[...], w_ref[:, c0:c0 + width])

    for c0 in range(0, n_q, COL_CHUNK):
        store_qk(head_norm(proj(c0, COL_CHUNK), qg_ref[...]), q_ref, None, slice(c0, c0 + COL_CHUNK))
    if n_k + n_v == COL_CHUNK:
        y = proj(n_q, COL_CHUNK)
        store_qk(head_norm(y[:, :n_k], kg_ref[:, :n_k]), k_ref, k32_scr, slice(0, n_k))
        store_v(y[:, n_k:], slice(0, n_v))
    else:
        for c0 in range(0, n_k, COL_CHUNK):
            cs = slice(c0, c0 + COL_CHUNK)
            store_qk(head_norm(proj(n_q + c0, COL_CHUNK), kg_ref[...]), k_ref, k32_scr, cs)
        for c0 in range(0, n_v, COL_CHUNK):
            store_v(proj(n_q + n_k + c0, COL_CHUNK), slice(c0, c0 + COL_CHUNK))

    @pl.when(is_prompt)
    def _():
        k32_ref[...] = k32_scr[...]
        v32_ref[...] = v32_scr[...]


def _qkv_proj(geo, x, mod, gain, w, gmat, q_gain, k_gain, cos, sin, *, n_q, n_k, n_v, v_hd):
    tm = geo.tm
    x_specs, x_args = geo.stream_specs(x)
    rope_spec = pl.BlockSpec((tm, COL_CHUNK), lambda i: (geo.rope_tile(i), 0))
    kern = functools.partial(_qkv_kernel, tm=tm, n_p=geo.n_p, n_x=len(x_args), n_q=n_q, n_k=n_k, n_v=n_v)
    q, k, v, k32, v32 = pl.pallas_call(
        kern,
        out_shape=(
            jax.ShapeDtypeStruct((geo.n_t, n_q), BF16),
            jax.ShapeDtypeStruct((geo.n_t, n_k), BF16),
            jax.ShapeDtypeStruct((geo.n_t, n_v), BF16),
            jax.ShapeDtypeStruct((geo.n_p, n_k), F32),
            jax.ShapeDtypeStruct((geo.n_p, n_v), F32),
        ),
        grid=(geo.n_t // tm,),
        in_specs=x_specs + [
            geo.mod_spec(),
            _vec_spec(D_MODEL),
            _const_spec((D_MODEL, n_q + n_k + n_v)),
            _const_spec((NORM_CHUNK, NORM_CHUNK)),
            _vec_spec(COL_CHUNK),
            _vec_spec(COL_CHUNK),
            rope_spec,
            rope_spec,
        ],
        out_specs=(geo.row_spec(n_q), geo.row_spec(n_k), geo.row_spec(n_v),
                   geo.prompt_spec(n_k), geo.prompt_spec(n_v)),
        scratch_shapes=[pltpu.VMEM((tm, D_MODEL), BF16), pltpu.VMEM((tm, n_k), F32), pltpu.VMEM((tm, n_v), F32)],
        compiler_params=_params(1),
        name="qkv_proj",
    )(*x_args, mod, gain, w, gmat, q_gain, k_gain, cos, sin)
    return q, k, v, k32.reshape(geo.n_p, n_k // HEAD_DIM, HEAD_DIM), v32.reshape(geo.n_p, n_v // v_hd, v_hd)


def _flash_kernel(*refs, R, tq, kb, n_own, t_len, n_inner, hps, spb, has_ctx, use_sink, window, diff,
                  lam_scale):
    refs = list(refs)
    q_ref, k_ref, v_ref = refs[:3]
    del refs[:3]
    if has_ctx:
        ck_ref, cv_ref = refs[:2]
        del refs[:2]
    if use_sink:
        sink_ref = refs.pop(0)
    if diff:
        lam_ref, subg_ref = refs[:2]
        del refs[:2]
    o_ref, qs_scr = refs[:2]
    del refs[:2]
    cached = n_inner == 1
    if cached:
        vt_scr = refs.pop(0)
    if has_ctx:
        ctk_scr, ctvt_scr = refs

    W = HEAD_DIM * R
    M = R * tq
    DV = 2 * HEAD_DIM if diff else HEAD_DIM
    qi = pl.program_id(2)

    def transposed(v):
        return v.astype(F32).T.astype(BF16)

    if cached:
        first = qi == 0 if diff else jnp.logical_and(qi == 0, pl.program_id(1) == 0)

        def fill(dst, vt):
            if diff:
                dst[...] = vt
            else:
                for g in range(vt.shape[0] // DV):
                    dst[g, :DV, :] = vt[g * DV:(g + 1) * DV]
                    dst[g, DV:, :] = jnp.ones((SUM_ROWS, vt.shape[1]), BF16)

        @pl.when(first)
        def _():
            fill(vt_scr, transposed(v_ref[...]))
            if has_ctx:
                ctk_scr[...] = ck_ref[...].astype(BF16)
                fill(ctvt_scr, cv_ref[...].T.astype(BF16))
    else:
        vt_all = transposed(v_ref[...])

    def scores(kblk, mask, qs):
        st = _dot_nt(kblk, qs[...])
        if mask is not None:
            st = jnp.where(mask, st, NEG_INF)
        return st

    def update(st, vt, state, sink_row):
        m_new = jnp.max(st, axis=0, keepdims=True)
        if state is not None:
            m_prev, l_prev, acc_prev = state
            m_new = jnp.maximum(m_new, m_prev)
        elif sink_row is not None:
            m_new = jnp.maximum(m_new, sink_row)
        p = jnp.exp2(st - m_new)
        acc = _dot(vt, p.astype(BF16))
        if diff:
            l_new = jnp.sum(p, axis=0, keepdims=True)
        else:
            l_new, acc = acc[DV:DV + 1], acc[:DV]
        if state is not None:
            alpha = jnp.exp2(m_prev - m_new)
            l_new = alpha * l_prev + l_new
            acc = alpha * acc_prev + acc
        elif sink_row is not None:
            l_new = l_new + jnp.exp2(sink_row - m_new)
        return m_new, l_new, acc

    def head(c, i, s=0):
        static = not cached
        q_rows = slice(s * tq, (s + 1) * tq)
        k0 = s * t_len
        q_cols = slice(i * W, (i + 1) * W)
        k_cols = slice(i * W, (i + 1) * W) if diff else slice(None)
        v_rows = slice(i * DV, (i + 1) * DV)
        q = q_ref[q_rows, q_cols].astype(F32)
        lane = lax.broadcasted_iota(jnp.int32, q.shape, 1) // HEAD_DIM
        stacked = []
        for r in range(R):
            t = jnp.where(lane == r, q, 0.0)
            if not diff:
                if static:
                    if (c - r) % R:
                        t = pltpu.roll(t, HEAD_DIM * ((c - r) % R), 1)
                else:
                    t = pltpu.roll(t, HEAD_DIM * ((c - r + R) % R), 1)
            stacked.append(t.astype(BF16))
        if static:
            qs = jnp.concatenate(stacked, axis=0)
        else:
            qs = qs_scr.at[i]
            for r in range(R):
                qs[r * tq:(r + 1) * tq, :] = stacked[r]

        def vt_block(off):
            if static:
                vt = vt_all[c * DV:(c + 1) * DV, k0 + off:k0 + off + kb]
                return vt if diff else jnp.concatenate([vt, jnp.ones((SUM_ROWS, kb), BF16)], axis=0)
            return vt_scr[v_rows, pl.ds(off, kb)] if diff else vt_scr[c, :, pl.ds(off, kb)]

        sink_row = None
        if use_sink:
            sink_row = jnp.concatenate(
                [jnp.full((1, tq), sink_ref[c * R + r] * LOG2E, F32) for r in range(R)], axis=1)

        blocks = []
        if has_ctx:
            blocks.append((lambda: ctk_scr[:, k_cols], lambda: ctvt_scr[v_rows, :] if diff else ctvt_scr[c], None))
        if window:
            q0 = qi * tq
            start = pl.multiple_of(jnp.clip(q0 - WINDOW, 0, t_len - kb), WINDOW)
            kpos = start + lax.broadcasted_iota(jnp.int32, (kb, M), 0)
            qpos = q0 + lax.broadcasted_iota(jnp.int32, (kb, M), 1) % tq
            blocks.append((lambda: k_ref[pl.ds(start, kb), k_cols], lambda: vt_block(start),
                           jnp.abs(qpos - kpos) <= WINDOW))
        else:
            for j in range(n_own):
                blocks.append((lambda j=j: k_ref[k0 + j * kb:k0 + (j + 1) * kb, k_cols],
                               lambda j=j: vt_block(j * kb), None))

        state = None
        st_next = scores(blocks[0][0](), blocks[0][2], qs)
        yield
        for j, (_, vt_get, _) in enumerate(blocks):
            st = st_next
            if j + 1 < len(blocks):
                st_next = scores(blocks[j + 1][0](), blocks[j + 1][2], qs)
            state = update(st, vt_get(), state, sink_row if j == 0 else None)
            yield

        _, l_fin, acc = state
        a = acc * (1.0 / l_fin)
        if diff:
            ot = a[:, :tq] - lam_ref[0] * a[:, tq:]
            ms = jnp.mean(ot * ot, axis=0, keepdims=True)
            o = (ot * lax.rsqrt(ms + EPS)).T * (subg_ref[...] * lam_scale)
        else:
            o = jnp.concatenate([a[:, r * tq:(r + 1) * tq] for r in range(R)], axis=0).T
        o_ref[q_rows, q_cols] = o.astype(BF16)

    if cached:
        running = [head(pl.program_id(1) * hps + i, i) for i in range(hps)]
    else:
        running = [head(c, c, s) for s in range(spb) for c in range(n_inner)]
    while running:
        running = [g for g in running if next(g, "done") != "done"]


def _flash(q, k, v, *, row0, n_b, t_len, tq, R, inner, ctx=None, sink=None, window=False,
           lam=None, subg=None, lam_scale=1.0, hps=1, spb=1, name):
    diff = lam is not None
    W = HEAD_DIM * R
    w_kv = k.shape[1]
    n_c = D_MODEL // W
    n_qb = t_len // tq
    if window:
        kb, n_own = tq + 2 * WINDOW, 1
    else:
        kb = min(t_len, 512)
        n_own = t_len // kb
    q_base, k_base = row0 // tq, row0 // t_len
    M = R * tq
    assert hps == 1 or not inner
    scratch = [pltpu.VMEM((hps, M, W), BF16)]
    if inner:
        assert n_qb == 1 and n_own == 1 and ctx is None and n_b % spb == 0 and row0 % (spb * t_len) == 0
        grid = (n_b // spb, 1, 1)
        first_blk = row0 // (spb * t_len)
        q_spec = pl.BlockSpec((spb * tq, D_MODEL), lambda b, c, qi: (first_blk + b, 0))
        kv_spec = pl.BlockSpec((spb * t_len, w_kv), lambda b, c, qi: (first_blk + b, 0))
        o_spec = pl.BlockSpec((spb * tq, D_MODEL), lambda b, c, qi: (b, 0))
        n_inner = n_c
    else:
        grid = (n_b, n_c // hps, n_qb)
        q_spec = pl.BlockSpec((tq, W * hps), lambda b, c, qi: (q_base + b * n_qb + qi, c))
        o_spec = pl.BlockSpec((tq, W * hps), lambda b, c, qi: (b * n_qb + qi, c))
        if diff:
            kv_spec = pl.BlockSpec((t_len, W * hps), lambda b, c, qi: (k_base + b, c))
            scratch.append(pltpu.VMEM((W * hps, t_len), BF16))
        else:
            kv_spec = pl.BlockSpec((t_len, w_kv), lambda b, c, qi: (k_base + b, 0))
            scratch.append(pltpu.VMEM((w_kv // HEAD_DIM, HEAD_DIM + SUM_ROWS, t_len), BF16))
        n_inner = 1
    in_specs = [q_spec, kv_spec, kv_spec]
    args = [q, k, v]
    if ctx is not None:
        ck, cv = ctx
        p_len = ck.shape[1]
        w_ctx = W * hps if diff else w_kv
        cmap = (lambda b, c, qi: (b, 0, c)) if diff else (lambda b, c, qi: (b, 0, 0))
        ctvt_shape = (w_ctx, p_len) if diff else (w_ctx // HEAD_DIM, HEAD_DIM + SUM_ROWS, p_len)
        scratch += [pltpu.VMEM((p_len, w_ctx), BF16), pltpu.VMEM(ctvt_shape, BF16)]
        in_specs += [pl.BlockSpec((None, p_len, w_ctx), cmap)] * 2
        args += [ck, cv]
    if sink is not None:
        in_specs.append(pl.BlockSpec(memory_space=pltpu.SMEM))
        args.append(sink)
    if diff:
        in_specs += [pl.BlockSpec(memory_space=pltpu.SMEM), pl.BlockSpec((1, W), lambda b, c, qi: (0, 0))]
        args += [lam, subg]
    kern = functools.partial(
        _flash_kernel, R=R, tq=tq, kb=kb, n_own=n_own, t_len=t_len, n_inner=n_inner, hps=hps, spb=spb,
        has_ctx=ctx is not None,
        use_sink=sink is not None, window=window, diff=diff, lam_scale=lam_scale)
    return pl.pallas_call(
        kern,
        out_shape=jax.ShapeDtypeStruct((n_b * t_len, D_MODEL), BF16),
        grid=grid,
        in_specs=in_specs,
        out_specs=o_spec,
        scratch_shapes=scratch,
        compiler_params=_params(3),
        name=name,
    )(*args)


def _proj_d_kernel(x_ref, mod_ref, gain_ref, w_ref, lb_ref, qs_ref, lff_ref, lfb_ref, v_ref, gs_ref, h_scr):
    mod = mod_ref[...]
    h = _norm_mod(x_ref[...], gain_ref[...], mod[:, 0:D_MODEL], mod[:, D_MODEL:2 * D_MODEL])
    h_scr[...] = h.astype(BF16)

    def proj(part, c0):
        return _dot(h_scr[...], w_ref[:, part * D_MODEL + c0:part * D_MODEL + c0 + COL_CHUNK])

    for c0 in range(0, D_MODEL, COL_CHUNK):
        cs = slice(c0, c0 + COL_CHUNK)
        qs_ref[:, cs] = _silu(proj(0, c0))
        for d, ref in ((0, lff_ref), (1, lfb_ref)):
            lb = lb_ref[d:d + 1, cs]
            ref[:, cs] = jnp.log(lb + (1.0 - lb) * jax.nn.sigmoid(proj(1 + d, c0)))
        v_ref[:, cs] = proj(3, c0).astype(BF16)
        gs_ref[:, cs] = _silu(proj(4, c0))


def _proj_d(geo, x, mod, gain, w, lb):
    spec = geo.row_spec()
    f32_out = jax.ShapeDtypeStruct((geo.n_t, D_MODEL), F32)
    return pl.pallas_call(
        _proj_d_kernel,
        out_shape=(f32_out, f32_out, f32_out, jax.ShapeDtypeStruct((geo.n_t, D_MODEL), BF16), f32_out),
        grid=(geo.n_t // geo.tm,),
        in_specs=[spec, geo.mod_spec(), _vec_spec(D_MODEL),
                  _const_spec((D_MODEL, 5 * D_MODEL)), pl.BlockSpec((2, D_MODEL), lambda i: (0, 0))],
        out_specs=(spec,) * 5,
        scratch_shapes=[pltpu.VMEM((geo.tm, D_MODEL), BF16)],
        compiler_params=_params(1),
        name="hgrn_proj",
    )(x, mod, gain, w, lb)


class _ScanPlan:
    def __init__(self, geo, rows):
        self.npp, self.nps = geo.tp // rows, geo.ts // rows
        self.s_p, self.s_s = geo.bp * self.npp, geo.bs * self.nps
        self.bp = geo.bp

    def split(self, s):
        lat = s >= self.s_p
        r = s - self.s_p
        seq = jnp.where(lat, r // self.nps, s // self.npp)
        n = jnp.where(lat, r % self.nps, s % self.npp)
        per = jnp.where(lat, self.nps, self.npp)
        base = jnp.where(lat, self.s_p + seq * self.nps, seq * self.npp)
        return lat, seq, n, per, base

    def fwd_block(self, s):
        _, _, n, _, base = self.split(s)
        return base + n

    def bwd_block(self, s):
        _, _, n, per, base = self.split(s)
        return base + per - 1 - n


def _hgrn_kernel(tri_ref, qf_ref, lff_ref, vf_ref, qb_ref, lfb_ref, vb_ref, s0_ref, of_ref, ob_ref, fin_ref,
                 st_scr, *, plan, cps):
    ins = [(qf_ref, lff_ref, vf_ref), (qb_ref, lfb_ref, vb_ref)]
    o_refs = [of_ref, ob_ref]
    lat, _, n, per, _ = plan.split(pl.program_id(0))

    @pl.when(n == 0)
    def _():
        st_scr[...] = jnp.where(lat, s0_ref[...], 0.0)

    rows = cps * CHUNK
    row = lax.broadcasted_iota(jnp.int32, (rows, rows), 0)
    col = lax.broadcasted_iota(jnp.int32, (rows, rows), 1)
    same_chunk = row // CHUNK == col // CHUNK
    prep = []
    for d in range(2):
        qs_ref, lf_ref, v_ref = ins[d]
        tri = tri_ref[d]
        lf = lf_ref[...]
        cum = sum(_dot(tri, part) for part in _split3(lf))
        q_dec = (qs_ref[...] * jnp.exp(cum)).astype(BF16)
        k_in = (1.0 - jnp.exp(lf)) * jnp.exp(-cum)
        order = list(range(cps)) if d == 0 else list(reversed(range(cps)))
        e_last = [jnp.exp(cum[i * CHUNK + CHUNK - 1:(i + 1) * CHUNK] if d == 0 else cum[i * CHUNK:i * CHUNK + 1])
                  for i in range(cps)]
        k_out = jnp.concatenate([k_in[i * CHUNK:(i + 1) * CHUNK] * e_last[i] for i in range(cps)],
                                axis=0).astype(BF16)
        k_in = k_in.astype(BF16)
        keep = jnp.logical_and(same_chunk, (col <= row) if d == 0 else (col >= row))
        prep.append((q_dec, k_in, k_out, e_last, keep, order, v_ref))

    chunk_rows = [slice(i * CHUNK, (i + 1) * CHUNK) for i in range(cps)]
    heads = [slice(hd * D_KDIM, (hd + 1) * D_KDIM) for hd in range(D_HEADS)]
    for d, (q_dec, k_in, _, _, keep, _, v_ref) in enumerate(prep):
        for hs in heads:
            att = jnp.where(keep, _dot_nt(q_dec[:, hs], k_in[:, hs]), 0.0).astype(BF16)
            o_refs[d][:, hs] = _dot(att, v_ref[:, hs])
    for t in range(cps):
        for d, (q_dec, _, k_out, e_last, _, order, v_ref) in enumerate(prep):
            rs = chunk_rows[order[t]]
            for hd, hs in enumerate(heads):
                st = st_scr[d, hd]
                o_refs[d][rs, hs] += _dot_nt(q_dec[rs, hs], st.astype(BF16))
                st_scr[d, hd] = st * e_last[order[t]][:, hs] + _dot_tn(v_ref[rs, hs], k_out[rs, hs])

    @pl.when(jnp.logical_and(n == per - 1, jnp.logical_not(lat)))
    def _():
        for d in range(2):
            for hd in range(D_HEADS):
                fin_ref[d, hd] = st_scr[d, hd].T


def _hgrn_scan(geo, tri, qs, lff, lfb, v, s0):
    cps = tri.shape[1] // CHUNK
    rows = cps * CHUNK
    plan = _ScanPlan(geo, rows)
    fwd = pl.BlockSpec((rows, D_MODEL), lambda s: (plan.fwd_block(s), 0))
    bwd = pl.BlockSpec((rows, D_MODEL), lambda s: (plan.bwd_block(s), 0))
    st_shape = (2, D_HEADS, D_KDIM, D_KDIM)
    s0_spec = pl.BlockSpec((None,) + st_shape,
                           lambda s: (jnp.where(plan.split(s)[0], plan.split(s)[1], 0), 0, 0, 0, 0))
    fin_spec = pl.BlockSpec((None,) + st_shape,
                            lambda s: (jnp.where(plan.split(s)[0], plan.bp - 1, plan.split(s)[1]), 0, 0, 0, 0))
    o_shape = jax.ShapeDtypeStruct(qs.shape, F32)
    return pl.pallas_call(
        functools.partial(_hgrn_kernel, plan=plan, cps=cps),
        out_shape=(o_shape, o_shape, jax.ShapeDtypeStruct((geo.bp,) + st_shape, F32)),
        grid=(plan.s_p + plan.s_s,),
        in_specs=[pl.BlockSpec((2, rows, rows), lambda s: (0, 0, 0)), fwd, fwd, fwd, bwd, bwd, bwd, s0_spec],
        out_specs=(fwd, bwd, fin_spec),
        scratch_shapes=[pltpu.VMEM(st_shape, F32)],
        compiler_params=_params(1),
        name="hgrn_scan",
    )(tri, qs, lff, v, qs, lfb, v, s0)


def _ffn_chunks(d_ff):
    chunks, f0 = [], 0
    while f0 < d_ff:
        fs = min(COL_CHUNK, d_ff - f0)
        chunks.append((f0, fs))
        f0 += fs
    return tuple(chunks)


def _out_ffn_kernel(*refs, tm, n_p, n_x, hgrn, split_out, f_chunks):
    refs = list(refs)
    x_refs = refs[:n_x]
    del refs[:n_x]
    o_in = refs[:2]
    del refs[:2]
    if hgrn:
        gs_ref, gn_ref = refs[:2]
        del refs[:2]
    mod_ref, wo_ref, ng_ref, wg_ref, wu_ref, wd_ref = refs[:6]
    del refs[:6]
    n_out = 2 if split_out else 1
    out_refs = refs[:n_out]
    x1_scr, h_scr, acc_scr = refs[n_out:]
    is_sample = pl.program_id(0) * tm >= n_p
    mod = mod_ref[...]
    m = [mod[:, j * D_MODEL:(j + 1) * D_MODEL] for j in range(N_MOD)]

    if hgrn:
        o = o_in[0][...] + o_in[1][...]
        parts = []
        for hd in range(D_HEADS):
            hs = slice(hd * D_KDIM, (hd + 1) * D_KDIM)
            oh = o[:, hs]
            ms = jnp.mean(oh * oh, axis=-1, keepdims=True)
            parts.append(oh * lax.rsqrt(ms + EPS) * gn_ref[...] * gs_ref[:, hs])
        o = jnp.concatenate(parts, axis=-1).astype(BF16)
    else:
        o = _load_stream(o_in, is_sample)

    x1 = _load_stream(x_refs, is_sample) + m[2] * _dot(o, wo_ref[...])
    x1_scr[...] = x1
    h_scr[...] = _norm_mod(x1, ng_ref[...], m[3], m[4]).astype(BF16)
    for idx, (f0, fs) in enumerate(f_chunks):
        hb = h_scr[...]
        g = _dot(hb, wg_ref[:, f0:f0 + fs])
        u = _dot(hb, wu_ref[:, f0:f0 + fs])
        y = _dot((_silu(g) * u).astype(BF16), wd_ref[f0:f0 + fs, :])
        acc_scr[...] = y if idx == 0 else acc_scr[...] + y
    res = x1_scr[...] + m[5] * acc_scr[...]
    if split_out:
        @pl.when(jnp.logical_not(is_sample))
        def _():
            out_refs[0][...] = res

        @pl.when(is_sample)
        def _():
            out_refs[1][...] = res
    else:
        out_refs[0][...] = res


def _out_ffn(geo, x, o_parts, mod, w_o, norm_gain, layer, w_gate, w_up, w_down, gs=None, gn=None, split_out=False):
    tm = geo.tm
    hgrn = gs is not None
    d_ff = w_gate.shape[2]
    in_specs, args = geo.stream_specs(x)
    n_x = len(args)
    args += list(o_parts)
    if hgrn:
        in_specs += [geo.row_spec(), geo.row_spec(), geo.row_spec(), _vec_spec(D_KDIM)]
        args += [gs, gn]
    else:
        in_specs += [geo.prompt_spec(), geo.sample_spec()]
    in_specs += [geo.mod_spec(), _const_spec((D_MODEL, D_MODEL)), _vec_spec(D_MODEL),
                 _layer_spec((D_MODEL, d_ff), layer), _layer_spec((D_MODEL, d_ff), layer),
                 _layer_spec((d_ff, D_MODEL), layer)]
    args += [mod, w_o, norm_gain, w_gate, w_up, w_down]
    if split_out:
        out_shape = (jax.ShapeDtypeStruct((geo.n_p, D_MODEL), F32), jax.ShapeDtypeStruct((geo.n_s, D_MODEL), F32))
        out_specs = (geo.prompt_spec(), geo.sample_spec())
    else:
        out_shape = jax.ShapeDtypeStruct((geo.n_t, D_MODEL), F32)
        out_specs = geo.row_spec()
    kern = functools.partial(_out_ffn_kernel, tm=tm, n_p=geo.n_p, n_x=n_x, hgrn=hgrn, split_out=split_out,
                             f_chunks=_ffn_chunks(d_ff))
    return pl.pallas_call(
        kern,
        out_shape=out_shape,
        grid=(geo.n_t // tm,),
        in_specs=in_specs,
        out_specs=out_specs,
        scratch_shapes=[pltpu.VMEM((tm, D_MODEL), F32), pltpu.VMEM((tm, D_MODEL), BF16),
                        pltpu.VMEM((tm, D_MODEL), F32)],
        compiler_params=_params(1),
        name="out_ffn",
    )(*args)


def _rope_tables(n_tokens, n_identity):
    rows = n_tokens // GRID_W
    row = np.repeat(np.arange(rows, dtype=np.float32), GRID_W)
    col = np.tile(np.arange(GRID_W, dtype=np.float32), rows)
    axis_dim = HEAD_DIM // 2
    inv_freq = np.float32(ROPE_THETA) ** (-np.arange(0, axis_dim, 2, dtype=np.float32) / np.float32(axis_dim))
    ang = np.stack([row[:, None] * inv_freq, col[:, None] * inv_freq], axis=1).astype(np.float32)
    cos, sin = np.cos(ang), np.sin(ang)
    cos64 = np.concatenate([cos, cos], axis=-1).reshape(n_tokens, HEAD_DIM)
    sin64 = np.concatenate([-sin, sin], axis=-1).reshape(n_tokens, HEAD_DIM)
    reps = COL_CHUNK // HEAD_DIM
    ident = np.ones((n_identity, COL_CHUNK), np.float32)
    return (np.concatenate([np.tile(cos64, (1, reps)), ident], axis=0).astype(np.float32),
            np.concatenate([np.tile(sin64, (1, reps)), 0.0 * ident], axis=0).astype(np.float32))


def _group_matrix():
    g = np.arange(NORM_CHUNK) // HEAD_DIM
    return jnp.asarray(g[:, None] == g[None, :], BF16)


def _tri_matrices(cps):
    r = np.arange(cps * CHUNK)
    same = r[None, :] // CHUNK == r[:, None] // CHUNK
    lower = np.logical_and(same, r[None, :] <= r[:, None])
    return jnp.asarray(np.stack([lower, lower.T]), BF16)


def kernel(x_prompt, x_sample, c, cache_a_k, cache_a_v, cache_b_k, cache_b_v, cache_c_k, cache_c_v, state_d, c_ctx, norm_mix, norm_ffn, w_ada, b_ada, w_ffn_gate, w_ffn_up, w_ffn_down, w_qkv_a, w_o_a, qn_a, kn_a, subln_a, lam_q1_a, lam_k1_a, lam_q2_a, lam_k2_a, w_qkv_b, w_o_b, qn_b, kn_b, sink_b, w_qkv_c, w_o_c, qn_c, kn_c, w_in_d, w_o_d, gn_d, lb_logits_d):
    bp, tp, _ = x_prompt.shape
    bs, ts, _ = x_sample.shape
    past = cache_a_k.shape[2]
    depth = w_ada.shape[0]
    geo = _Geom(bp, tp, bs, ts, past)
    n_p = geo.n_p

    cond = jnp.concatenate([c_ctx[None], c, jnp.zeros((MOD_ROWS - 1 - bs, D_MODEL), F32)], axis=0)
    mod_all = _ada_all(cond, w_ada, b_ada).reshape(depth, MOD_ROWS, 1, N_MOD * D_MODEL)

    cos, sin = _rope_tables(ts, geo.tm)
    gmat = _group_matrix()
    tri = _tri_matrices(math.gcd(HGRN_CHUNKS_PER_STEP, math.gcd(tp // CHUNK, ts // CHUNK)))
    p_lb = jax.nn.softmax(lb_logits_d.astype(F32), axis=1)
    lb_all = jnp.cumsum(p_lb, axis=1) - p_lb[:, :1]
    q_scale = HEAD_DIM ** -0.5 * LOG2E
    tile_gain = lambda g, s=1.0: jnp.tile(g * s, COL_CHUNK // HEAD_DIM)[None]
    bf = lambda w: w.astype(BF16)
    ffn_stacks = (bf(w_ffn_gate), bf(w_ffn_up), bf(w_ffn_down))

    x = (x_prompt.reshape(n_p, D_MODEL), x_sample.reshape(geo.n_s, D_MODEL))
    outs = {k: [] for k in ("ak", "av", "bk", "bv", "ck", "cv", "d")}
    for li in range(depth):
        kind, j = li % 4, li // 4
        mod = mod_all[li]
        gain = norm_mix[li][None]
        ffn_w = (norm_ffn[li][None], li) + ffn_stacks
        split_out = li == depth - 1
        if kind == 0:
            lam_init = 0.8 - 0.6 * math.exp(-0.3 * li)
            lam = (jnp.exp(jnp.sum(lam_q1_a[j] * lam_k1_a[j])) - jnp.exp(jnp.sum(lam_q2_a[j] * lam_k2_a[j]))
                   + lam_init).astype(F32).reshape(1)
            q, k, v, k32, v32 = _qkv_proj(geo, x, mod, gain, bf(w_qkv_a[j]), gmat, tile_gain(qn_a[j], q_scale),
                                          tile_gain(kn_a[j]), cos, sin, n_q=D_MODEL, n_k=D_MODEL, n_v=D_MODEL,
                                          v_hd=2 * HEAD_DIM)
            common = dict(R=2, lam=lam, subg=subln_a[j][None], lam_scale=1.0 - lam_init)
            o_p = _flash(q, k, v, row0=0, n_b=bp, t_len=tp, tq=tp, inner=True, spb=math.gcd(2, bp),
                         name="attn_a_prompt", **common)
            ctx = (cache_a_k[:, j].reshape(bs, past, D_MODEL), cache_a_v[:, j].reshape(bs, past, D_MODEL))
            o_s = _flash(q, k, v, row0=n_p, n_b=bs, t_len=ts, tq=min(1024, ts), inner=False, ctx=ctx,
                         name="attn_a_sample", **common)
            outs["ak"].append(k32.reshape((bp, tp) + k32.shape[1:]))
            outs["av"].append(v32.reshape((bp, tp) + v32.shape[1:]))
            x = _out_ffn(geo, x, (o_p, o_s), mod, bf(w_o_a[j]), *ffn_w, split_out=split_out)
        elif kind in (1, 2):
            w_qkv, w_o, qn, kn = (w_qkv_b, w_o_b, qn_b, kn_b) if kind == 1 else (w_qkv_c, w_o_c, qn_c, kn_c)
            cache_k, cache_v = (cache_b_k, cache_b_v) if kind == 1 else (cache_c_k, cache_c_v)
            n_kv = (w_qkv.shape[-1] - D_MODEL) // 2
            q, k, v, k32, v32 = _qkv_proj(geo, x, mod, gain, bf(w_qkv[j]), gmat, tile_gain(qn[j], q_scale),
                                          tile_gain(kn[j]), cos, sin, n_q=D_MODEL, n_k=n_kv, n_v=n_kv,
                                          v_hd=HEAD_DIM)
            sink = sink_b[j].astype(F32) if kind == 1 else None
            ctx = (cache_k[:, j].reshape(bs, past, n_kv), cache_v[:, j].reshape(bs, past, n_kv))
            tag = "b" if kind == 1 else "c"
            o_p = _flash(q, k, v, row0=0, n_b=bp, t_len=tp, tq=tp, R=4, inner=True, spb=math.gcd(2, bp), sink=sink,
                         name=f"attn_{tag}_prompt")
            o_s = _flash(q, k, v, row0=n_p, n_b=bs, t_len=ts, tq=min(256, ts), R=4,
                         inner=False, ctx=ctx, sink=sink, window=kind == 1, hps=4, name=f"attn_{tag}_sample")
            outs[tag + "k"].append(k32.reshape((bp, tp) + k32.shape[1:]))
            outs[tag + "v"].append(v32.reshape((bp, tp) + v32.shape[1:]))
            x = _out_ffn(geo, x, (o_p, o_s), mod, bf(w_o[j]), *ffn_w, split_out=split_out)
        else:
            if isinstance(x, tuple):
                x = jnp.concatenate(x, axis=0)
            qs, lff, lfb, v, gs = _proj_d(geo, x, mod, gain, bf(w_in_d[j]), lb_all[:, li])
            o_f, o_b, fin = _hgrn_scan(geo, tri, qs, lff, lfb, v, jnp.swapaxes(state_d[:, j], -1, -2))
            outs["d"].append(fin)
            x = _out_ffn(geo, x, (o_f, o_b), mod, bf(w_o_d[j]), *ffn_w, gs=gs, gn=gn_d[j][None],
                         split_out=split_out)

    if not isinstance(x, tuple):
        x = (x[:n_p], x[n_p:])
    stack = lambda xs: jnp.stack(xs, axis=1)
    return (x[0].reshape(bp, tp, D_MODEL), x[1].reshape(bs, ts, D_MODEL), stack(outs["ak"]), stack(outs["av"]),
            stack(outs["bk"]), stack(outs["bv"]), stack(outs["ck"]), stack(outs["cv"]), stack(outs["d"]))
```
